```python
import math
import jax, jax.numpy as jnp
from jax import lax
import numpy as np

D_MODEL = 2048
BATCH = 32
SEQ = 256
DEPTH = 1
DEC_BATCH = 4
DEC_SEQ = 4096
PAST_LEN = 256

GRID_W = 64
CHUNK = 128
D_A = D_MODEL // 2
N_GROUPS_A = 4
D_B = D_MODEL // 2
SSM_IN = 16
N_GROUPS_B = D_B // SSM_IN
SSM_STATE = 64
N_DIR = 2
D_FF = 4 * D_MODEL
N_MOD = 6
D_IN = 2 * D_A + D_B + 2 * D_MODEL
EPS = 1e-6
POS_BASE = 10000.0

kernel_name = "hybrid_gmlp_s5_diffusion_step"


def rmsnorm(x, g):
    xf = x.astype(jnp.float32)
    y = xf * lax.rsqrt(jnp.mean(xf * xf, axis=-1, keepdims=True) + EPS)
    return (y * g.astype(jnp.float32)).astype(x.dtype)


def modulate(h, shift, scale):
    return h * (1 + scale[:, None, :]) + shift[:, None, :]


def grid_pos_embed(n_tokens, dtype):
    rows = n_tokens // GRID_W
    quarter = D_MODEL // 4
    omega = 1.0 / (POS_BASE ** (jnp.arange(quarter, dtype=jnp.float32) / quarter))
    r = jnp.arange(rows, dtype=jnp.float32)[:, None] * omega
    col = jnp.arange(GRID_W, dtype=jnp.float32)[:, None] * omega
    e_r = jnp.concatenate([jnp.sin(r), jnp.cos(r)], axis=-1)
    e_c = jnp.concatenate([jnp.sin(col), jnp.cos(col)], axis=-1)
    emb = jnp.concatenate([
        jnp.broadcast_to(e_r[:, None, :], (rows, GRID_W, D_MODEL // 2)),
        jnp.broadcast_to(e_c[None, :, :], (rows, GRID_W, D_MODEL // 2))], axis=-1)
    return emb.reshape(n_tokens, D_MODEL).astype(dtype)


def chunk_mlp(u, v, g_sgu, w_s, b_s):
    bsz, n, _ = u.shape
    u = jax.nn.gelu(u)
    v = rmsnorm(jax.nn.gelu(v), g_sgu)
    vc = v.reshape(bsz, n // CHUNK, CHUNK, N_GROUPS_A, D_A // N_GROUPS_A)
    s = jnp.einsum('gij,bnjgc->bnigc', w_s, vc) + jnp.transpose(b_s)[None, None, :, :, None]
    return u * s.reshape(bsz, n, D_A)


def _cmul(ar, ai, br, bi):
    return ar * br - ai * bi, ar * bi + ai * br


def _scan_combine(e1, e2):
    a1r, a1i, b1r, b1i = e1
    a2r, a2i, b2r, b2i = e2
    ar, ai = _cmul(a2r, a2i, a1r, a1i)
    br, bi = _cmul(a2r, a2i, b1r, b1i)
    return ar, ai, br + b2r, bi + b2i


def s5_direction(u, h0_re, h0_im, a_re, a_im, log_dt, b_re, b_im, c_re, c_im):
    n = u.shape[0]
    dt = jnp.exp(log_dt)[:, None]
    mag = jnp.exp(a_re * dt)
    ab_re, ab_im = mag * jnp.cos(a_im * dt), mag * jnp.sin(a_im * dt)
    den = a_re * a_re + a_im * a_im
    f_re, f_im = _cmul(ab_re - 1.0, ab_im, a_re / den, -a_im / den)
    bu_re = jnp.einsum('lbgc,gpc->lbgp', u, b_re)
    bu_im = jnp.einsum('lbgc,gpc->lbgp', u, b_im)
    x_re, x_im = _cmul(f_re, f_im, bu_re, bu_im)
    i_re, i_im = _cmul(ab_re, ab_im, h0_re, h0_im)
    x_re = x_re.at[0].add(i_re)
    x_im = x_im.at[0].add(i_im)
    a_seq_re = jnp.broadcast_to(ab_re[None, None], (n, 1) + ab_re.shape)
    a_seq_im = jnp.broadcast_to(ab_im[None, None], (n, 1) + ab_im.shape)
    _, _, s_re, s_im = lax.associative_scan(_scan_combine, (a_seq_re, a_seq_im, x_re, x_im), axis=0)
    y = jnp.einsum('lbgp,gcp->lbgc', s_re, c_re) - jnp.einsum('lbgp,gcp->lbgc', s_im, c_im)
    return y, s_re[-1], s_im[-1]


def s5_mixer(xb, h0_re, h0_im, a_re, a_im, log_dt, b_re, b_im, c_re, c_im, d, w_glu, b_glu):
    f32 = jnp.float32
    bsz, n, _ = xb.shape
    xf = xb.astype(f32)
    u = jnp.transpose(xf.reshape(bsz, n, N_GROUPS_B, SSM_IN), (1, 0, 2, 3))
    a_re, a_im, log_dt = a_re.astype(f32), a_im.astype(f32), log_dt.astype(f32)
    b_re, b_im, c_re, c_im = b_re.astype(f32), b_im.astype(f32), c_re.astype(f32), c_im.astype(f32)
    h0_re, h0_im = h0_re.astype(f32), h0_im.astype(f32)
    y_f, fw_re, fw_im = s5_direction(u, h0_re[:, 0], h0_im[:, 0], a_re[0], a_im[0], log_dt[0],
                                     b_re[0], b_im[0], c_re[0], c_im[0])
    y_b, bw_re, bw_im = s5_direction(u[::-1], h0_re[:, 1], h0_im[:, 1], a_re[1], a_im[1], log_dt[1],
                                     b_re[1], b_im[1], c_re[1], c_im[1])
    y = y_f + y_b[::-1]
    y = jnp.transpose(y, (1, 0, 2, 3)).reshape(bsz, n, D_B) + d.astype(f32) * xf
    y = jax.nn.gelu(y).astype(xb.dtype)
    y = y * jax.nn.sigmoid(y @ w_glu + b_glu)
    return y, jnp.stack([fw_re, bw_re], axis=1), jnp.stack([fw_im, bw_im], axis=1)


def trunk_layer(x, mod, h0_re, h0_im, g_norm_mix, w_in, g_sgu, w_spatial, b_spatial,
                ssm_a_re, ssm_a_im, ssm_log_dt, ssm_b_re, ssm_b_im, ssm_c_re, ssm_c_im, ssm_d,
                w_glu, b_glu, w_proj_a, w_proj_b, w_out, g_norm_mlp, w_mlp_in, w_mlp_out):
    shift1, scale1, gate1, shift2, scale2, gate2 = jnp.split(mod, N_MOD, axis=-1)
    h = modulate(rmsnorm(x, g_norm_mix), shift1, scale1)
    z = h @ w_in
    u_a, v_a, x_b, gl_a, gl_b = jnp.split(
        z, [D_A, 2 * D_A, 2 * D_A + D_B, 2 * D_A + D_B + D_MODEL], axis=-1)
    y_a = chunk_mlp(u_a, v_a, g_sgu, w_spatial, b_spatial)
    y_b, s_re, s_im = s5_mixer(x_b, h0_re, h0_im, ssm_a_re, ssm_a_im, ssm_log_dt,
                               ssm_b_re, ssm_b_im, ssm_c_re, ssm_c_im, ssm_d, w_glu, b_glu)
    m = jax.nn.sigmoid(gl_a) * (y_a @ w_proj_a) + jax.nn.sigmoid(gl_b) * (y_b @ w_proj_b)
    x = x + gate1[:, None, :] * (m @ w_out)
    h2 = modulate(rmsnorm(x, g_norm_mlp), shift2, scale2)
    x = x + gate2[:, None, :] * (jnp.square(jax.nn.relu(h2 @ w_mlp_in)) @ w_mlp_out)
    return x, s_re, s_im


def setup_inputs(seed: int = 0) -> dict:
    key = jax.random.key(seed)
    ks = jax.random.split(key, 30)
    f32 = jnp.float32

    def nrm(k, shape, scale):
        return jax.random.normal(k, shape, f32) * scale

    G, P = N_GROUPS_B, SSM_STATE
    n_idx = jnp.arange(P, dtype=f32)
    ssm_shape = (DEPTH, N_DIR, G, P)
    return {
        "x_prompt": nrm(ks[0], (BATCH, SEQ, D_MODEL), 1.0),
        "x_sample": nrm(ks[1], (DEC_BATCH, DEC_SEQ, D_MODEL), 1.0),
        "state_ssm_re": nrm(ks[2], (DEC_BATCH, DEPTH, N_DIR, G, P), 0.5),
        "state_ssm_im": nrm(ks[3], (DEC_BATCH, DEPTH, N_DIR, G, P), 0.5),
        "c": nrm(ks[4], (DEC_BATCH, D_MODEL), 1.0),
        "c_ctx": nrm(ks[5], (D_MODEL,), 1.0),
        "w_ada": nrm(ks[6], (DEPTH, D_MODEL, N_MOD * D_MODEL), 0.5 * D_MODEL ** -0.5),
        "b_ada": nrm(ks[7], (DEPTH, N_MOD * D_MODEL), 0.02),
        "g_norm_mix": 1.0 + nrm(ks[8], (DEPTH, D_MODEL), 0.02),
        "w_in": nrm(ks[9], (DEPTH, D_MODEL, D_IN), D_MODEL ** -0.5),
        "g_sgu": 1.0 + nrm(ks[10], (DEPTH, D_A), 0.02),
        "w_spatial": nrm(ks[11], (DEPTH, N_GROUPS_A, CHUNK, CHUNK), CHUNK ** -0.5),
        "b_spatial": 1.0 + nrm(ks[12], (DEPTH, N_GROUPS_A, CHUNK), 0.02),
        "ssm_a_re": -0.5 + nrm(ks[13], ssm_shape, 0.01),
        "ssm_a_im": math.pi * n_idx + nrm(ks[14], ssm_shape, 0.01),
        "ssm_log_dt": jax.random.uniform(ks[15], (DEPTH, N_DIR, G), f32,
                                         math.log(1e-3), math.log(1e-1)),
        "ssm_b_re": nrm(ks[16], (DEPTH, N_DIR, G, P, SSM_IN), (2 * SSM_IN) ** -0.5),
        "ssm_b_im": nrm(ks[17], (DEPTH, N_DIR, G, P, SSM_IN), (2 * SSM_IN) ** -0.5),
        "ssm_c_re": nrm(ks[18], (DEPTH, N_DIR, G, SSM_IN, P), (2 * P) ** -0.5),
        "ssm_c_im": nrm(ks[19], (DEPTH, N_DIR, G, SSM_IN, P), (2 * P) ** -0.5),
        "ssm_d": nrm(ks[20], (DEPTH, D_B), 1.0),
        "w_glu": nrm(ks[21], (DEPTH, D_B, D_B), D_B ** -0.5),
        "b_glu": nrm(ks[22], (DEPTH, D_B), 0.02),
        "w_proj_a": nrm(ks[23], (DEPTH, D_A, D_MODEL), D_A ** -0.5),
        "w_proj_b": nrm(ks[24], (DEPTH, D_B, D_MODEL), D_B ** -0.5),
        "w_out": nrm(ks[25], (DEPTH, D_MODEL, D_MODEL), D_MODEL ** -0.5),
        "g_norm_mlp": 1.0 + nrm(ks[26], (DEPTH, D_MODEL), 0.02),
        "w_mlp_in": nrm(ks[27], (DEPTH, D_MODEL, D_FF), D_MODEL ** -0.5),
        "w_mlp_out": nrm(ks[28], (DEPTH, D_FF, D_MODEL), D_FF ** -0.5),
        "g_final": 1.0 + nrm(ks[29], (D_MODEL,), 0.02),
    }


def reference(x_prompt, x_sample, state_ssm_re, state_ssm_im, c, c_ctx, w_ada, b_ada,
              g_norm_mix, w_in, g_sgu, w_spatial, b_spatial, ssm_a_re, ssm_a_im, ssm_log_dt,
              ssm_b_re, ssm_b_im, ssm_c_re, ssm_c_im, ssm_d, w_glu, b_glu, w_proj_a, w_proj_b,
              w_out, g_norm_mlp, w_mlp_in, w_mlp_out, g_final):
    bsz_p = x_prompt.shape[0]
    zero_state = jnp.zeros((bsz_p, N_DIR, N_GROUPS_B, SSM_STATE), jnp.float32)
    xp = x_prompt
    xs = x_sample + grid_pos_embed(x_sample.shape[1], x_sample.dtype)[None]
    ctx_re, ctx_im = [], []
    for l in range(DEPTH):
        mod_ctx = jax.nn.silu(c_ctx)[None, :] @ w_ada[l] + b_ada[l]
        mod_lat = jax.nn.silu(c) @ w_ada[l] + b_ada[l]
        xp, st_re, st_im = trunk_layer(
            xp, mod_ctx, zero_state, zero_state, g_norm_mix[l], w_in[l], g_sgu[l],
            w_spatial[l], b_spatial[l], ssm_a_re[l], ssm_a_im[l], ssm_log_dt[l],
            ssm_b_re[l], ssm_b_im[l], ssm_c_re[l], ssm_c_im[l], ssm_d[l], w_glu[l], b_glu[l],
            w_proj_a[l], w_proj_b[l], w_out[l], g_norm_mlp[l], w_mlp_in[l], w_mlp_out[l])
        ctx_re.append(st_re)
        ctx_im.append(st_im)
        xs, _, _ = trunk_layer(
            xs, mod_lat, state_ssm_re[:, l], state_ssm_im[:, l], g_norm_mix[l], w_in[l], g_sgu[l],
            w_spatial[l], b_spatial[l], ssm_a_re[l], ssm_a_im[l], ssm_log_dt[l],
            ssm_b_re[l], ssm_b_im[l], ssm_c_re[l], ssm_c_im[l], ssm_d[l], w_glu[l], b_glu[l],
            w_proj_a[l], w_proj_b[l], w_out[l], g_norm_mlp[l], w_mlp_in[l], w_mlp_out[l])
    new_ssm_re = jnp.stack(ctx_re, axis=1)
    new_ssm_im = jnp.stack(ctx_im, axis=1)
    y_prompt = rmsnorm(xp, g_final)
    y_sample = rmsnorm(xs, g_final)
    return (y_prompt, y_sample, new_ssm_re, new_ssm_im)
```

```python
import functools
import math

import jax
import jax.numpy as jnp
from jax import lax
from jax.experimental import pallas as pl
from jax.experimental.pallas import tpu as pltpu

F32 = jnp.float32
BF16 = jnp.bfloat16

EPS = 1e-6
POS_BASE = 10000.0
GRID_W = 64
N_MOD = 6
T1 = 16
LANES = 128
VMEM_LIMIT = 56 * 1024 * 1024


def _gelu(x):
    return 0.5 * x * (1.0 + jnp.tanh(0.7978845608028654 * (x + 0.044715 * (x * x * x))))


def _sigmoid(x):
    return 1.0 / (1.0 + jnp.exp(-x))


def _rms(x, g):
    return x * lax.rsqrt(jnp.mean(x * x, axis=-1, keepdims=True) + EPS) * g


def _modnorm(x, g, shift, scale):
    return _rms(x, g) * (1.0 + scale) + shift


def _resident(shape):
    nd = len(shape)
    return pl.BlockSpec(shape, lambda *_: (0,) * nd, pipeline_mode=pl.Buffered(1))


def _params(sem):
    return pltpu.CompilerParams(dimension_semantics=sem, vmem_limit_bytes=VMEM_LIMIT)


def _mod_kernel(c_ref, w_ref, b_ref, o_ref):
    cc = c_ref[...]
    s = cc * _sigmoid(cc)
    o_ref[...] = jnp.dot(s, w_ref[...], preferred_element_type=F32) + b_ref[...]


def _mod_call(cc, w_ada, b_ada):
    rows, d = cc.shape
    n = w_ada.shape[1]
    tn = n // (2 * N_MOD)
    assert n % tn == 0 and tn % LANES == 0
    return pl.pallas_call(
        _mod_kernel,
        grid=(n // tn,),
        in_specs=[pl.BlockSpec((rows, d), lambda j: (0, 0)),
                  pl.BlockSpec((d, tn), lambda j: (0, j)),
                  pl.BlockSpec((1, tn), lambda j: (0, j))],
        out_specs=pl.BlockSpec((rows, tn), lambda j: (0, j)),
        out_shape=jax.ShapeDtypeStruct((rows, n), F32),
        compiler_params=_params(("arbitrary",)),
        name="mod",
    )(cc, w_ada, b_ada.reshape(1, n))


def _xb_kernel(*refs, has_pos):
    if has_pos:
        x_ref, pos_ref, mod_ref, g_ref, w_ref, o_ref = refs
        x = x_ref[...] + pos_ref[...]
    else:
        x_ref, mod_ref, g_ref, w_ref, o_ref = refs
        x = x_ref[...]
    h = _modnorm(x, g_ref[...], mod_ref[0:1, :], mod_ref[1:2, :]).astype(BF16)
    o_ref[...] = lax.dot_general(w_ref[...], h, (((1,), (1,)), ((), ())),
                                 preferred_element_type=F32)


def _xb_call(x, pos, mod, mod_row, g_norm, w_xbt, nn):
    bsz, n, d = x.shape
    db = w_xbt.shape[0]
    nch = n // T1
    cols = bsz * nch
    xv = x.reshape(cols, T1 * d)
    has_pos = pos is not None
    in_specs = [pl.BlockSpec((nn, d), lambda i, t: (i, t))]
    args = [xv]
    if has_pos:
        pblk = nch // nn
        in_specs.append(pl.BlockSpec((nn, d), lambda i, t: (i % pblk, t)))
        args.append(pos.reshape(nch, T1 * d))
    in_specs += [pl.BlockSpec((None, N_MOD, d), lambda i, t: (mod_row(i), 0, 0)),
                 _resident((1, d)),
                 _resident((db, d))]
    args += [mod, g_norm, w_xbt]
    return pl.pallas_call(
        functools.partial(_xb_kernel, has_pos=has_pos),
        grid=(cols // nn, T1),
        in_specs=in_specs,
        out_specs=pl.BlockSpec((None, db, nn), lambda i, t: (t, 0, i)),
        out_shape=jax.ShapeDtypeStruct((T1, db, cols), F32),
        compiler_params=_params(("arbitrary", "arbitrary")),
        name="xb",
    )(*args)


def _s5_kernel(*refs, nseq, n_chunks, has_h0, emit_state, p):
    it = iter(refs)
    xt_ref, w1_ref, wout_ref, pw_ref, d_ref = next(it), next(it), next(it), next(it), next(it)
    h0_ref = next(it) if has_h0 else None
    y_ref = next(it)
    e_ref = next(it) if emit_state else None

    t1, ci, cols = xt_ref.shape
    kt = t1 * ci
    u = xt_ref[...].reshape(kt, cols)
    r1 = jnp.dot(w1_ref[...], u.astype(BF16), preferred_element_type=F32)
    yloc = r1[:kt]
    fr, fi = r1[kt:kt + p], r1[kt + p:kt + 2 * p]
    br, bi = r1[kt + 2 * p:kt + 3 * p], r1[kt + 3 * p:kt + 4 * p]

    nloc = lax.broadcasted_iota(jnp.int32, (p, cols), 1) & (n_chunks - 1)
    first = nloc == 0
    last = nloc == n_chunks - 1

    def coef(k):
        return (pw_ref[k, 0:p, :], pw_ref[k, p:2 * p, :],
                pw_ref[k, 2 * p:3 * p, :], pw_ref[k, 3 * p:4 * p, :])

    if has_h0:
        zero = jnp.zeros((p, cols), F32)
        h0fr = h0fi = h0br = h0bi = zero
        col_seq = lax.broadcasted_iota(jnp.int32, (p, cols), 1) // n_chunks
        for b in range(nseq):
            sel = col_seq == b
            h0fr = jnp.where(sel, h0_ref[0:p, b:b + 1], h0fr)
            h0fi = jnp.where(sel, h0_ref[p:2 * p, b:b + 1], h0fi)
            h0br = jnp.where(sel, h0_ref[2 * p:3 * p, b:b + 1], h0br)
            h0bi = jnp.where(sel, h0_ref[3 * p:4 * p, b:b + 1], h0bi)
        afr, afi, abr, abi = coef(0)
        fr = fr + jnp.where(first, afr * h0fr - afi * h0fi, 0.0)
        fi = fi + jnp.where(first, afr * h0fi + afi * h0fr, 0.0)
        br = br + jnp.where(last, abr * h0br - abi * h0bi, 0.0)
        bi = bi + jnp.where(last, abr * h0bi + abi * h0br, 0.0)

    k = 0
    while (1 << k) < n_chunks:
        sh = 1 << k
        afr, afi, abr, abi = coef(k)
        mf = nloc >= sh
        rr = jnp.where(mf, pltpu.roll(fr, sh, 1), 0.0)
        ri = jnp.where(mf, pltpu.roll(fi, sh, 1), 0.0)
        fr, fi = fr + (afr * rr - afi * ri), fi + (afr * ri + afi * rr)
        mb = nloc < n_chunks - sh
        rr = jnp.where(mb, pltpu.roll(br, cols - sh, 1), 0.0)
        ri = jnp.where(mb, pltpu.roll(bi, cols - sh, 1), 0.0)
        br, bi = br + (abr * rr - abi * ri), bi + (abr * ri + abi * rr)
        k += 1

    if emit_state:
        e_ref[...] = jnp.concatenate([fr, fi, br, bi], axis=0)

    if has_h0:
        ifr, ifi, ibr, ibi = h0fr, h0fi, h0br, h0bi
    else:
        ifr = ifi = ibr = ibi = 0.0
    hin = jnp.concatenate([
        jnp.where(first, ifr, pltpu.roll(fr, 1, 1)),
        jnp.where(first, ifi, pltpu.roll(fi, 1, 1)),
        jnp.where(last, ibr, pltpu.roll(br, cols - 1, 1)),
        jnp.where(last, ibi, pltpu.roll(bi, cols - 1, 1))], axis=0)
    y = yloc + jnp.dot(wout_ref[...], hin.astype(BF16), preferred_element_type=F32)
    y = y + d_ref[...] * u
    y_ref[...] = _gelu(y).reshape(t1, ci, cols)


def _s5_call(xt, w1t, woutt, pw, dvec, h0t, nseq, emit_state):
    t1, db, cols = xt.shape
    g, kt = woutt.shape[0], woutt.shape[1]
    p = woutt.shape[2] // 4
    ci = db // g
    n_chunks = cols // nseq
    has_h0 = h0t is not None
    in_specs = [pl.BlockSpec((t1, ci, cols), lambda i: (0, i, 0)),
                pl.BlockSpec((None,) + w1t.shape[1:], lambda i: (i, 0, 0)),
                pl.BlockSpec((None,) + woutt.shape[1:], lambda i: (i, 0, 0)),
                pl.BlockSpec((None,) + pw.shape[1:], lambda i: (i, 0, 0, 0)),
                pl.BlockSpec((None,) + dvec.shape[1:], lambda i: (i, 0, 0))]
    args = [xt, w1t, woutt, pw, dvec]
    if has_h0:
        in_specs.append(pl.BlockSpec((None,) + h0t.shape[1:], lambda i: (i, 0, 0)))
        args.append(h0t)
    out_specs = [pl.BlockSpec((t1, ci, cols), lambda i: (0, i, 0))]
    out_shape = [jax.ShapeDtypeStruct((t1, db, cols), F32)]
    if emit_state:
        out_specs.append(pl.BlockSpec((None, 4 * p, cols), lambda i: (i, 0, 0)))
        out_shape.append(jax.ShapeDtypeStruct((g, 4 * p, cols), F32))
    return pl.pallas_call(
        functools.partial(_s5_kernel, nseq=nseq, n_chunks=n_chunks, has_h0=has_h0,
                          emit_state=emit_state, p=p),
        grid=(g,),
        in_specs=in_specs,
        out_specs=out_specs,
        out_shape=out_shape,
        compiler_params=_params(("arbitrary",)),
        name="s5",
    )(*args)


def _t_kernel(y_ref, o_ref, s_ref):
    t1, db, nc = y_ref.shape
    for cb in range(db // LANES):
        c0 = cb * LANES
        for t in range(t1):
            s_ref[pl.ds(t, nc, stride=t1), :] = y_ref[t, c0:c0 + LANES, :].T
        o_ref[:, c0:c0 + LANES] = s_ref[...]


def _t_call(yt, nc=LANES):
    t1, db, cols = yt.shape
    return pl.pallas_call(
        _t_kernel,
        grid=(cols // nc,),
        in_specs=[pl.BlockSpec((t1, db, nc), lambda j: (0, 0, j))],
        out_specs=pl.BlockSpec((nc * t1, db), lambda j: (j, 0)),
        out_shape=jax.ShapeDtypeStruct((cols * t1, db), F32),
        scratch_shapes=[pltpu.VMEM((nc * t1, LANES), F32)],
        compiler_params=_params(("arbitrary",)),
        name="t",
    )(yt)


def _mixa_kernel(*refs, has_pos, chunk, da):
    if has_pos:
        x_ref, pos_ref, mod_ref, g_ref, w_ref, gs_ref, ws_ref, bs_ref, wpa_ref, o_ref, ya_ref = refs
        x = x_ref[...] + pos_ref[...]
    else:
        x_ref, mod_ref, g_ref, w_ref, gs_ref, ws_ref, bs_ref, wpa_ref, o_ref, ya_ref = refs
        x = x_ref[...]
    tm = x.shape[0]
    h = _modnorm(x, g_ref[...], mod_ref[0:1, :], mod_ref[1:2, :]).astype(BF16)
    z = jnp.dot(h, w_ref[...], preferred_element_type=F32)
    u = _gelu(z[:, :da])
    v = _rms(_gelu(z[:, da:2 * da]), gs_ref[...]).astype(BF16)
    ng = ws_ref.shape[0]
    cg = da // ng
    for ck in range(tm // chunk):
        r0 = ck * chunk
        for gi in range(ng):
            c0 = gi * cg
            s = jnp.dot(ws_ref[gi], v[r0:r0 + chunk, c0:c0 + cg], preferred_element_type=F32)
            ya_ref[r0:r0 + chunk, c0:c0 + cg] = (u[r0:r0 + chunk, c0:c0 + cg] * (s + bs_ref[gi])).astype(BF16)
    pa = jnp.dot(ya_ref[...], wpa_ref[...], preferred_element_type=F32)
    o_ref[...] = _sigmoid(z[:, 2 * da:]) * pa


def _mixa_call(x2, pos, mod, mod_row, pos_blocks, g_norm, w_uvg, g_sgu, w_s, b_s, w_pa, tm):
    ntok, d = x2.shape
    da = w_pa.shape[0]
    chunk = w_s.shape[1]
    has_pos = pos is not None
    in_specs = [pl.BlockSpec((tm, d), lambda i: (i, 0))]
    args = [x2]
    if has_pos:
        in_specs.append(pl.BlockSpec((tm, d), lambda i: (i % pos_blocks, 0)))
        args.append(pos)
    in_specs += [pl.BlockSpec((None, N_MOD, d), lambda i: (mod_row(i), 0, 0)),
                 _resident((1, d)), _resident(w_uvg.shape), _resident((1, da)),
                 _resident(w_s.shape), _resident(b_s.shape), _resident(w_pa.shape)]
    args += [mod, g_norm, w_uvg, g_sgu, w_s, b_s, w_pa]
    return pl.pallas_call(
        functools.partial(_mixa_kernel, has_pos=has_pos, chunk=chunk, da=da),
        grid=(ntok // tm,),
        in_specs=in_specs,
        out_specs=pl.BlockSpec((tm, d), lambda i: (i, 0)),
        out_shape=jax.ShapeDtypeStruct((ntok, d), F32),
        scratch_shapes=[pltpu.VMEM((tm, da), BF16)],
        compiler_params=_params(("arbitrary",)),
        name="mixa",
    )(*args)


def _mixb_kernel(*refs, has_pos):
    if has_pos:
        (x_ref, pos_ref, mod_ref, g_ref, y_ref, pa_ref, wgb_ref, wglu_ref, bglu_ref,
         wpb_ref, wout_ref, o_ref) = refs
        x = x_ref[...] + pos_ref[...]
    else:
        (x_ref, mod_ref, g_ref, y_ref, pa_ref, wgb_ref, wglu_ref, bglu_ref,
         wpb_ref, wout_ref, o_ref) = refs
        x = x_ref[...]
    h = _modnorm(x, g_ref[...], mod_ref[0:1, :], mod_ref[1:2, :]).astype(BF16)
    glb = jnp.dot(h, wgb_ref[...], preferred_element_type=F32)
    y = y_ref[...]
    gate = _sigmoid(jnp.dot(y.astype(BF16), wglu_ref[...], preferred_element_type=F32) + bglu_ref[...])
    yb = (y * gate).astype(BF16)
    pb = _sigmoid(glb) * jnp.dot(yb, wpb_ref[...], preferred_element_type=F32)
    m = (pa_ref[...] + pb).astype(BF16)
    o_ref[...] = x + mod_ref[2:3, :] * jnp.dot(m, wout_ref[...], preferred_element_type=F32)


def _mixb_call(x2, pos, mod, mod_row, pos_blocks, g_norm, y, pa, w_gb, w_glu, b_glu, w_pb, w_out, tm):
    ntok, d = x2.shape
    db = y.shape[1]
    has_pos = pos is not None
    in_specs = [pl.BlockSpec((tm, d), lambda i: (i, 0))]
    args = [x2]
    if has_pos:
        in_specs.append(pl.BlockSpec((tm, d), lambda i: (i % pos_blocks, 0)))
        args.append(pos)
    in_specs += [pl.BlockSpec((None, N_MOD, d), lambda i: (mod_row(i), 0, 0)),
                 _resident((1, d)),
                 pl.BlockSpec((tm, db), lambda i: (i, 0)),
                 pl.BlockSpec((tm, d), lambda i: (i, 0)),
                 _resident(w_gb.shape), _resident(w_glu.shape), _resident((1, db)),
                 _resident(w_pb.shape), _resident(w_out.shape)]
    args += [mod, g_norm, y, pa, w_gb, w_glu, b_glu, w_pb, w_out]
    return pl.pallas_call(
        functools.partial(_mixb_kernel, has_pos=has_pos),
        grid=(ntok // tm,),
        in_specs=in_specs,
        out_specs=pl.BlockSpec((tm, d), lambda i: (i, 0)),
        out_shape=jax.ShapeDtypeStruct((ntok, d), F32),
        compiler_params=_params(("arbitrary",)),
        name="mixb",
    )(*args)


def _mlp_kernel(x_ref, mod_ref, g_ref, w1_ref, w2_ref, gf_ref, o_ref, h_ref, acc_ref):
    j = pl.program_id(1)

    @pl.when(j == 0)
    def _():
        h_ref[...] = _modnorm(x_ref[...], g_ref[...], mod_ref[3:4, :], mod_ref[4:5, :]).astype(BF16)
        acc_ref[...] = jnp.zeros_like(acc_ref)

    hid = jnp.dot(h_ref[...], w1_ref[...], preferred_element_type=F32)
    hid = jnp.square(jnp.maximum(hid, 0.0)).astype(BF16)
    acc_ref[...] += jnp.dot(hid, w2_ref[...], preferred_element_type=F32)

    @pl.when(j == pl.num_programs(1) - 1)
    def _():
        x2 = x_ref[...] + mod_ref[5:6, :] * acc_ref[...]
        o_ref[...] = _rms(x2, gf_ref[...])


def _mlp_call(x1, mod, mod_row, g_norm, w1, w2, g_final, tm, tf):
    ntok, d = x1.shape
    dff = w1.shape[1]
    return pl.pallas_call(
        _mlp_kernel,
        grid=(ntok // tm, dff // tf),
        in_specs=[pl.BlockSpec((tm, d), lambda i, j: (i, 0)),
                  pl.BlockSpec((None, N_MOD, d), lambda i, j: (mod_row(i), 0, 0)),
                  _resident((1, d)),
                  pl.BlockSpec((d, tf), lambda i, j: (0, j)),
                  pl.BlockSpec((tf, d), lambda i, j: (j, 0)),
                  _resident((1, d))],
        out_specs=pl.BlockSpec((tm, d), lambda i, j: (i, 0)),
        out_shape=jax.ShapeDtypeStruct((ntok, d), F32),
        scratch_shapes=[pltpu.VMEM((tm, d), BF16), pltpu.VMEM((tm, d), F32)],
        compiler_params=_params(("arbitrary", "arbitrary")),
        name="mlp",
    )(x1, mod, g_norm, w1, w2, g_final)


def _pos_embed(n_tokens, d):
    rows = n_tokens // GRID_W
    quarter = d // 4
    omega = 1.0 / (POS_BASE ** (jnp.arange(quarter, dtype=F32) / quarter))
    r = jnp.arange(rows, dtype=F32)[:, None] * omega
    col = jnp.arange(GRID_W, dtype=F32)[:, None] * omega
    e_r = jnp.concatenate([jnp.sin(r), jnp.cos(r)], axis=-1)
    e_c = jnp.concatenate([jnp.sin(col), jnp.cos(col)], axis=-1)
    emb = jnp.concatenate([
        jnp.broadcast_to(e_r[:, None, :], (rows, GRID_W, d // 2)),
        jnp.broadcast_to(e_c[None, :, :], (rows, GRID_W, d // 2))], axis=-1)
    return emb.reshape(n_tokens, d)


def _cmul(ar, ai, br, bi):
    return ar * br - ai * bi, ar * bi + ai * br


def _ssm_prepare(a_re, a_im, log_dt, b_re, b_im, c_re, c_im, d_skip, n_pow):
    ndir, g, p = a_re.shape
    ci = b_re.shape[-1]
    hi = lax.Precision.HIGHEST
    dt = jnp.exp(log_dt)[..., None]
    mag = jnp.exp(a_re * dt)
    ab_re, ab_im = mag * jnp.cos(a_im * dt), mag * jnp.sin(a_im * dt)
    den = a_re * a_re + a_im * a_im
    f_re, f_im = _cmul(ab_re - 1.0, ab_im, a_re / den, -a_im / den)
    bb_re, bb_im = _cmul(f_re[..., None], f_im[..., None], b_re, b_im)

    pr, pi = [jnp.ones_like(ab_re)], [jnp.zeros_like(ab_re)]
    for _ in range(T1):
        nr, ni = _cmul(pr[-1], pi[-1], ab_re, ab_im)
        pr.append(nr)
        pi.append(ni)
    pw_re, pw_im = jnp.stack(pr), jnp.stack(pi)

    ba_re, ba_im = _cmul(pw_re[:T1, ..., None], pw_im[:T1, ..., None], bb_re[None], bb_im[None])
    kk = (jnp.einsum('jdgpa,dgcp->djgac', ba_re, c_re, precision=hi)
          - jnp.einsum('jdgpa,dgcp->djgac', ba_im, c_im, precision=hi))
    kfull = jnp.concatenate([kk[1, :0:-1], (kk[0, 0] + kk[1, 0])[None], kk[0, 1:]], axis=0)
    tt = jnp.arange(T1)
    lag = tt[None, :] - tt[:, None] + (T1 - 1)
    toep = kfull[lag]
    toep_t = jnp.transpose(toep, (2, 1, 4, 0, 3)).reshape(g, T1 * ci, T1 * ci)

    def wend(ba, d, rev):
        w = ba[::-1, d] if rev else ba[:, d]
        return jnp.transpose(w, (1, 2, 0, 3)).reshape(g, p, T1 * ci)
    wend_t = jnp.concatenate([wend(ba_re, 0, True), wend(ba_im, 0, True),
                              wend(ba_re, 1, False), wend(ba_im, 1, False)], axis=1)
    w1t = jnp.concatenate([toep_t, wend_t], axis=1).astype(BF16)

    def wout(d, pows_re, pows_im):
        ca_re, ca_im = _cmul(c_re[d][None], c_im[d][None], pows_re[:, :, None, :], pows_im[:, :, None, :])
        to = lambda w: jnp.transpose(w, (1, 0, 2, 3)).reshape(g, T1 * ci, p)
        return to(ca_re), to(-ca_im)
    fo_re, fo_im = wout(0, pw_re[1:, 0], pw_im[1:, 0])
    bo_re, bo_im = wout(1, pw_re[:0:-1, 1], pw_im[:0:-1, 1])
    wout_t = jnp.concatenate([fo_re, fo_im, bo_re, bo_im], axis=2).astype(BF16)

    sr, si = [pw_re[T1]], [pw_im[T1]]
    for _ in range(n_pow - 1):
        nr, ni = _cmul(sr[-1], si[-1], sr[-1], si[-1])
        sr.append(nr)
        si.append(ni)
    sr, si = jnp.stack(sr), jnp.stack(si)
    pw = jnp.stack([sr[:, 0], si[:, 0], sr[:, 1], si[:, 1]], axis=2)
    pw = jnp.transpose(pw, (1, 0, 2, 3)).reshape(g, n_pow, 4 * p, 1)

    dvec = jnp.tile(d_skip.reshape(g, 1, ci), (1, T1, 1)).reshape(g, T1 * ci, 1)
    return w1t, wout_t, pw, dvec


def _trunk(x, pos, mod, mod_row_tok, mod_row_col, h0t, lw, emit_state, nn, tm, tm_mlp, tf):
    bsz, n, d = x.shape
    x2 = x.reshape(bsz * n, d)
    xt = _xb_call(x, pos, mod, mod_row_col, lw["g_norm_mix"], lw["w_xbt"], nn)
    res = _s5_call(xt, lw["w1t"], lw["wout_t"], lw["pw"], lw["dvec"], h0t, bsz, emit_state)
    y = _t_call(res[0])
    pos_blocks = n // tm
    pa = _mixa_call(x2, pos, mod, mod_row_tok, pos_blocks, lw["g_norm_mix"], lw["w_uvg"], lw["g_sgu"],
                    lw["w_s"], lw["b_s"], lw["w_pa"], tm)
    x1 = _mixb_call(x2, pos, mod, mod_row_tok, pos_blocks, lw["g_norm_mix"], y, pa, lw["w_gb"],
                    lw["w_glu"], lw["b_glu"], lw["w_pb"], lw["w_out"], tm)
    return x1, (res[1] if emit_state else None)


def kernel(x_prompt, x_sample, state_ssm_re, state_ssm_im, c, c_ctx, w_ada, b_ada, g_norm_mix, w_in,
           g_sgu, w_spatial, b_spatial, ssm_a_re, ssm_a_im, ssm_log_dt, ssm_b_re, ssm_b_im, ssm_c_re,
           ssm_c_im, ssm_d, w_glu, b_glu, w_proj_a, w_proj_b, w_out, g_norm_mlp, w_mlp_in, w_mlp_out,
           g_final):
    bp, sp, d = x_prompt.shape
    bs, ss, _ = x_sample.shape
    depth = w_in.shape[0]
    assert depth == 1, "positional embedding and final norm are fused assuming a single trunk layer"
    da = w_proj_a.shape[1]
    db = w_proj_b.shape[1]
    g, p = ssm_a_re.shape[2], ssm_a_re.shape[3]
    nn, tm, tm_mlp, tf = 256, 256, 512, 1024
    n_pow = max(1, int(math.log2(max(sp, ss) // T1)))

    rows = -(-(bs + 1) // 8) * 8
    cc = jnp.zeros((rows, d), F32).at[:bs].set(c).at[bs].set(c_ctx)
    pos = _pos_embed(ss, d)
    xp, xs = x_prompt, x_sample
    ctx_re, ctx_im = [], []
    for l in range(depth):
        mod = _mod_call(cc, w_ada[l], b_ada[l]).reshape(rows, N_MOD, d)
        w1t, wout_t, pw, dvec = _ssm_prepare(ssm_a_re[l], ssm_a_im[l], ssm_log_dt[l], ssm_b_re[l],
                                             ssm_b_im[l], ssm_c_re[l], ssm_c_im[l], ssm_d[l], n_pow)
        wi = w_in[l]
        lw = dict(
            g_norm_mix=g_norm_mix[l].reshape(1, d),
            w_xbt=wi[:, 2 * da:2 * da + db].T.astype(BF16),
            w_uvg=jnp.concatenate([wi[:, :2 * da], wi[:, 2 * da + db:2 * da + db + d]], axis=1).astype(BF16),
            w_gb=wi[:, 2 * da + db + d:].astype(BF16),
            g_sgu=g_sgu[l].reshape(1, da),
            w_s=w_spatial[l].astype(BF16),
            b_s=b_spatial[l][:, :, None],
            w_pa=w_proj_a[l].astype(BF16),
            w_glu=w_glu[l].astype(BF16),
            b_glu=b_glu[l].reshape(1, db),
            w_pb=w_proj_b[l].astype(BF16),
            w_out=w_out[l].astype(BF16),
            w1t=w1t, wout_t=wout_t, pw=pw, dvec=dvec)
        h0 = jnp.concatenate([state_ssm_re[:, l, 0], state_ssm_im[:, l, 0],
                              state_ssm_re[:, l, 1], state_ssm_im[:, l, 1]], axis=-1)
        h0t = jnp.transpose(h0, (1, 2, 0))

        tok_blocks_s = ss // tm
        col_blocks_s = max(1, (ss // T1) // nn)
        xp1, e = _trunk(xp, None, mod, lambda i: bs, lambda i: bs, None, lw, True, nn, tm, tm_mlp, tf)
        xs1, _ = _trunk(xs, pos, mod, lambda i: i // tok_blocks_s, lambda i: i // col_blocks_s, h0t, lw,
                        False, nn, tm, tm_mlp, tf)

        last = l == depth - 1
        gf = g_final.reshape(1, d) if last else None
        w1, w2 = w_mlp_in[l].astype(BF16), w_mlp_out[l].astype(BF16)
        gm = g_norm_mlp[l].reshape(1, d)
        mlp_blocks_s = ss // tm_mlp
        xp = _mlp_call(xp1, mod, lambda i: bs, gm, w1, w2, gf, tm_mlp, tf).reshape(bp, sp, d)
        xs = _mlp_call(xs1, mod, lambda i: i // mlp_blocks_s, gm, w1, w2, gf, tm_mlp, tf).reshape(bs, ss, d)

        nch = sp // T1
        ev = e.reshape(g, 4, p, bp, nch)
        fin = jnp.stack([ev[:, 0:2, :, :, nch - 1], ev[:, 2:4, :, :, 0]], axis=0)
        fin = jnp.transpose(fin, (2, 4, 0, 1, 3))
        ctx_re.append(fin[0])
        ctx_im.append(fin[1])
    return xp, xs, jnp.stack(ctx_re, axis=1), jnp.stack(ctx_im, axis=1)
```

```python
import functools
import math

import jax
import jax.numpy as jnp
from jax import lax
from jax.experimental import pallas as pl
from jax.experimental.pallas import tpu as pltpu

F32 = jnp.float32
BF16 = jnp.bfloat16

EPS = 1e-6
POS_BASE = 10000.0
GRID_W = 64
N_MOD = 6
T1 = 16
LANES = 128
VMEM_LIMIT = 56 * 1024 * 1024


def _gelu(x):
    return 0.5 * x * (1.0 + jnp.tanh(0.7978845608028654 * (x + 0.044715 * (x * x * x))))


def _sigmoid(x):
    return 1.0 / (1.0 + jnp.exp(-x))


def _rms(x, g):
    return x * lax.rsqrt(jnp.mean(x * x, axis=-1, keepdims=True) + EPS) * g


def _modnorm(x, g, shift, scale):
    return _rms(x, g) * (1.0 + scale) + shift


def _resident(shape):
    nd = len(shape)
    return pl.BlockSpec(shape, lambda *_: (0,) * nd, pipeline_mode=pl.Buffered(1))


def _params(sem):
    return pltpu.CompilerParams(dimension_semantics=sem, vmem_limit_bytes=VMEM_LIMIT)


def _add_pos(x, er_ref, ec_ref, tile_in_seq):
    tm, d = x.shape
    half = d // 2
    r0 = tile_in_seq * (tm // GRID_W)
    ec = ec_ref[...]
    parts = []
    for k in range(tm // GRID_W):
        er = jnp.broadcast_to(er_ref[pl.ds(r0 + k, 1), :], (GRID_W, half))
        parts.append(jnp.concatenate([er, ec], axis=1))
    return x + jnp.concatenate(parts, axis=0)


def _mod_kernel(c_ref, w_ref, b_ref, o_ref):
    cc = c_ref[...]
    s = cc * _sigmoid(cc)
    o_ref[...] = jnp.dot(s, w_ref[...], preferred_element_type=F32) + b_ref[...]


def _mod_call(cc, w_ada, b_ada):
    rows, d = cc.shape
    n = w_ada.shape[1]
    tn = n // (2 * N_MOD)
    assert n % tn == 0 and tn % LANES == 0
    return pl.pallas_call(
        _mod_kernel,
        grid=(n // tn,),
        in_specs=[pl.BlockSpec((rows, d), lambda j: (0, 0)),
                  pl.BlockSpec((d, tn), lambda j: (0, j)),
                  pl.BlockSpec((1, tn), lambda j: (0, j))],
        out_specs=pl.BlockSpec((rows, tn), lambda j: (0, j)),
        out_shape=jax.ShapeDtypeStruct((rows, n), F32),
        compiler_params=_params(("arbitrary",)),
        name="mod",
    )(cc, w_ada, b_ada.reshape(1, n))


def _mixa_kernel(*refs, pos_blocks, chunk, da):
    if pos_blocks:
        (x_ref, er_ref, ec_ref, mod_ref, g_ref, w_ref, gs_ref, ws_ref, bs_ref, wpa_ref,
         o_ref, xb_ref, ya_ref, xs_ref) = refs
        x = _add_pos(x_ref[...], er_ref, ec_ref, pl.program_id(0) % pos_blocks)
    else:
        (x_ref, mod_ref, g_ref, w_ref, gs_ref, ws_ref, bs_ref, wpa_ref,
         o_ref, xb_ref, ya_ref, xs_ref) = refs
        x = x_ref[...]
    tm, d = x.shape
    h = _modnorm(x, g_ref[...], mod_ref[0:1, :], mod_ref[1:2, :]).astype(BF16)
    z = jnp.dot(h, w_ref[...], preferred_element_type=F32)
    for cb in range(xs_ref.shape[0]):
        c0 = 2 * da + d + cb * LANES
        xs_ref[cb] = z[:, c0:c0 + LANES]
        for t in range(T1):
            xb_ref[t, :, cb * LANES:(cb + 1) * LANES] = xs_ref[cb, pl.ds(t, tm // T1, stride=T1), :]
    u = _gelu(z[:, :da])
    v = _rms(_gelu(z[:, da:2 * da]), gs_ref[...]).astype(BF16)
    ng = ws_ref.shape[0]
    cg = da // ng
    for ck in range(tm // chunk):
        r0 = ck * chunk
        for gi in range(ng):
            c0 = gi * cg
            s = jnp.dot(ws_ref[gi], v[r0:r0 + chunk, c0:c0 + cg], preferred_element_type=F32)
            ya_ref[r0:r0 + chunk, c0:c0 + cg] = (u[r0:r0 + chunk, c0:c0 + cg] * (s + bs_ref[gi])).astype(BF16)
    pa = jnp.dot(ya_ref[...], wpa_ref[...], preferred_element_type=F32)
    o_ref[...] = _sigmoid(z[:, 2 * da:2 * da + d]) * pa


def _mixa_call(x2, pos_tabs, mod, mod_row, pos_blocks, g_norm, w_uvgx, g_sgu, w_s, b_s, w_pa, tm):
    ntok, d = x2.shape
    da = w_pa.shape[0]
    db = w_uvgx.shape[1] - 2 * da - d
    chunk = w_s.shape[1]
    in_specs = [pl.BlockSpec((tm, d), lambda i: (i, 0))]
    args = [x2]
    if pos_tabs is not None:
        in_specs += [_resident(pos_tabs[0].shape), _resident(pos_tabs[1].shape)]
        args += list(pos_tabs)
    in_specs += [pl.BlockSpec((None, N_MOD, d), lambda i: (mod_row(i), 0, 0)),
                 _resident((1, d)), _resident(w_uvgx.shape), _resident((1, da)),
                 _resident(w_s.shape), _resident(b_s.shape), _resident(w_pa.shape)]
    args += [mod, g_norm, w_uvgx, g_sgu, w_s, b_s, w_pa]
    return pl.pallas_call(
        functools.partial(_mixa_kernel, pos_blocks=pos_blocks if pos_tabs is not None else 0,
                          chunk=chunk, da=da),
        grid=(ntok // tm,),
        in_specs=in_specs,
        out_specs=[pl.BlockSpec((tm, d), lambda i: (i, 0)),
                   pl.BlockSpec((T1, tm // T1, db), lambda i: (0, i, 0))],
        out_shape=[jax.ShapeDtypeStruct((ntok, d), F32),
                   jax.ShapeDtypeStruct((T1, ntok // T1, db), F32)],
        scratch_shapes=[pltpu.VMEM((tm, da), BF16), pltpu.VMEM((db // LANES, tm, LANES), F32)],
        compiler_params=_params(("arbitrary",)),
        name="mixa",
    )(*args)


def _sel_dot(x, sel):
    hi = x.astype(BF16)
    r = x - hi.astype(F32)
    mid = r.astype(BF16)
    lo = (r - mid.astype(F32)).astype(BF16)
    return (jnp.dot(hi, sel, preferred_element_type=F32) + jnp.dot(mid, sel, preferred_element_type=F32)
            + jnp.dot(lo, sel, preferred_element_type=F32))


def _s5_kernel(*refs, nsb, has_h0, emit_state, p, ci):
    it = iter(refs)
    x_ref, w1_ref, wout_ref, pw_ref, d_ref = next(it), next(it), next(it), next(it), next(it)
    h0_ref = next(it) if has_h0 else None
    y_ref = next(it)
    e_ref = next(it) if emit_state else None
    ut_ref, yt_ref = next(it), next(it)

    ngl, kt, cols = ut_ref.shape
    n_chunks = cols // nsb
    n_pow = int(math.log2(n_chunks))

    for t in range(T1):
        st = x_ref[t].T
        for gl in range(ngl):
            ut_ref[gl, t * ci:(t + 1) * ci, :] = st[gl * ci:(gl + 1) * ci, :]

    lane = lax.broadcasted_iota(jnp.int32, (p, cols), 1)
    nloc = lane & (n_chunks - 1)
    first = nloc == 0
    last = nloc == n_chunks - 1
    if emit_state:
        col = lax.broadcasted_iota(jnp.int32, (cols, nsb), 0)
        seq = lax.broadcasted_iota(jnp.int32, (cols, nsb), 1)
        sel_f = (col == seq * n_chunks + (n_chunks - 1)).astype(BF16)
        sel_b = (col == seq * n_chunks).astype(BF16)

    def group(gl, carry):
        r1 = jnp.dot(w1_ref[gl], ut_ref[gl].astype(BF16), preferred_element_type=F32)
        yloc = r1[:kt]
        fr, fi = r1[kt:kt + p], r1[kt + p:kt + 2 * p]
        br, bi = r1[kt + 2 * p:kt + 3 * p], r1[kt + 3 * p:kt + 4 * p]

        def coef(k):
            return (pw_ref[gl, 0:p, k:k + 1], pw_ref[gl, p:2 * p, k:k + 1],
                    pw_ref[gl, 2 * p:3 * p, k:k + 1], pw_ref[gl, 3 * p:4 * p, k:k + 1])

        if has_h0:
            h0fr = h0fi = h0br = h0bi = jnp.zeros((p, cols), F32)
            for b in range(nsb):
                sel = (lane // n_chunks) == b
                h0fr = jnp.where(sel, h0_ref[gl, 0:p, b:b + 1], h0fr)
                h0fi = jnp.where(sel, h0_ref[gl, p:2 * p, b:b + 1], h0fi)
                h0br = jnp.where(sel, h0_ref[gl, 2 * p:3 * p, b:b + 1], h0br)
                h0bi = jnp.where(sel, h0_ref[gl, 3 * p:4 * p, b:b + 1], h0bi)
            afr, afi, abr, abi = coef(0)
            fr = fr + jnp.where(first, afr * h0fr - afi * h0fi, 0.0)
            fi = fi + jnp.where(first, afr * h0fi + afi * h0fr, 0.0)
            br = br + jnp.where(last, abr * h0br - abi * h0bi, 0.0)
            bi = bi + jnp.where(last, abr * h0bi + abi * h0br, 0.0)
        else:
            h0fr = h0fi = h0br = h0bi = 0.0

        for k in range(n_pow):
            sh = 1 << k
            afr, afi, abr, abi = coef(k)
            mf = nloc >= sh
            rr = jnp.where(mf, pltpu.roll(fr, sh, 1), 0.0)
            ri = jnp.where(mf, pltpu.roll(fi, sh, 1), 0.0)
            fr, fi = fr + (afr * rr - afi * ri), fi + (afr * ri + afi * rr)
            mb = nloc < n_chunks - sh
            rr = jnp.where(mb, pltpu.roll(br, cols - sh, 1), 0.0)
            ri = jnp.where(mb, pltpu.roll(bi, cols - sh, 1), 0.0)
            br, bi = br + (abr * rr - abi * ri), bi + (abr * ri + abi * rr)

        if emit_state:
            e_ref[gl] = jnp.concatenate([_sel_dot(jnp.concatenate([fr, fi], axis=0), sel_f),
                                         _sel_dot(jnp.concatenate([br, bi], axis=0), sel_b)], axis=0)

        hin = jnp.concatenate([
            jnp.where(first, h0fr, pltpu.roll(fr, 1, 1)),
            jnp.where(first, h0fi, pltpu.roll(fi, 1, 1)),
            jnp.where(last, h0br, pltpu.roll(br, cols - 1, 1)),
            jnp.where(last, h0bi, pltpu.roll(bi, cols - 1, 1))], axis=0)
        yt_ref[gl] = yloc + jnp.dot(wout_ref[gl], hin.astype(BF16), preferred_element_type=F32)
        return carry

    lax.fori_loop(0, ngl, group, 0)

    for t in range(T1):
        yt = jnp.concatenate([yt_ref[gl, t * ci:(t + 1) * ci, :] for gl in range(ngl)], axis=0)
        y_ref[t] = _gelu(yt.T + d_ref[...] * x_ref[t])


def _s5_call(xb, w1t, woutt, pw, d_skip, h0t, n, nsb, emit_state):
    _, ncol, db = xb.shape
    ntok = ncol * T1
    g, kt = woutt.shape[0], woutt.shape[1]
    p = woutt.shape[2] // 4
    ci = db // g
    ngl = LANES // ci
    blk = nsb * n
    cols = blk // T1
    has_h0 = h0t is not None
    in_specs = [pl.BlockSpec((T1, cols, LANES), lambda i, j: (0, j, i)),
                pl.BlockSpec((ngl,) + w1t.shape[1:], lambda i, j: (i, 0, 0)),
                pl.BlockSpec((ngl,) + woutt.shape[1:], lambda i, j: (i, 0, 0)),
                pl.BlockSpec((ngl,) + pw.shape[1:], lambda i, j: (i, 0, 0)),
                pl.BlockSpec((1, LANES), lambda i, j: (0, i))]
    args = [xb, w1t, woutt, pw, d_skip]
    if has_h0:
        in_specs.append(pl.BlockSpec((None, ngl) + h0t.shape[2:], lambda i, j: (j, i, 0, 0)))
        args.append(h0t)
    out_specs = [pl.BlockSpec((T1, cols, LANES), lambda i, j: (0, j, i))]
    out_shape = [jax.ShapeDtypeStruct((T1, ncol, db), F32)]
    if emit_state:
        out_specs.append(pl.BlockSpec((None, ngl, 4 * p, nsb), lambda i, j: (j, i, 0, 0)))
        out_shape.append(jax.ShapeDtypeStruct((ntok // blk, g, 4 * p, nsb), F32))
    return pl.pallas_call(
        functools.partial(_s5_kernel, nsb=nsb, has_h0=has_h0, emit_state=emit_state, p=p, ci=ci),
        grid=(g // ngl, ntok // blk),
        in_specs=in_specs,
        out_specs=out_specs,
        out_shape=out_shape,
        scratch_shapes=[pltpu.VMEM((ngl, kt, cols), F32), pltpu.VMEM((ngl, kt, cols), F32)],
        compiler_params=_params(("arbitrary", "arbitrary")),
        name="s5",
    )(*args)


def _mixb_kernel(*refs, pos_blocks):
    if pos_blocks:
        (x_ref, er_ref, ec_ref, mod_ref, g_ref, y_ref, pa_ref, wgb_ref, wglu_ref, bglu_ref,
         wpb_ref, wout_ref, o_ref, ys_ref) = refs
        x = _add_pos(x_ref[...], er_ref, ec_ref, pl.program_id(0) % pos_blocks)
    else:
        (x_ref, mod_ref, g_ref, y_ref, pa_ref, wgb_ref, wglu_ref, bglu_ref,
         wpb_ref, wout_ref, o_ref, ys_ref) = refs
        x = x_ref[...]
    tm = x.shape[0]
    h = _modnorm(x, g_ref[...], mod_ref[0:1, :], mod_ref[1:2, :]).astype(BF16)
    glb = jnp.dot(h, wgb_ref[...], preferred_element_type=F32)
    for cb in range(ys_ref.shape[0]):
        for t in range(T1):
            ys_ref[cb, pl.ds(t, tm // T1, stride=T1), :] = y_ref[t, :, cb * LANES:(cb + 1) * LANES]
    y = jnp.concatenate([ys_ref[cb] for cb in range(ys_ref.shape[0])], axis=1)
    gate = _sigmoid(jnp.dot(y.astype(BF16), wglu_ref[...], preferred_element_type=F32) + bglu_ref[...])
    yb = (y * gate).astype(BF16)
    pb = _sigmoid(glb) * jnp.dot(yb, wpb_ref[...], preferred_element_type=F32)
    m = (pa_ref[...] + pb).astype(BF16)
    o_ref[...] = x + mod_ref[2:3, :] * jnp.dot(m, wout_ref[...], preferred_element_type=F32)


def _mixb_call(x2, pos_tabs, mod, mod_row, pos_blocks, g_norm, y, pa, w_gb, w_glu, b_glu, w_pb, w_out, tm):
    ntok, d = x2.shape
    db = y.shape[2]
    in_specs = [pl.BlockSpec((tm, d), lambda i: (i, 0))]
    args = [x2]
    if pos_tabs is not None:
        in_specs += [_resident(pos_tabs[0].shape), _resident(pos_tabs[1].shape)]
        args += list(pos_tabs)
    in_specs += [pl.BlockSpec((None, N_MOD, d), lambda i: (mod_row(i), 0, 0)),
                 _resident((1, d)),
                 pl.BlockSpec((T1, tm // T1, db), lambda i: (0, i, 0)),
                 pl.BlockSpec((tm, d), lambda i: (i, 0)),
                 _resident(w_gb.shape), _resident(w_glu.shape), _resident((1, db)),
                 _resident(w_pb.shape), _resident(w_out.shape)]
    args += [mod, g_norm, y, pa, w_gb, w_glu, b_glu, w_pb, w_out]
    return pl.pallas_call(
        functools.partial(_mixb_kernel, pos_blocks=pos_blocks if pos_tabs is not None else 0),
        grid=(ntok // tm,),
        in_specs=in_specs,
        out_specs=pl.BlockSpec((tm, d), lambda i: (i, 0)),
        out_shape=jax.ShapeDtypeStruct((ntok, d), F32),
        scratch_shapes=[pltpu.VMEM((db // LANES, tm, LANES), F32)],
        compiler_params=_params(("arbitrary",)),
        name="mixb",
    )(*args)


def _mlp_kernel(x_ref, mod_ref, g_ref, w1_ref, w2_ref, gf_ref, o_ref, h_ref, acc_ref):
    j = pl.program_id(1)

    @pl.when(j == 0)
    def _():
        h_ref[...] = _modnorm(x_ref[...], g_ref[...], mod_ref[3:4, :], mod_ref[4:5, :]).astype(BF16)
        acc_ref[...] = jnp.zeros_like(acc_ref)

    hid = jnp.dot(h_ref[...], w1_ref[...], preferred_element_type=F32)
    hid = jnp.square(jnp.maximum(hid, 0.0)).astype(BF16)
    acc_ref[...] += jnp.dot(hid, w2_ref[...], preferred_element_type=F32)

    @pl.when(j == pl.num_programs(1) - 1)
    def _():
        x2 = x_ref[...] + mod_ref[5:6, :] * acc_ref[...]
        o_ref[...] = _rms(x2, gf_ref[...])


def _mlp_call(x1, mod, mod_row, g_norm, w1, w2, g_final, tm, tf):
    ntok, d = x1.shape
    dff = w1.shape[1]
    return pl.pallas_call(
        _mlp_kernel,
        grid=(ntok // tm, dff // tf),
        in_specs=[pl.BlockSpec((tm, d), lambda i, j: (i, 0)),
                  pl.BlockSpec((None, N_MOD, d), lambda i, j: (mod_row(i), 0, 0)),
                  _resident((1, d)),
                  pl.BlockSpec((d, tf), lambda i, j: (0, j)),
                  pl.BlockSpec((tf, d), lambda i, j: (j, 0)),
                  _resident((1, d))],
        out_specs=pl.BlockSpec((tm, d), lambda i, j: (i, 0)),
        out_shape=jax.ShapeDtypeStruct((ntok, d), F32),
        scratch_shapes=[pltpu.VMEM((tm, d), BF16), pltpu.VMEM((tm, d), F32)],
        compiler_params=_params(("arbitrary", "arbitrary")),
        name="mlp",
    )(x1, mod, g_norm, w1, w2, g_final)


def _pos_tables(n_tokens, d):
    rows = n_tokens // GRID_W
    quarter = d // 4
    omega = 1.0 / (POS_BASE ** (jnp.arange(quarter, dtype=F32) / quarter))
    r = jnp.arange(rows, dtype=F32)[:, None] * omega
    col = jnp.arange(GRID_W, dtype=F32)[:, None] * omega
    e_r = jnp.concatenate([jnp.sin(r), jnp.cos(r)], axis=-1)
    e_c = jnp.concatenate([jnp.sin(col), jnp.cos(col)], axis=-1)
    return e_r, e_c


def _cmul(ar, ai, br, bi):
    return ar * br - ai * bi, ar * bi + ai * br


def _ssm_prepare(a_re, a_im, log_dt, b_re, b_im, c_re, c_im, n_pow):
    ndir, g, p = a_re.shape
    ci = b_re.shape[-1]
    hi = lax.Precision.HIGHEST
    dt = jnp.exp(log_dt)[..., None]
    mag = jnp.exp(a_re * dt)
    ab_re, ab_im = mag * jnp.cos(a_im * dt), mag * jnp.sin(a_im * dt)
    den = a_re * a_re + a_im * a_im
    f_re, f_im = _cmul(ab_re - 1.0, ab_im, a_re / den, -a_im / den)
    bb_re, bb_im = _cmul(f_re[..., None], f_im[..., None], b_re, b_im)

    pr, pi = [jnp.ones_like(ab_re)], [jnp.zeros_like(ab_re)]
    for _ in range(T1):
        nr, ni = _cmul(pr[-1], pi[-1], ab_re, ab_im)
        pr.append(nr)
        pi.append(ni)
    pw_re, pw_im = jnp.stack(pr), jnp.stack(pi)

    ba_re, ba_im = _cmul(pw_re[:T1, ..., None], pw_im[:T1, ..., None], bb_re[None], bb_im[None])
    kk = (jnp.einsum('jdgpa,dgcp->djgac', ba_re, c_re, precision=hi)
          - jnp.einsum('jdgpa,dgcp->djgac', ba_im, c_im, precision=hi))
    kfull = jnp.concatenate([kk[1, :0:-1], (kk[0, 0] + kk[1, 0])[None], kk[0, 1:]], axis=0)
    tt = jnp.arange(T1)
    lag = tt[None, :] - tt[:, None] + (T1 - 1)
    toep = kfull[lag]
    toep_t = jnp.transpose(toep, (2, 1, 4, 0, 3)).reshape(g, T1 * ci, T1 * ci)

    def wend(ba, d, rev):
        w = ba[::-1, d] if rev else ba[:, d]
        return jnp.transpose(w, (1, 2, 0, 3)).reshape(g, p, T1 * ci)
    wend_t = jnp.concatenate([wend(ba_re, 0, True), wend(ba_im, 0, True),
                              wend(ba_re, 1, False), wend(ba_im, 1, False)], axis=1)
    w1t = jnp.concatenate([toep_t, wend_t], axis=1).astype(BF16)

    def wout(d, pows_re, pows_im):
        ca_re, ca_im = _cmul(c_re[d][None], c_im[d][None], pows_re[:, :, None, :], pows_im[:, :, None, :])
        to = lambda w: jnp.transpose(w, (1, 0, 2, 3)).reshape(g, T1 * ci, p)
        return to(ca_re), to(-ca_im)
    fo_re, fo_im = wout(0, pw_re[1:, 0], pw_im[1:, 0])
    bo_re, bo_im = wout(1, pw_re[:0:-1, 1], pw_im[:0:-1, 1])
    wout_t = jnp.concatenate([fo_re, fo_im, bo_re, bo_im], axis=2).astype(BF16)

    sr, si = [pw_re[T1]], [pw_im[T1]]
    for _ in range(n_pow - 1):
        nr, ni = _cmul(sr[-1], si[-1], sr[-1], si[-1])
        sr.append(nr)
        si.append(ni)
    sr, si = jnp.stack(sr), jnp.stack(si)
    pw = jnp.stack([sr[:, 0], si[:, 0], sr[:, 1], si[:, 1]], axis=2)
    pw = jnp.transpose(pw, (1, 2, 3, 0)).reshape(g, 4 * p, n_pow)
    return w1t, wout_t, pw


def _trunk(x, pos_tabs, mod, mod_row, h0t, lw, nsb, emit_state, tm):
    bsz, n, d = x.shape
    x2 = x.reshape(bsz * n, d)
    pos_blocks = n // tm
    pa, xb = _mixa_call(x2, pos_tabs, mod, mod_row, pos_blocks, lw["g_norm_mix"], lw["w_uvgx"], lw["g_sgu"],
                        lw["w_s"], lw["b_s"], lw["w_pa"], tm)
    res = _s5_call(xb, lw["w1t"], lw["wout_t"], lw["pw"], lw["d_skip"], h0t, n, nsb, emit_state)
    x1 = _mixb_call(x2, pos_tabs, mod, mod_row, pos_blocks, lw["g_norm_mix"], res[0], pa, lw["w_gb"],
                    lw["w_glu"], lw["b_glu"], lw["w_pb"], lw["w_out"], tm)
    return x1, (res[1] if emit_state else None)


def kernel(x_prompt, x_sample, state_ssm_re, state_ssm_im, c, c_ctx, w_ada, b_ada, g_norm_mix, w_in,
           g_sgu, w_spatial, b_spatial, ssm_a_re, ssm_a_im, ssm_log_dt, ssm_b_re, ssm_b_im, ssm_c_re,
           ssm_c_im, ssm_d, w_glu, b_glu, w_proj_a, w_proj_b, w_out, g_norm_mlp, w_mlp_in, w_mlp_out,
           g_final):
    bp, sp, d = x_prompt.shape
    bs, ss, _ = x_sample.shape
    depth = w_in.shape[0]
    assert depth == 1, "positional embedding and final norm are fused assuming a single trunk layer"
    da = w_proj_a.shape[1]
    db = w_proj_b.shape[1]
    g, p = ssm_a_re.shape[2], ssm_a_re.shape[3]
    tm, tm_mlp, tf = 256, 512, 1024
    nsb_p = max(1, LANES * T1 // sp)
    nsb_s = max(1, LANES * T1 // ss)
    n_pow = max(1, int(math.log2(max(sp, ss) // T1)))

    rows = -(-(bs + 1) // 8) * 8
    cc = jnp.zeros((rows, d), F32).at[:bs].set(c).at[bs].set(c_ctx)
    pos_tabs = _pos_tables(ss, d)
    l = 0
    mod = _mod_call(cc, w_ada[l], b_ada[l]).reshape(rows, N_MOD, d)
    w1t, wout_t, pw = _ssm_prepare(ssm_a_re[l], ssm_a_im[l], ssm_log_dt[l], ssm_b_re[l],
                                   ssm_b_im[l], ssm_c_re[l], ssm_c_im[l], n_pow)
    wi = w_in[l]
    lw = dict(
        g_norm_mix=g_norm_mix[l].reshape(1, d),
        w_uvgx=jnp.concatenate([wi[:, :2 * da], wi[:, 2 * da + db:2 * da + db + d],
                                wi[:, 2 * da:2 * da + db]], axis=1).astype(BF16),
        w_gb=wi[:, 2 * da + db + d:].astype(BF16),
        g_sgu=g_sgu[l].reshape(1, da),
        w_s=w_spatial[l].astype(BF16),
        b_s=b_spatial[l][:, :, None],
        w_pa=w_proj_a[l].astype(BF16),
        w_glu=w_glu[l].astype(BF16),
        b_glu=b_glu[l].reshape(1, db),
        w_pb=w_proj_b[l].astype(BF16),
        w_out=w_out[l].astype(BF16),
        w1t=w1t, wout_t=wout_t, pw=pw, d_skip=ssm_d[l].reshape(1, db))
    h0 = jnp.concatenate([state_ssm_re[:, l, 0], state_ssm_im[:, l, 0],
                          state_ssm_re[:, l, 1], state_ssm_im[:, l, 1]], axis=-1)
    h0t = jnp.transpose(h0.reshape(bs // nsb_s, nsb_s, g, 4 * p), (0, 2, 3, 1))

    xp1, e = _trunk(x_prompt, None, mod, lambda i: bs, None, lw, nsb_p, True, tm)
    xs1, _ = _trunk(x_sample, pos_tabs, mod, lambda i: i // (ss // tm), h0t, lw, nsb_s, False, tm)

    gf = g_final.reshape(1, d)
    w1, w2 = w_mlp_in[l].astype(BF16), w_mlp_out[l].astype(BF16)
    gm = g_norm_mlp[l].reshape(1, d)
    xp = _mlp_call(xp1, mod, lambda i: bs, gm, w1, w2, gf, tm_mlp, tf).reshape(bp, sp, d)
    xs = _mlp_call(xs1, mod, lambda i: i // (ss // tm_mlp), gm, w1, w2, gf, tm_mlp, tf).reshape(bs, ss, d)

    ev = e.reshape(bp // nsb_p, g, 2, 2, p, nsb_p)
    fin = jnp.transpose(ev, (3, 0, 5, 2, 1, 4)).reshape(2, bp, 2, g, p)
    return xp, xs, fin[0][:, None], fin[1][:, None]
```

```python
import functools

import jax
import jax.numpy as jnp
from jax import lax
from jax.experimental import pallas as pl
from jax.experimental.pallas import tpu as pltpu

F32 = jnp.float32
BF16 = jnp.bfloat16

EPS = 1e-6
POS_BASE = 10000.0
GRID_W = 64
N_MOD = 6
T1 = 16
LANES = 128
SUBLANES = 8
VMEM_LIMIT = 56 * 1024 * 1024


def _gelu(x):
    return 0.5 * x * (1.0 + jnp.tanh(0.7978845608028654 * (x + 0.044715 * (x * x * x))))


def _sigmoid(x):
    return 1.0 / (1.0 + jnp.exp(-x))


def _rms(x, g):
    return x * lax.rsqrt(jnp.mean(x * x, axis=-1, keepdims=True) + EPS) * g


def _modnorm(x, g, shift, scale):
    return _rms(x, g) * (1.0 + scale) + shift


def _resident(shape):
    nd = len(shape)
    return pl.BlockSpec(shape, lambda *_: (0,) * nd, pipeline_mode=pl.Buffered(1))


def _params(sem):
    return pltpu.CompilerParams(dimension_semantics=sem, vmem_limit_bytes=VMEM_LIMIT)


def _add_pos(x, er_ref, ec_ref, tile_in_seq):
    tm, d = x.shape
    half = d // 2
    r0 = tile_in_seq * (tm // GRID_W)
    ec = ec_ref[...]
    parts = []
    for k in range(tm // GRID_W):
        er = jnp.broadcast_to(er_ref[pl.ds(r0 + k, 1), :], (GRID_W, half))
        parts.append(jnp.concatenate([er, ec], axis=1))
    return x + jnp.concatenate(parts, axis=0)


def _mod_kernel(c_ref, w_ref, b_ref, o_ref):
    cc = c_ref[...]
    s = cc * _sigmoid(cc)
    o_ref[...] = jnp.dot(s, w_ref[...], preferred_element_type=F32) + b_ref[...]


def _mod_call(cc, w_ada, b_ada):
    rows, d = cc.shape
    n = w_ada.shape[1]
    tn = n // (2 * N_MOD)
    assert n % tn == 0 and tn % LANES == 0
    return pl.pallas_call(
        _mod_kernel,
        grid=(n // tn,),
        in_specs=[pl.BlockSpec((rows, d), lambda j: (0, 0)),
                  pl.BlockSpec((d, tn), lambda j: (0, j)),
                  pl.BlockSpec((1, tn), lambda j: (0, j))],
        out_specs=pl.BlockSpec((rows, tn), lambda j: (0, j)),
        out_shape=jax.ShapeDtypeStruct((rows, n), F32),
        compiler_params=_params(("arbitrary",)),
        name="mod",
    )(cc, w_ada, b_ada.reshape(1, n))


def _mixa_kernel(*refs, pos_blocks, chunk, da):
    if pos_blocks:
        (x_ref, er_ref, ec_ref, mod_ref, g_ref, w_ref, gs_ref, ws_ref, bs_ref, wpa_ref,
         o_ref, xb_ref, ya_ref, xs_ref) = refs
        x = _add_pos(x_ref[...], er_ref, ec_ref, pl.program_id(0) % pos_blocks)
    else:
        (x_ref, mod_ref, g_ref, w_ref, gs_ref, ws_ref, bs_ref, wpa_ref,
         o_ref, xb_ref, ya_ref, xs_ref) = refs
        x = x_ref[...]
    tm, d = x.shape
    h = _modnorm(x, g_ref[...], mod_ref[0:1, :], mod_ref[1:2, :]).astype(BF16)
    z = jnp.dot(h, w_ref[...], preferred_element_type=F32)
    for cb in range(xs_ref.shape[0]):
        c0 = 2 * da + d + cb * LANES
        xs_ref[cb] = z[:, c0:c0 + LANES]
        for t in range(T1):
            xb_ref[t, :, cb * LANES:(cb + 1) * LANES] = (
                xs_ref[cb, pl.ds(t, tm // T1, stride=T1), :].astype(xb_ref.dtype))
    u = _gelu(z[:, :da])
    v = _rms(_gelu(z[:, da:2 * da]), gs_ref[...]).astype(BF16)
    ng = ws_ref.shape[0]
    cg = da // ng
    for ck in range(tm // chunk):
        r0 = ck * chunk
        for gi in range(ng):
            c0 = gi * cg
            s = jnp.dot(ws_ref[gi], v[r0:r0 + chunk, c0:c0 + cg], preferred_element_type=F32)
            ya_ref[r0:r0 + chunk, c0:c0 + cg] = (u[r0:r0 + chunk, c0:c0 + cg] * (s + bs_ref[gi])).astype(BF16)
    pa = jnp.dot(ya_ref[...], wpa_ref[...], preferred_element_type=F32)
    o_ref[...] = _sigmoid(z[:, 2 * da:2 * da + d]) * pa


def _mixa_call(x2, pos_tabs, mod, mod_row, pos_blocks, g_norm, w_uvgx, g_sgu, w_s, b_s, w_pa, tm):
    ntok, d = x2.shape
    da = w_pa.shape[0]
    db = w_uvgx.shape[1] - 2 * da - d
    chunk = w_s.shape[1]
    in_specs = [pl.BlockSpec((tm, d), lambda i: (i, 0))]
    args = [x2]
    if pos_tabs is not None:
        in_specs += [_resident(pos_tabs[0].shape), _resident(pos_tabs[1].shape)]
        args += list(pos_tabs)
    in_specs += [pl.BlockSpec((None, N_MOD, d), lambda i: (mod_row(i), 0, 0)),
                 _resident((1, d)), _resident(w_uvgx.shape), _resident((1, da)),
                 _resident(w_s.shape), _resident(b_s.shape), _resident(w_pa.shape)]
    args += [mod, g_norm, w_uvgx, g_sgu, w_s, b_s, w_pa]
    return pl.pallas_call(
        functools.partial(_mixa_kernel, pos_blocks=pos_blocks if pos_tabs is not None else 0,
                          chunk=chunk, da=da),
        grid=(ntok // tm,),
        in_specs=in_specs,
        out_specs=[pl.BlockSpec((tm, d), lambda i: (i, 0)),
                   pl.BlockSpec((T1, tm // T1, db), lambda i: (0, i, 0))],
        out_shape=[jax.ShapeDtypeStruct((ntok, d), F32),
                   jax.ShapeDtypeStruct((T1, ntok // T1, db), BF16)],
        scratch_shapes=[pltpu.VMEM((tm, da), BF16), pltpu.VMEM((db // LANES, tm, LANES), F32)],
        compiler_params=_params(("arbitrary",)),
        name="mixa",
    )(*args)


def _s5_kernel(*refs, nsb, n_chunks, has_h0, emit_state, p, ci):
    it = iter(refs)
    x_ref, w1_ref, wout_ref, pw_ref, d_ref = next(it), next(it), next(it), next(it), next(it)
    h0_ref = next(it) if has_h0 else None
    y_ref = next(it)
    e_ref = next(it) if emit_state else None
    ut_ref, s_ref, hin_ref = next(it), next(it), next(it)

    ngl, kt, cols = ut_ref.shape
    npair = ngl // 2
    pitch = s_ref.shape[2] // nsb

    for t in range(T1):
        st = x_ref[t].astype(F32).T
        for gl in range(ngl):
            ut_ref[gl, t * ci:(t + 1) * ci, :] = st[gl * ci:(gl + 1) * ci, :]

    def local(pr, carry):
        r = []
        for hlf in range(2):
            gl = 2 * pr + hlf
            r1 = jnp.dot(w1_ref[gl], ut_ref[gl].astype(BF16), preferred_element_type=F32)
            ut_ref[gl] = r1[:kt]
            r.append(r1)
        for q in range(4):
            lo = kt + q * p
            sq = jnp.concatenate([r[0][lo:lo + p], r[1][lo:lo + p]], axis=0).T
            for b in range(nsb):
                s_ref[pr, q, b * pitch:b * pitch + n_chunks, :] = sq[b * n_chunks:(b + 1) * n_chunks]
        return carry

    lax.fori_loop(0, npair, local, 0)

    def rows(n):
        return pl.ds(n, nsb, stride=pitch) if nsb > 1 else pl.ds(n, 1)

    def step(n, hs):
        rf, rb = rows(n), rows(n_chunks - 1 - n)
        out = []
        for pr in range(npair):
            fr, fi, br, bi = hs[4 * pr:4 * pr + 4]
            afr, afi = pw_ref[pr, 0:1, :], pw_ref[pr, 1:2, :]
            abr, abi = pw_ref[pr, 2:3, :], pw_ref[pr, 3:4, :]
            sfr, sfi = s_ref[pr, 0, rf, :], s_ref[pr, 1, rf, :]
            sbr, sbi = s_ref[pr, 2, rb, :], s_ref[pr, 3, rb, :]
            hin_ref[pr, 0, rf, :] = fr
            hin_ref[pr, 1, rf, :] = fi
            hin_ref[pr, 2, rb, :] = br
            hin_ref[pr, 3, rb, :] = bi
            out += [afr * fr - afi * fi + sfr, afr * fi + afi * fr + sfi,
                    abr * br - abi * bi + sbr, abr * bi + abi * br + sbi]
        return tuple(out)

    if has_h0:
        init = tuple(h0_ref[pr, q] for pr in range(npair) for q in range(4))
    else:
        init = tuple(jnp.zeros((nsb, LANES), F32) for _ in range(4 * npair))
    fin = lax.fori_loop(0, n_chunks, step, init)
    if emit_state:
        for pr in range(npair):
            for q in range(4):
                e_ref[pr, q] = fin[4 * pr + q]

    def inter(pr, carry):
        hin = jnp.concatenate(
            [jnp.concatenate([hin_ref[pr, q, b * pitch:b * pitch + n_chunks, :] for b in range(nsb)], axis=0)
             for q in range(4)], axis=1).astype(BF16)
        yy = lax.dot_general(wout_ref[pr], hin, (((1,), (1,)), ((), ())), preferred_element_type=F32)
        ut_ref[2 * pr] += yy[:kt]
        ut_ref[2 * pr + 1] += yy[kt:]
        return carry

    lax.fori_loop(0, npair, inter, 0)

    for t in range(T1):
        yt = jnp.concatenate([ut_ref[gl, t * ci:(t + 1) * ci, :] for gl in range(ngl)], axis=0)
        y_ref[t] = _gelu(yt.T + d_ref[...] * x_ref[t].astype(F32)).astype(y_ref.dtype)


def _s5_call(xb, w1t, woutp, pw, d_skip, h0p, n, nsb, emit_state):
    _, ncol, db = xb.shape
    g, kt = w1t.shape[0], w1t.shape[2]
    p = (w1t.shape[1] - kt) // 4
    ci = db // g
    ngl = LANES // ci
    npair = ngl // 2
    assert 2 * p == LANES and ngl % 2 == 0
    n_chunks = n // T1
    cols = nsb * n_chunks
    has_h0 = h0p is not None
    in_specs = [pl.BlockSpec((T1, cols, LANES), lambda i, j: (0, j, i)),
                pl.BlockSpec((ngl,) + w1t.shape[1:], lambda i, j: (i, 0, 0)),
                pl.BlockSpec((npair,) + woutp.shape[1:], lambda i, j: (i, 0, 0)),
                pl.BlockSpec((npair,) + pw.shape[1:], lambda i, j: (i, 0, 0)),
                pl.BlockSpec((1, LANES), lambda i, j: (0, i))]
    args = [xb, w1t, woutp, pw, d_skip]
    if has_h0:
        in_specs.append(pl.BlockSpec((None, npair, 4, nsb, LANES), lambda i, j: (j, i, 0, 0, 0)))
        args.append(h0p)
    out_specs = [pl.BlockSpec((T1, cols, LANES), lambda i, j: (0, j, i))]
    out_shape = [jax.ShapeDtypeStruct((T1, ncol, db), xb.dtype)]
    if emit_state:
        out_specs.append(pl.BlockSpec((None, npair, 4, nsb, LANES), lambda i, j: (j, i, 0, 0, 0)))
        out_shape.append(jax.ShapeDtypeStruct((ncol // cols, g // 2, 4, nsb, LANES), F32))
    return pl.pallas_call(
        functools.partial(_s5_kernel, nsb=nsb, n_chunks=n_chunks, has_h0=has_h0, emit_state=emit_state,
                          p=p, ci=ci),
        grid=(g // ngl, ncol // cols),
        in_specs=in_specs,
        out_specs=out_specs,
        out_shape=out_shape,
        scratch_shapes=[pltpu.VMEM((ngl, kt, cols), F32)]
        + [pltpu.VMEM((npair, 4, nsb * (n_chunks + SUBLANES), LANES), F32)] * 2,
        compiler_params=_params(("arbitrary", "arbitrary")),
        name="s5",
    )(*args)


def _mixb_kernel(*refs, pos_blocks):
    if pos_blocks:
        (x_ref, er_ref, ec_ref, mod_ref, g_ref, y_ref, pa_ref, wgb_ref, wglu_ref, bglu_ref,
         wpb_ref, wout_ref, o_ref, ys_ref) = refs
        x = _add_pos(x_ref[...], er_ref, ec_ref, pl.program_id(0) % pos_blocks)
    else:
        (x_ref, mod_ref, g_ref, y_ref, pa_ref, wgb_ref, wglu_ref, bglu_ref,
         wpb_ref, wout_ref, o_ref, ys_ref) = refs
        x = x_ref[...]
    tm = x.shape[0]
    h = _modnorm(x, g_ref[...], mod_ref[0:1, :], mod_ref[1:2, :]).astype(BF16)
    glb = jnp.dot(h, wgb_ref[...], preferred_element_type=F32)
    for cb in range(ys_ref.shape[0]):
        for t in range(T1):
            ys_ref[cb, pl.ds(t, tm // T1, stride=T1), :] = y_ref[t, :, cb * LANES:(cb + 1) * LANES].astype(F32)
    y = jnp.concatenate([ys_ref[cb] for cb in range(ys_ref.shape[0])], axis=1)
    gate = _sigmoid(jnp.dot(y.astype(BF16), wglu_ref[...], preferred_element_type=F32) + bglu_ref[...])
    yb = (y * gate).astype(BF16)
    pb = _sigmoid(glb) * jnp.dot(yb, wpb_ref[...], preferred_element_type=F32)
    m = (pa_ref[...] + pb).astype(BF16)
    o_ref[...] = x + mod_ref[2:3, :] * jnp.dot(m, wout_ref[...], preferred_element_type=F32)


def _mixb_call(x2, pos_tabs, mod, mod_row, pos_blocks, g_norm, y, pa, w_gb, w_glu, b_glu, w_pb, w_out, tm):
    ntok, d = x2.shape
    db = y.shape[2]
    in_specs = [pl.BlockSpec((tm, d), lambda i: (i, 0))]
    args = [x2]
    if pos_tabs is not None:
        in_specs += [_resident(pos_tabs[0].shape), _resident(pos_tabs[1].shape)]
        args += list(pos_tabs)
    in_specs += [pl.BlockSpec((None, N_MOD, d), lambda i: (mod_row(i), 0, 0)),
                 _resident((1, d)),
                 pl.BlockSpec((T1, tm // T1, db), lambda i: (0, i, 0)),
                 pl.BlockSpec((tm, d), lambda i: (i, 0)),
                 _resident(w_gb.shape), _resident(w_glu.shape), _resident((1, db)),
                 _resident(w_pb.shape), _resident(w_out.shape)]
    args += [mod, g_norm, y, pa, w_gb, w_glu, b_glu, w_pb, w_out]
    return pl.pallas_call(
        functools.partial(_mixb_kernel, pos_blocks=pos_blocks if pos_tabs is not None else 0),
        grid=(ntok // tm,),
        in_specs=in_specs,
        out_specs=pl.BlockSpec((tm, d), lambda i: (i, 0)),
        out_shape=jax.ShapeDtypeStruct((ntok, d), F32),
        scratch_shapes=[pltpu.VMEM((db // LANES, tm, LANES), F32)],
        compiler_params=_params(("arbitrary",)),
        name="mixb",
    )(*args)


def _mlp_kernel(x_ref, mod_ref, g_ref, w1_ref, w2_ref, gf_ref, o_ref, h_ref, acc_ref):
    j = pl.program_id(1)
    nj = pl.num_programs(1)

    def ffn(h):
        hid = jnp.dot(h, w1_ref[...], preferred_element_type=F32)
        hid = jnp.square(jnp.maximum(hid, 0.0)).astype(BF16)
        return jnp.dot(hid, w2_ref[...], preferred_element_type=F32)

    @pl.when(j == 0)
    def _():
        h = _modnorm(x_ref[...], g_ref[...], mod_ref[3:4, :], mod_ref[4:5, :]).astype(BF16)
        h_ref[...] = h
        acc_ref[...] = ffn(h)

    @pl.when(jnp.logical_and(j > 0, j < nj - 1))
    def _():
        acc_ref[...] += ffn(h_ref[...])

    @pl.when(j == nj - 1)
    def _():
        x2 = x_ref[...] + mod_ref[5:6, :] * (acc_ref[...] + ffn(h_ref[...]))
        o_ref[...] = _rms(x2, gf_ref[...])


def _mlp_call(x1, mod, mod_row, g_norm, w1, w2, g_final, tm, tf):
    ntok, d = x1.shape
    dff = w1.shape[1]
    assert dff // tf >= 2
    return pl.pallas_call(
        _mlp_kernel,
        grid=(ntok // tm, dff // tf),
        in_specs=[pl.BlockSpec((tm, d), lambda i, j: (i, 0)),
                  pl.BlockSpec((None, N_MOD, d), lambda i, j: (mod_row(i), 0, 0)),
                  _resident((1, d)),
                  pl.BlockSpec((d, tf), lambda i, j: (0, j)),
                  pl.BlockSpec((tf, d), lambda i, j: (j, 0)),
                  _resident((1, d))],
        out_specs=pl.BlockSpec((tm, d), lambda i, j: (i, 0)),
        out_shape=jax.ShapeDtypeStruct((ntok, d), F32),
        scratch_shapes=[pltpu.VMEM((tm, d), BF16), pltpu.VMEM((tm, d), F32)],
        compiler_params=_params(("arbitrary", "arbitrary")),
        name="mlp",
    )(x1, mod, g_norm, w1, w2, g_final)


def _pos_tables(n_tokens, d):
    rows = n_tokens // GRID_W
    quarter = d // 4
    omega = 1.0 / (POS_BASE ** (jnp.arange(quarter, dtype=F32) / quarter))
    r = jnp.arange(rows, dtype=F32)[:, None] * omega
    col = jnp.arange(GRID_W, dtype=F32)[:, None] * omega
    e_r = jnp.concatenate([jnp.sin(r), jnp.cos(r)], axis=-1)
    e_c = jnp.concatenate([jnp.sin(col), jnp.cos(col)], axis=-1)
    return e_r, e_c


def _cmul(ar, ai, br, bi):
    return ar * br - ai * bi, ar * bi + ai * br


def _ssm_prepare(a_re, a_im, log_dt, b_re, b_im, c_re, c_im):
    ndir, g, p = a_re.shape
    ci = b_re.shape[-1]
    hi = lax.Precision.HIGHEST
    dt = jnp.exp(log_dt)[..., None]
    mag = jnp.exp(a_re * dt)
    ab_re, ab_im = mag * jnp.cos(a_im * dt), mag * jnp.sin(a_im * dt)
    den = a_re * a_re + a_im * a_im
    f_re, f_im = _cmul(ab_re - 1.0, ab_im, a_re / den, -a_im / den)
    bb_re, bb_im = _cmul(f_re[..., None], f_im[..., None], b_re, b_im)

    pr, pi = [jnp.ones_like(ab_re)], [jnp.zeros_like(ab_re)]
    for _ in range(T1):
        nr, ni = _cmul(pr[-1], pi[-1], ab_re, ab_im)
        pr.append(nr)
        pi.append(ni)
    pw_re, pw_im = jnp.stack(pr), jnp.stack(pi)

    ba_re, ba_im = _cmul(pw_re[:T1, ..., None], pw_im[:T1, ..., None], bb_re[None], bb_im[None])
    kk = (jnp.einsum('jdgpa,dgcp->djgac', ba_re, c_re, precision=hi)
          - jnp.einsum('jdgpa,dgcp->djgac', ba_im, c_im, precision=hi))
    kfull = jnp.concatenate([kk[1, :0:-1], (kk[0, 0] + kk[1, 0])[None], kk[0, 1:]], axis=0)
    tt = jnp.arange(T1)
    lag = tt[None, :] - tt[:, None] + (T1 - 1)
    toep = kfull[lag]
    toep_t = jnp.transpose(toep, (2, 1, 4, 0, 3)).reshape(g, T1 * ci, T1 * ci)

    def wend(ba, d, rev):
        w = ba[::-1, d] if rev else ba[:, d]
        return jnp.transpose(w, (1, 2, 0, 3)).reshape(g, p, T1 * ci)
    wend_t = jnp.concatenate([wend(ba_re, 0, True), wend(ba_im, 0, True),
                              wend(ba_re, 1, False), wend(ba_im, 1, False)], axis=1)
    w1t = jnp.concatenate([toep_t, wend_t], axis=1).astype(BF16)

    def wout(d, pows_re, pows_im):
        ca_re, ca_im = _cmul(c_re[d][None], c_im[d][None], pows_re[:, :, None, :], pows_im[:, :, None, :])
        to = lambda w: jnp.transpose(w, (1, 0, 2, 3)).reshape(g, T1 * ci, p)
        return to(ca_re), to(-ca_im)
    fo_re, fo_im = wout(0, pw_re[1:, 0], pw_im[1:, 0])
    bo_re, bo_im = wout(1, pw_re[:0:-1, 1], pw_im[:0:-1, 1])
    wout_t = jnp.concatenate([fo_re, fo_im, bo_re, bo_im], axis=2)
    wt = wout_t.reshape(g // 2, 2, T1 * ci, 4, p)
    eye = jnp.eye(2, dtype=F32)[None, :, None, None, :, None]
    woutp = (wt[:, :, :, :, None, :] * eye).reshape(g // 2, 2 * T1 * ci, 8 * p).astype(BF16)

    pw = jnp.stack([pw_re[T1, 0], pw_im[T1, 0], pw_re[T1, 1], pw_im[T1, 1]], axis=0)
    pw = jnp.transpose(pw.reshape(4, g // 2, 2 * p), (1, 0, 2))
    return w1t, woutp, pw


def _trunk(x, pos_tabs, mod, mod_row, h0p, lw, nsb, emit_state, tm):
    bsz, n, d = x.shape
    x2 = x.reshape(bsz * n, d)
    pos_blocks = n // tm
    pa, xb = _mixa_call(x2, pos_tabs, mod, mod_row, pos_blocks, lw["g_norm_mix"], lw["w_uvgx"], lw["g_sgu"],
                        lw["w_s"], lw["b_s"], lw["w_pa"], tm)
    res = _s5_call(xb, lw["w1t"], lw["woutp"], lw["pw"], lw["d_skip"], h0p, n, nsb, emit_state)
    x1 = _mixb_call(x2, pos_tabs, mod, mod_row, pos_blocks, lw["g_norm_mix"], res[0], pa, lw["w_gb"],
                    lw["w_glu"], lw["b_glu"], lw["w_pb"], lw["w_out"], tm)
    return x1, (res[1] if emit_state else None)


def kernel(x_prompt, x_sample, state_ssm_re, state_ssm_im, c, c_ctx, w_ada, b_ada, g_norm_mix, w_in,
           g_sgu, w_spatial, b_spatial, ssm_a_re, ssm_a_im, ssm_log_dt, ssm_b_re, ssm_b_im, ssm_c_re,
           ssm_c_im, ssm_d, w_glu, b_glu, w_proj_a, w_proj_b, w_out, g_norm_mlp, w_mlp_in, w_mlp_out,
           g_final):
    bp, sp, d = x_prompt.shape
    bs, ss, _ = x_sample.shape
    depth = w_in.shape[0]
    assert depth == 1, "positional embedding and final norm are fused assuming a single trunk layer"
    da = w_proj_a.shape[1]
    db = w_proj_b.shape[1]
    g, p = ssm_a_re.shape[2], ssm_a_re.shape[3]
    tm, tm_mlp = 256, 512
    tf = min(1024, w_mlp_in.shape[2] // 2)

    rows = -(-(bs + 1) // 8) * 8
    cc = jnp.zeros((rows, d), F32).at[:bs].set(c).at[bs].set(c_ctx)
    pos_tabs = _pos_tables(ss, d)
    l = 0
    mod = _mod_call(cc, w_ada[l], b_ada[l]).reshape(rows, N_MOD, d)
    w1t, woutp, pw = _ssm_prepare(ssm_a_re[l], ssm_a_im[l], ssm_log_dt[l], ssm_b_re[l],
                                  ssm_b_im[l], ssm_c_re[l], ssm_c_im[l])
    wi = w_in[l]
    lw = dict(
        g_norm_mix=g_norm_mix[l].reshape(1, d),
        w_uvgx=jnp.concatenate([wi[:, :2 * da], wi[:, 2 * da + db:2 * da + db + d],
                                wi[:, 2 * da:2 * da + db]], axis=1).astype(BF16),
        w_gb=wi[:, 2 * da + db + d:].astype(BF16),
        g_sgu=g_sgu[l].reshape(1, da),
        w_s=w_spatial[l].astype(BF16),
        b_s=b_spatial[l][:, :, None],
        w_pa=w_proj_a[l].astype(BF16),
        w_glu=w_glu[l].astype(BF16),
        b_glu=b_glu[l].reshape(1, db),
        w_pb=w_proj_b[l].astype(BF16),
        w_out=w_out[l].astype(BF16),
        w1t=w1t, woutp=woutp, pw=pw, d_skip=ssm_d[l].reshape(1, db))

    def pairs(st):
        return jnp.transpose(st.reshape(bs, 2, g // 2, 2 * p), (1, 2, 0, 3))
    sre, sim = pairs(state_ssm_re[:, l]), pairs(state_ssm_im[:, l])
    h0p = jnp.stack([sre[0], sim[0], sre[1], sim[1]], axis=1)[None]

    xp1, e = _trunk(x_prompt, None, mod, lambda i: bs, None, lw, bp, True, tm)
    xs1, _ = _trunk(x_sample, pos_tabs, mod, lambda i: i // (ss // tm), h0p, lw, bs, False, tm)

    gf = g_final.reshape(1, d)
    w1, w2 = w_mlp_in[l].astype(BF16), w_mlp_out[l].astype(BF16)
    gm = g_norm_mlp[l].reshape(1, d)
    xp = _mlp_call(xp1, mod, lambda i: bs, gm, w1, w2, gf, tm_mlp, tf).reshape(bp, sp, d)
    xs = _mlp_call(xs1, mod, lambda i: i // (ss // tm_mlp), gm, w1, w2, gf, tm_mlp, tf).reshape(bs, ss, d)

    ev = e.reshape(g // 2, 2, 2, bp, 2, p)
    fin = jnp.transpose(ev, (2, 3, 1, 0, 4, 5)).reshape(2, bp, 2, g, p)
    return xp, xs, fin[0][:, None], fin[1][:, None]
```

```python
import functools

import jax
import jax.numpy as jnp
from jax import lax
from jax.experimental import pallas as pl
from jax.experimental.pallas import tpu as pltpu

F32 = jnp.float32
BF16 = jnp.bfloat16

EPS = 1e-6
POS_BASE = 10000.0
GRID_W = 64
N_MOD = 6
T1 = 16
LANES = 128
SUBLANES = 8
VMEM_LIMIT = 56 * 1024 * 1024


def _gelu(x):
    return 0.5 * x * (1.0 + jnp.tanh(0.7978845608028654 * (x + 0.044715 * (x * x * x))))


def _sigmoid(x):
    return 1.0 / (1.0 + jnp.exp(-x))


def _rms(x, g):
    return x * lax.rsqrt(jnp.mean(x * x, axis=-1, keepdims=True) + EPS) * g


def _modnorm(x, g, shift, scale):
    return _rms(x, g) * (1.0 + scale) + shift


def _resident(shape):
    nd = len(shape)
    return pl.BlockSpec(shape, lambda *_: (0,) * nd, pipeline_mode=pl.Buffered(1))


def _params(sem):
    return pltpu.CompilerParams(dimension_semantics=sem, vmem_limit_bytes=VMEM_LIMIT)


def _add_pos(x, er_ref, ec_ref, tile_in_seq):
    tm, d = x.shape
    half = d // 2
    r0 = tile_in_seq * (tm // GRID_W)
    ec = ec_ref[...]
    parts = []
    for k in range(tm // GRID_W):
        er = jnp.broadcast_to(er_ref[pl.ds(r0 + k, 1), :], (GRID_W, half))
        parts.append(jnp.concatenate([er, ec], axis=1))
    return x + jnp.concatenate(parts, axis=0)


def _mod_kernel(c_ref, w_ref, b_ref, o_ref):
    cc = c_ref[...]
    s = cc * _sigmoid(cc)
    o_ref[...] = jnp.dot(s, w_ref[...], preferred_element_type=F32) + b_ref[...]


def _mod_call(cc, w_ada, b_ada):
    rows, d = cc.shape
    n = w_ada.shape[1]
    tn = n // (2 * N_MOD)
    assert n % tn == 0 and tn % LANES == 0
    return pl.pallas_call(
        _mod_kernel,
        grid=(n // tn,),
        in_specs=[pl.BlockSpec((rows, d), lambda j: (0, 0)),
                  pl.BlockSpec((d, tn), lambda j: (0, j)),
                  pl.BlockSpec((1, tn), lambda j: (0, j))],
        out_specs=pl.BlockSpec((rows, tn), lambda j: (0, j)),
        out_shape=jax.ShapeDtypeStruct((rows, n), F32),
        compiler_params=_params(("arbitrary",)),
        name="mod",
    )(cc, w_ada, b_ada.reshape(1, n))


def _mixa_kernel(*refs, pos_blocks, chunk, da):
    if pos_blocks:
        (x_ref, er_ref, ec_ref, mod_ref, g_ref, w_ref, gs_ref, ws_ref, bs_ref, wpa_ref,
         o_ref, xb_ref, ya_ref, xs_ref) = refs
        x = _add_pos(x_ref[...], er_ref, ec_ref, pl.program_id(0) % pos_blocks)
    else:
        (x_ref, mod_ref, g_ref, w_ref, gs_ref, ws_ref, bs_ref, wpa_ref,
         o_ref, xb_ref, ya_ref, xs_ref) = refs
        x = x_ref[...]
    tm, d = x.shape
    h = _modnorm(x, g_ref[...], mod_ref[0:1, :], mod_ref[1:2, :]).astype(BF16)
    def proj(c0, c1):
        return jnp.dot(h, w_ref[:, c0:c1], preferred_element_type=F32)

    zx = proj(2 * da + d, w_ref.shape[1])
    for cb in range(xs_ref.shape[0]):
        xs_ref[cb] = zx[:, cb * LANES:(cb + 1) * LANES]
        for t in range(T1):
            xb_ref[t, :, cb * LANES:(cb + 1) * LANES] = (
                xs_ref[cb, pl.ds(t, tm // T1, stride=T1), :].astype(xb_ref.dtype))
    u = _gelu(proj(0, da))
    v = _rms(_gelu(proj(da, 2 * da)), gs_ref[...]).astype(BF16)
    ng = ws_ref.shape[0]
    cg = da // ng
    for ck in range(tm // chunk):
        r0 = ck * chunk
        for gi in range(ng):
            c0 = gi * cg
            s = jnp.dot(ws_ref[gi], v[r0:r0 + chunk, c0:c0 + cg], preferred_element_type=F32)
            ya_ref[r0:r0 + chunk, c0:c0 + cg] = (u[r0:r0 + chunk, c0:c0 + cg] * (s + bs_ref[gi])).astype(BF16)
    pa = jnp.dot(ya_ref[...], wpa_ref[...], preferred_element_type=F32)
    o_ref[...] = (_sigmoid(proj(2 * da, 2 * da + d)) * pa).astype(o_ref.dtype)


def _mixa_call(x2, pos_tabs, mod, mod_row, pos_blocks, g_norm, w_uvgx, g_sgu, w_s, b_s, w_pa, tm):
    ntok, d = x2.shape
    da = w_pa.shape[0]
    db = w_uvgx.shape[1] - 2 * da - d
    chunk = w_s.shape[1]
    in_specs = [pl.BlockSpec((tm, d), lambda i: (i, 0))]
    args = [x2]
    if pos_tabs is not None:
        in_specs += [_resident(pos_tabs[0].shape), _resident(pos_tabs[1].shape)]
        args += list(pos_tabs)
    in_specs += [pl.BlockSpec((None, N_MOD, d), lambda i: (mod_row(i), 0, 0)),
                 _resident((1, d)), _resident(w_uvgx.shape), _resident((1, da)),
                 _resident(w_s.shape), _resident(b_s.shape), _resident(w_pa.shape)]
    args += [mod, g_norm, w_uvgx, g_sgu, w_s, b_s, w_pa]
    return pl.pallas_call(
        functools.partial(_mixa_kernel, pos_blocks=pos_blocks if pos_tabs is not None else 0,
                          chunk=chunk, da=da),
        grid=(ntok // tm,),
        in_specs=in_specs,
        out_specs=[pl.BlockSpec((tm, d), lambda i: (i, 0)),
                   pl.BlockSpec((T1, tm // T1, db), lambda i: (0, i, 0))],
        out_shape=[jax.ShapeDtypeStruct((ntok, d), BF16),
                   jax.ShapeDtypeStruct((T1, ntok // T1, db), BF16)],
        scratch_shapes=[pltpu.VMEM((tm, da), BF16), pltpu.VMEM((db // LANES, tm, LANES), F32)],
        compiler_params=_params(("arbitrary",)),
        name="mixa",
    )(*args)


def _s5_kernel(*refs, nsb, n_chunks, has_h0, emit_state, p, ci):
    it = iter(refs)
    x_ref, w1_ref, wout_ref, pw_ref, d_ref = next(it), next(it), next(it), next(it), next(it)
    h0_ref = next(it) if has_h0 else None
    y_ref = next(it)
    e_ref = next(it) if emit_state else None
    ut_ref, s_ref, hin_ref = next(it), next(it), next(it)

    ngl, kt, cols = ut_ref.shape
    npair = ngl // 2
    pitch = s_ref.shape[2] // nsb

    for t in range(T1):
        st = x_ref[t].astype(F32).T
        for gl in range(ngl):
            ut_ref[gl, t * ci:(t + 1) * ci, :] = st[gl * ci:(gl + 1) * ci, :]

    def local(pr, carry):
        r = []
        for hlf in range(2):
            gl = 2 * pr + hlf
            r1 = jnp.dot(w1_ref[gl], ut_ref[gl].astype(BF16), preferred_element_type=F32)
            ut_ref[gl] = r1[:kt]
            r.append(r1)
        for q in range(4):
            lo = kt + q * p
            sq = jnp.concatenate([r[0][lo:lo + p], r[1][lo:lo + p]], axis=0).T
            for b in range(nsb):
                s_ref[pr, q, b * pitch:b * pitch + n_chunks, :] = sq[b * n_chunks:(b + 1) * n_chunks]
        return carry

    lax.fori_loop(0, npair, local, 0)

    def rows(n):
        return pl.ds(n, nsb, stride=pitch) if nsb > 1 else pl.ds(n, 1)

    def step(n, hs):
        rf, rb = rows(n), rows(n_chunks - 1 - n)
        out = []
        for pr in range(npair):
            fr, fi, br, bi = hs[4 * pr:4 * pr + 4]
            afr, afi = pw_ref[pr, 0:1, :], pw_ref[pr, 1:2, :]
            abr, abi = pw_ref[pr, 2:3, :], pw_ref[pr, 3:4, :]
            sfr, sfi = s_ref[pr, 0, rf, :], s_ref[pr, 1, rf, :]
            sbr, sbi = s_ref[pr, 2, rb, :], s_ref[pr, 3, rb, :]
            hin_ref[pr, 0, rf, :] = fr
            hin_ref[pr, 1, rf, :] = fi
            hin_ref[pr, 2, rb, :] = br
            hin_ref[pr, 3, rb, :] = bi
            out += [afr * fr - afi * fi + sfr, afr * fi + afi * fr + sfi,
                    abr * br - abi * bi + sbr, abr * bi + abi * br + sbi]
        return tuple(out)

    if has_h0:
        init = tuple(h0_ref[pr, q] for pr in range(npair) for q in range(4))
    else:
        init = tuple(jnp.zeros((nsb, LANES), F32) for _ in range(4 * npair))
    fin = lax.fori_loop(0, n_chunks, step, init)
    if emit_state:
        for pr in range(npair):
            for q in range(4):
                e_ref[pr, q] = fin[4 * pr + q]

    def inter(pr, carry):
        hin = jnp.concatenate(
            [jnp.concatenate([hin_ref[pr, q, b * pitch:b * pitch + n_chunks, :] for b in range(nsb)], axis=0)
             for q in range(4)], axis=1).astype(BF16)
        yy = lax.dot_general(wout_ref[pr], hin, (((1,), (1,)), ((), ())), preferred_element_type=F32)
        ut_ref[2 * pr] += yy[:kt]
        ut_ref[2 * pr + 1] += yy[kt:]
        return carry

    lax.fori_loop(0, npair, inter, 0)

    for t in range(T1):
        yt = jnp.concatenate([ut_ref[gl, t * ci:(t + 1) * ci, :] for gl in range(ngl)], axis=0)
        y_ref[t] = _gelu(yt.T + d_ref[...] * x_ref[t].astype(F32)).astype(y_ref.dtype)


def _s5_call(xb, w1t, woutp, pw, d_skip, h0p, n, nsb, emit_state):
    _, ncol, db = xb.shape
    g, kt = w1t.shape[0], w1t.shape[2]
    p = (w1t.shape[1] - kt) // 4
    ci = db // g
    ngl = LANES // ci
    npair = ngl // 2
    assert 2 * p == LANES and ngl % 2 == 0
    n_chunks = n // T1
    cols = nsb * n_chunks
    has_h0 = h0p is not None
    in_specs = [pl.BlockSpec((T1, cols, LANES), lambda i, j: (0, j, i)),
                pl.BlockSpec((ngl,) + w1t.shape[1:], lambda i, j: (i, 0, 0)),
                pl.BlockSpec((npair,) + woutp.shape[1:], lambda i, j: (i, 0, 0)),
                pl.BlockSpec((npair,) + pw.shape[1:], lambda i, j: (i, 0, 0)),
                pl.BlockSpec((1, LANES), lambda i, j: (0, i))]
    args = [xb, w1t, woutp, pw, d_skip]
    if has_h0:
        in_specs.append(pl.BlockSpec((None, npair, 4, nsb, LANES), lambda i, j: (j, i, 0, 0, 0)))
        args.append(h0p)
    out_specs = [pl.BlockSpec((T1, cols, LANES), lambda i, j: (0, j, i))]
    out_shape = [jax.ShapeDtypeStruct((T1, ncol, db), xb.dtype)]
    if emit_state:
        out_specs.append(pl.BlockSpec((None, npair, 4, nsb, LANES), lambda i, j: (j, i, 0, 0, 0)))
        out_shape.append(jax.ShapeDtypeStruct((ncol // cols, g // 2, 4, nsb, LANES), F32))
    return pl.pallas_call(
        functools.partial(_s5_kernel, nsb=nsb, n_chunks=n_chunks, has_h0=has_h0, emit_state=emit_state,
                          p=p, ci=ci),
        grid=(g // ngl, ncol // cols),
        in_specs=in_specs,
        out_specs=out_specs,
        out_shape=out_shape,
        scratch_shapes=[pltpu.VMEM((ngl, kt, cols), F32)]
        + [pltpu.VMEM((npair, 4, nsb * (n_chunks + SUBLANES), LANES), F32)] * 2,
        compiler_params=_params(("arbitrary", "arbitrary")),
        name="s5",
    )(*args)


def _mixb_kernel(*refs, pos_blocks):
    if pos_blocks:
        (x_ref, er_ref, ec_ref, mod_ref, g_ref, y_ref, pa_ref, wgb_ref, wglu_ref, bglu_ref,
         wpb_ref, wout_ref, o_ref, ys_ref) = refs
        x = _add_pos(x_ref[...], er_ref, ec_ref, pl.program_id(0) % pos_blocks)
    else:
        (x_ref, mod_ref, g_ref, y_ref, pa_ref, wgb_ref, wglu_ref, bglu_ref,
         wpb_ref, wout_ref, o_ref, ys_ref) = refs
        x = x_ref[...]
    tm = x.shape[0]
    h = _modnorm(x, g_ref[...], mod_ref[0:1, :], mod_ref[1:2, :]).astype(BF16)
    for cb in range(ys_ref.shape[0]):
        for t in range(T1):
            ys_ref[cb, pl.ds(t, tm // T1, stride=T1), :] = y_ref[t, :, cb * LANES:(cb + 1) * LANES].astype(F32)
    y = jnp.concatenate([ys_ref[cb] for cb in range(ys_ref.shape[0])], axis=1)
    gate = _sigmoid(jnp.dot(y.astype(BF16), wglu_ref[...], preferred_element_type=F32) + bglu_ref[...])
    yb = (y * gate).astype(BF16)
    pb = jnp.dot(yb, wpb_ref[...], preferred_element_type=F32)
    pb = _sigmoid(jnp.dot(h, wgb_ref[...], preferred_element_type=F32)) * pb
    m = (pa_ref[...].astype(F32) + pb).astype(BF16)
    o_ref[...] = x + mod_ref[2:3, :] * jnp.dot(m, wout_ref[...], preferred_element_type=F32)


def _mixb_call(x2, pos_tabs, mod, mod_row, pos_blocks, g_norm, y, pa, w_gb, w_glu, b_glu, w_pb, w_out, tm):
    ntok, d = x2.shape
    db = y.shape[2]
    in_specs = [pl.BlockSpec((tm, d), lambda i: (i, 0))]
    args = [x2]
    if pos_tabs is not None:
        in_specs += [_resident(pos_tabs[0].shape), _resident(pos_tabs[1].shape)]
        args += list(pos_tabs)
    in_specs += [pl.BlockSpec((None, N_MOD, d), lambda i: (mod_row(i), 0, 0)),
                 _resident((1, d)),
                 pl.BlockSpec((T1, tm // T1, db), lambda i: (0, i, 0)),
                 pl.BlockSpec((tm, d), lambda i: (i, 0)),
                 _resident(w_gb.shape), _resident(w_glu.shape), _resident((1, db)),
                 _resident(w_pb.shape), _resident(w_out.shape)]
    args += [mod, g_norm, y, pa, w_gb, w_glu, b_glu, w_pb, w_out]
    return pl.pallas_call(
        functools.partial(_mixb_kernel, pos_blocks=pos_blocks if pos_tabs is not None else 0),
        grid=(ntok // tm,),
        in_specs=in_specs,
        out_specs=pl.BlockSpec((tm, d), lambda i: (i, 0)),
        out_shape=jax.ShapeDtypeStruct((ntok, d), F32),
        scratch_shapes=[pltpu.VMEM((db // LANES, tm, LANES), F32)],
        compiler_params=_params(("arbitrary",)),
        name="mixb",
    )(*args)


def _mlp_kernel(x_ref, mod_ref, g_ref, w1_ref, w2_ref, gf_ref, o_ref, h_ref, acc_ref):
    j = pl.program_id(1)
    nj = pl.num_programs(1)

    def ffn(h):
        hid = jnp.dot(h, w1_ref[...], preferred_element_type=F32)
        hid = jnp.square(jnp.maximum(hid, 0.0)).astype(BF16)
        return jnp.dot(hid, w2_ref[...], preferred_element_type=F32)

    @pl.when(j == 0)
    def _():
        h = _modnorm(x_ref[...], g_ref[...], mod_ref[3:4, :], mod_ref[4:5, :]).astype(BF16)
        h_ref[...] = h
        acc_ref[...] = ffn(h)

    @pl.when(jnp.logical_and(j > 0, j < nj - 1))
    def _():
        acc_ref[...] += ffn(h_ref[...])

    @pl.when(j == nj - 1)
    def _():
        x2 = x_ref[...] + mod_ref[5:6, :] * (acc_ref[...] + ffn(h_ref[...]))
        o_ref[...] = _rms(x2, gf_ref[...])


def _mlp_call(x1, mod, mod_row, g_norm, w1, w2, g_final, tm, tf):
    ntok, d = x1.shape
    dff = w1.shape[1]
    assert dff // tf >= 2
    return pl.pallas_call(
        _mlp_kernel,
        grid=(ntok // tm, dff // tf),
        in_specs=[pl.BlockSpec((tm, d), lambda i, j: (i, 0)),
                  pl.BlockSpec((None, N_MOD, d), lambda i, j: (mod_row(i), 0, 0)),
                  _resident((1, d)),
                  pl.BlockSpec((d, tf), lambda i, j: (0, j)),
                  pl.BlockSpec((tf, d), lambda i, j: (j, 0)),
                  _resident((1, d))],
        out_specs=pl.BlockSpec((tm, d), lambda i, j: (i, 0)),
        out_shape=jax.ShapeDtypeStruct((ntok, d), F32),
        scratch_shapes=[pltpu.VMEM((tm, d), BF16), pltpu.VMEM((tm, d), F32)],
        compiler_params=_params(("arbitrary", "arbitrary")),
        name="mlp",
    )(x1, mod, g_norm, w1, w2, g_final)


def _pos_tables(n_tokens, d):
    rows = n_tokens // GRID_W
    quarter = d // 4
    omega = 1.0 / (POS_BASE ** (jnp.arange(quarter, dtype=F32) / quarter))
    r = jnp.arange(rows, dtype=F32)[:, None] * omega
    col = jnp.arange(GRID_W, dtype=F32)[:, None] * omega
    e_r = jnp.concatenate([jnp.sin(r), jnp.cos(r)], axis=-1)
    e_c = jnp.concatenate([jnp.sin(col), jnp.cos(col)], axis=-1)
    return e_r, e_c


def _cmul(ar, ai, br, bi):
    return ar * br - ai * bi, ar * bi + ai * br


def _ssm_prepare(a_re, a_im, log_dt, b_re, b_im, c_re, c_im):
    ndir, g, p = a_re.shape
    ci = b_re.shape[-1]
    hi = lax.Precision.HIGHEST
    dt = jnp.exp(log_dt)[..., None]
    mag = jnp.exp(a_re * dt)
    ab_re, ab_im = mag * jnp.cos(a_im * dt), mag * jnp.sin(a_im * dt)
    den = a_re * a_re + a_im * a_im
    f_re, f_im = _cmul(ab_re - 1.0, ab_im, a_re / den, -a_im / den)
    bb_re, bb_im = _cmul(f_re[..., None], f_im[..., None], b_re, b_im)

    pr, pi = [jnp.ones_like(ab_re)], [jnp.zeros_like(ab_re)]
    for _ in range(T1):
        nr, ni = _cmul(pr[-1], pi[-1], ab_re, ab_im)
        pr.append(nr)
        pi.append(ni)

    def ba(d, order):
        pcs = [_cmul(pr[j][d][..., None], pi[j][d][..., None], bb_re[d], bb_im[d]) for j in order]
        return (jnp.concatenate([q[0] for q in pcs], axis=-1), jnp.concatenate([q[1] for q in pcs], axis=-1))

    bf_re, bf_im = ba(0, range(T1 - 1, -1, -1))
    bb1_re, bb1_im = ba(1, range(T1))
    wend_t = jnp.concatenate([bf_re, bf_im, bb1_re, bb1_im], axis=1)

    def lagk(d, x_re, x_im):
        return (jnp.einsum('gcp,gpl->gcl', c_re[d], x_re, precision=hi)
                - jnp.einsum('gcp,gpl->gcl', c_im[d], x_im, precision=hi))
    kf, kb = lagk(0, bf_re, bf_im), lagk(1, bb1_re, bb1_im)
    mid = (T1 - 1) * ci
    r = jnp.concatenate([kf[..., :mid], kf[..., mid:] + kb[..., :ci], kb[..., ci:]], axis=-1)
    toep_t = jnp.stack([r[..., (T1 - 1 - t) * ci:(2 * T1 - 1 - t) * ci] for t in range(T1)], axis=1)
    toep_t = toep_t.reshape(g, T1 * ci, T1 * ci)
    w1t = jnp.concatenate([toep_t, wend_t], axis=1).astype(BF16)

    def wout(d, order):
        pw_r = jnp.stack([pr[j][d] for j in order], axis=1)[:, :, None, :]
        pw_i = jnp.stack([pi[j][d] for j in order], axis=1)[:, :, None, :]
        ca_re, ca_im = _cmul(c_re[d][:, None], c_im[d][:, None], pw_r, pw_i)
        return ca_re.reshape(g, T1 * ci, p), (-ca_im).reshape(g, T1 * ci, p)
    qs = wout(0, range(1, T1 + 1)) + wout(1, range(T1, 0, -1))
    z = jnp.zeros((g // 2, T1 * ci, p), F32)
    top = jnp.concatenate([x for q in qs for x in (q[0::2], z)], axis=-1)
    bot = jnp.concatenate([x for q in qs for x in (z, q[1::2])], axis=-1)
    woutp = jnp.concatenate([top, bot], axis=1).astype(BF16)

    pw = jnp.stack([pr[T1][0], pi[T1][0], pr[T1][1], pi[T1][1]], axis=0)
    pw = jnp.transpose(pw.reshape(4, g // 2, 2 * p), (1, 0, 2))
    return w1t, woutp, pw


def _trunk(x, pos_tabs, mod, mod_row, h0p, lw, nsb, emit_state, tm_a, tm_b):
    bsz, n, d = x.shape
    x2 = x.reshape(bsz * n, d)
    pa, xb = _mixa_call(x2, pos_tabs, mod, mod_row(tm_a), n // tm_a, lw["g_norm_mix"], lw["w_uvgx"],
                        lw["g_sgu"], lw["w_s"], lw["b_s"], lw["w_pa"], tm_a)
    res = _s5_call(xb, lw["w1t"], lw["woutp"], lw["pw"], lw["d_skip"], h0p, n, nsb, emit_state)
    x1 = _mixb_call(x2, pos_tabs, mod, mod_row(tm_b), n // tm_b, lw["g_norm_mix"], res[0], pa, lw["w_gb"],
                    lw["w_glu"], lw["b_glu"], lw["w_pb"], lw["w_out"], tm_b)
    return x1, (res[1] if emit_state else None)


def kernel(x_prompt, x_sample, state_ssm_re, state_ssm_im, c, c_ctx, w_ada, b_ada, g_norm_mix, w_in,
           g_sgu, w_spatial, b_spatial, ssm_a_re, ssm_a_im, ssm_log_dt, ssm_b_re, ssm_b_im, ssm_c_re,
           ssm_c_im, ssm_d, w_glu, b_glu, w_proj_a, w_proj_b, w_out, g_norm_mlp, w_mlp_in, w_mlp_out,
           g_final):
    bp, sp, d = x_prompt.shape
    bs, ss, _ = x_sample.shape
    depth = w_in.shape[0]
    assert depth == 1, "positional embedding and final norm are fused assuming a single trunk layer"
    da = w_proj_a.shape[1]
    db = w_proj_b.shape[1]
    g, p = ssm_a_re.shape[2], ssm_a_re.shape[3]
    tm_a, tm_b, tm_mlp = 512, 256, 512
    tf = min(1024, w_mlp_in.shape[2] // 2)

    rows = -(-(bs + 1) // 8) * 8
    cc = jnp.zeros((rows, d), F32).at[:bs].set(c).at[bs].set(c_ctx)
    pos_tabs = _pos_tables(ss, d)
    l = 0
    mod = _mod_call(cc, w_ada[l], b_ada[l]).reshape(rows, N_MOD, d)
    w1t, woutp, pw = _ssm_prepare(ssm_a_re[l], ssm_a_im[l], ssm_log_dt[l], ssm_b_re[l],
                                  ssm_b_im[l], ssm_c_re[l], ssm_c_im[l])
    wi = w_in[l]
    lw = dict(
        g_norm_mix=g_norm_mix[l].reshape(1, d),
        w_uvgx=jnp.concatenate([wi[:, :2 * da], wi[:, 2 * da + db:2 * da + db + d],
                                wi[:, 2 * da:2 * da + db]], axis=1).astype(BF16),
        w_gb=wi[:, 2 * da + db + d:].astype(BF16),
        g_sgu=g_sgu[l].reshape(1, da),
        w_s=w_spatial[l].astype(BF16),
        b_s=b_spatial[l][:, :, None],
        w_pa=w_proj_a[l].astype(BF16),
        w_glu=w_glu[l].astype(BF16),
        b_glu=b_glu[l].reshape(1, db),
        w_pb=w_proj_b[l].astype(BF16),
        w_out=w_out[l].astype(BF16),
        w1t=w1t, woutp=woutp, pw=pw, d_skip=ssm_d[l].reshape(1, db))

    def pairs(st):
        return jnp.transpose(st.reshape(bs, 2, g // 2, 2 * p), (1, 2, 0, 3))
    sre, sim = pairs(state_ssm_re[:, l]), pairs(state_ssm_im[:, l])
    h0p = jnp.stack([sre[0], sim[0], sre[1], sim[1]], axis=1)[None]

    ctx_row = lambda tm: (lambda i: bs)
    seq_row = lambda tm: (lambda i: i // (ss // tm))
    xp1, e = _trunk(x_prompt, None, mod, ctx_row, None, lw, bp, True, tm_a, tm_b)
    xs1, _ = _trunk(x_sample, pos_tabs, mod, seq_row, h0p, lw, bs, False, tm_a, tm_b)

    gf = g_final.reshape(1, d)
    w1, w2 = w_mlp_in[l].astype(BF16), w_mlp_out[l].astype(BF16)
    gm = g_norm_mlp[l].reshape(1, d)
    xp = _mlp_call(xp1, mod, ctx_row(tm_mlp), gm, w1, w2, gf, tm_mlp, tf).reshape(bp, sp, d)
    xs = _mlp_call(xs1, mod, seq_row(tm_mlp), gm, w1, w2, gf, tm_mlp, tf).reshape(bs, ss, d)

    ev = e.reshape(g // 2, 2, 2, bp, 2, p)
    fin = jnp.transpose(ev, (2, 3, 1, 0, 4, 5)).reshape(2, bp, 2, g, p)
    return xp, xs, fin[0][:, None], fin[1][:, None]
```

```python
import functools

import jax
import jax.numpy as jnp
from jax import lax
from jax.experimental import pallas as pl
from jax.experimental.pallas import tpu as pltpu

F32 = jnp.float32
BF16 = jnp.bfloat16

EPS = 1e-6
POS_BASE = 10000.0
GRID_W = 64
N_MOD = 6
T1 = 16
LANES = 128
SUBLANES = 8
VMEM_LIMIT = 56 * 1024 * 1024


def _gelu(x):
    return 0.5 * x * (1.0 + jnp.tanh(0.7978845608028654 * (x + 0.044715 * (x * x * x))))


def _sigmoid(x):
    return 1.0 / (1.0 + jnp.exp(-x))


def _rms(x, g):
    return x * lax.rsqrt(jnp.mean(x * x, axis=-1, keepdims=True) + EPS) * g


def _modnorm(x, g, shift, scale):
    return _rms(x, g) * (1.0 + scale) + shift


def _resident(shape):
    nd = len(shape)
    return pl.BlockSpec(shape, lambda *_: (0,) * nd, pipeline_mode=pl.Buffered(1))


def _params(sem):
    return pltpu.CompilerParams(dimension_semantics=sem, vmem_limit_bytes=VMEM_LIMIT)


def _add_pos(x, er_ref, ec_ref, tile_in_seq):
    tm, d = x.shape
    half = d // 2
    r0 = tile_in_seq * (tm // GRID_W)
    ec = ec_ref[...]
    parts = []
    for k in range(tm // GRID_W):
        er = jnp.broadcast_to(er_ref[pl.ds(r0 + k, 1), :], (GRID_W, half))
        parts.append(jnp.concatenate([er, ec], axis=1))
    return x + jnp.concatenate(parts, axis=0)


def _mod_kernel(c_ref, w_ref, b_ref, o_ref):
    cc = c_ref[...]
    s = cc * _sigmoid(cc)
    o_ref[...] = jnp.dot(s, w_ref[...], preferred_element_type=F32) + b_ref[...]


def _mod_call(cc, w_ada, b_ada):
    rows, d = cc.shape
    n = w_ada.shape[1]
    tn = n // (2 * N_MOD)
    assert n % tn == 0 and tn % LANES == 0
    return pl.pallas_call(
        _mod_kernel,
        grid=(n // tn,),
        in_specs=[pl.BlockSpec((rows, d), lambda j: (0, 0)),
                  pl.BlockSpec((d, tn), lambda j: (0, j)),
                  pl.BlockSpec((1, tn), lambda j: (0, j))],
        out_specs=pl.BlockSpec((rows, tn), lambda j: (0, j)),
        out_shape=jax.ShapeDtypeStruct((rows, n), F32),
        compiler_params=_params(("arbitrary",)),
        name="mod",
    )(cc, w_ada, b_ada.reshape(1, n))


def _mixa_kernel(*refs, pos_blocks, chunk, da):
    if pos_blocks:
        (x_ref, er_ref, ec_ref, mod_ref, g_ref, w_ref, gs_ref, ws_ref, bs_ref, wpa_ref,
         o_ref, xb_ref, ya_ref, xs_ref) = refs
        x = _add_pos(x_ref[...], er_ref, ec_ref, pl.program_id(0) % pos_blocks)
    else:
        (x_ref, mod_ref, g_ref, w_ref, gs_ref, ws_ref, bs_ref, wpa_ref,
         o_ref, xb_ref, ya_ref, xs_ref) = refs
        x = x_ref[...]
    tm, d = x.shape
    h = _modnorm(x, g_ref[...], mod_ref[0:1, :], mod_ref[1:2, :]).astype(BF16)
    z = jnp.dot(h, w_ref[...], preferred_element_type=F32)
    for cb in range(xs_ref.shape[0]):
        c0 = 2 * da + d + cb * LANES
        xs_ref[cb] = z[:, c0:c0 + LANES]
        for t in range(T1):
            xb_ref[t, :, cb * LANES:(cb + 1) * LANES] = (
                xs_ref[cb, pl.ds(t, tm // T1, stride=T1), :].astype(xb_ref.dtype))
    u = _gelu(z[:, :da])
    v = _rms(_gelu(z[:, da:2 * da]), gs_ref[...]).astype(BF16)
    ng = ws_ref.shape[0]
    cg = da // ng
    for ck in range(tm // chunk):
        r0 = ck * chunk
        for gi in range(ng):
            c0 = gi * cg
            s = jnp.dot(ws_ref[gi], v[r0:r0 + chunk, c0:c0 + cg], preferred_element_type=F32)
            ya_ref[r0:r0 + chunk, c0:c0 + cg] = (u[r0:r0 + chunk, c0:c0 + cg] * (s + bs_ref[gi])).astype(BF16)
    pa = jnp.dot(ya_ref[...], wpa_ref[...], preferred_element_type=F32)
    o_ref[...] = (_sigmoid(z[:, 2 * da:2 * da + d]) * pa).astype(o_ref.dtype)


def _mixa_call(x2, pos_tabs, mod, mod_row, pos_blocks, g_norm, w_uvgx, g_sgu, w_s, b_s, w_pa, tm):
    ntok, d = x2.shape
    da = w_pa.shape[0]
    db = w_uvgx.shape[1] - 2 * da - d
    chunk = w_s.shape[1]
    in_specs = [pl.BlockSpec((tm, d), lambda i: (i, 0))]
    args = [x2]
    if pos_tabs is not None:
        in_specs += [_resident(pos_tabs[0].shape), _resident(pos_tabs[1].shape)]
        args += list(pos_tabs)
    in_specs += [pl.BlockSpec((None, N_MOD, d), lambda i: (mod_row(i), 0, 0)),
                 _resident((1, d)), _resident(w_uvgx.shape), _resident((1, da)),
                 _resident(w_s.shape), _resident(b_s.shape), _resident(w_pa.shape)]
    args += [mod, g_norm, w_uvgx, g_sgu, w_s, b_s, w_pa]
    return pl.pallas_call(
        functools.partial(_mixa_kernel, pos_blocks=pos_blocks if pos_tabs is not None else 0,
                          chunk=chunk, da=da),
        grid=(ntok // tm,),
        in_specs=in_specs,
        out_specs=[pl.BlockSpec((tm, d), lambda i: (i, 0)),
                   pl.BlockSpec((T1, tm // T1, db), lambda i: (0, i, 0))],
        out_shape=[jax.ShapeDtypeStruct((ntok, d), BF16),
                   jax.ShapeDtypeStruct((T1, ntok // T1, db), BF16)],
        scratch_shapes=[pltpu.VMEM((tm, da), BF16), pltpu.VMEM((db // LANES, tm, LANES), F32)],
        compiler_params=_params(("arbitrary",)),
        name="mixa",
    )(*args)


def _s5_kernel(*refs, nsb, n_chunks, has_h0, emit_state, p, ci):
    it = iter(refs)
    x_ref, w1_ref, wout_ref, pw_ref, d_ref = next(it), next(it), next(it), next(it), next(it)
    h0_ref = next(it) if has_h0 else None
    y_ref = next(it)
    e_ref = next(it) if emit_state else None
    ut_ref, s_ref, hin_ref = next(it), next(it), next(it)

    ngl, kt, cols = ut_ref.shape
    npair = ngl // 2
    pitch = s_ref.shape[2] // nsb

    for t in range(T1):
        st = x_ref[t].astype(F32).T
        for gl in range(ngl):
            ut_ref[gl, t * ci:(t + 1) * ci, :] = st[gl * ci:(gl + 1) * ci, :]

    def local(pr, carry):
        r = []
        for hlf in range(2):
            gl = 2 * pr + hlf
            r1 = jnp.dot(w1_ref[gl], ut_ref[gl].astype(BF16), preferred_element_type=F32)
            ut_ref[gl] = r1[:kt]
            r.append(r1)
        for q in range(4):
            lo = kt + q * p
            sq = jnp.concatenate([r[0][lo:lo + p], r[1][lo:lo + p]], axis=0).T
            for b in range(nsb):
                s_ref[pr, q, b * pitch:b * pitch + n_chunks, :] = sq[b * n_chunks:(b + 1) * n_chunks]
        return carry

    lax.fori_loop(0, npair, local, 0)

    def rows(n):
        return pl.ds(n, nsb, stride=pitch) if nsb > 1 else pl.ds(n, 1)

    def step(n, hs):
        rf, rb = rows(n), rows(n_chunks - 1 - n)
        out = []
        for pr in range(npair):
            fr, fi, br, bi = hs[4 * pr:4 * pr + 4]
            afr, afi = pw_ref[pr, 0:1, :], pw_ref[pr, 1:2, :]
            abr, abi = pw_ref[pr, 2:3, :], pw_ref[pr, 3:4, :]
            sfr, sfi = s_ref[pr, 0, rf, :], s_ref[pr, 1, rf, :]
            sbr, sbi = s_ref[pr, 2, rb, :], s_ref[pr, 3, rb, :]
            hin_ref[pr, 0, rf, :] = fr
            hin_ref[pr, 1, rf, :] = fi
            hin_ref[pr, 2, rb, :] = br
            hin_ref[pr, 3, rb, :] = bi
            out += [afr * fr - afi * fi + sfr, afr * fi + afi * fr + sfi,
                    abr * br - abi * bi + sbr, abr * bi + abi * br + sbi]
        return tuple(out)

    if has_h0:
        init = tuple(h0_ref[pr, q] for pr in range(npair) for q in range(4))
    else:
        init = tuple(jnp.zeros((nsb, LANES), F32) for _ in range(4 * npair))
    fin = lax.fori_loop(0, n_chunks, step, init)
    if emit_state:
        for pr in range(npair):
            for q in range(4):
                e_ref[pr, q] = fin[4 * pr + q]

    first = lax.broadcasted_iota(jnp.int32, (cols, 8 * p), 1) % LANES < p

    def inter(pr, carry):
        hin = jnp.concatenate(
            [jnp.concatenate([hin_ref[pr, q, b * pitch:b * pitch + n_chunks, :] for b in range(nsb)], axis=0)
             for q in range(4)], axis=1)
        for hlf, hg in enumerate((jnp.where(first, hin, 0.0), jnp.where(first, 0.0, hin))):
            gl = 2 * pr + hlf
            ut_ref[gl] += lax.dot_general(wout_ref[gl], hg.astype(BF16), (((1,), (1,)), ((), ())),
                                          preferred_element_type=F32)
        return carry

    lax.fori_loop(0, npair, inter, 0)

    for t in range(T1):
        yt = jnp.concatenate([ut_ref[gl, t * ci:(t + 1) * ci, :] for gl in range(ngl)], axis=0)
        y_ref[t] = _gelu(yt.T + d_ref[...] * x_ref[t].astype(F32)).astype(y_ref.dtype)


def _s5_call(xb, w1t, wout2, pw, d_skip, h0p, n, nsb, emit_state):
    _, ncol, db = xb.shape
    g, kt = w1t.shape[0], w1t.shape[2]
    p = (w1t.shape[1] - kt) // 4
    ci = db // g
    ngl = LANES // ci
    npair = ngl // 2
    assert 2 * p == LANES and ngl % 2 == 0
    n_chunks = n // T1
    cols = nsb * n_chunks
    has_h0 = h0p is not None
    in_specs = [pl.BlockSpec((T1, cols, LANES), lambda i, j: (0, j, i)),
                pl.BlockSpec((ngl,) + w1t.shape[1:], lambda i, j: (i, 0, 0)),
                pl.BlockSpec((ngl,) + wout2.shape[1:], lambda i, j: (i, 0, 0)),
                pl.BlockSpec((npair,) + pw.shape[1:], lambda i, j: (i, 0, 0)),
                pl.BlockSpec((1, LANES), lambda i, j: (0, i))]
    args = [xb, w1t, wout2, pw, d_skip]
    if has_h0:
        in_specs.append(pl.BlockSpec((None, npair, 4, nsb, LANES), lambda i, j: (j, i, 0, 0, 0)))
        args.append(h0p)
    out_specs = [pl.BlockSpec((T1, cols, LANES), lambda i, j: (0, j, i))]
    out_shape = [jax.ShapeDtypeStruct((T1, ncol, db), xb.dtype)]
    if emit_state:
        out_specs.append(pl.BlockSpec((None, npair, 4, nsb, LANES), lambda i, j: (j, i, 0, 0, 0)))
        out_shape.append(jax.ShapeDtypeStruct((ncol // cols, g // 2, 4, nsb, LANES), F32))
    return pl.pallas_call(
        functools.partial(_s5_kernel, nsb=nsb, n_chunks=n_chunks, has_h0=has_h0, emit_state=emit_state,
                          p=p, ci=ci),
        grid=(g // ngl, ncol // cols),
        in_specs=in_specs,
        out_specs=out_specs,
        out_shape=out_shape,
        scratch_shapes=[pltpu.VMEM((ngl, kt, cols), F32)]
        + [pltpu.VMEM((npair, 4, nsb * (n_chunks + SUBLANES), LANES), F32)] * 2,
        compiler_params=_params(("arbitrary", "arbitrary")),
        name="s5",
    )(*args)


def _mixb_kernel(*refs, pos_blocks):
    if pos_blocks:
        (x_ref, er_ref, ec_ref, mod_ref, g_ref, y_ref, pa_ref, wgb_ref, wglu_ref, bglu_ref,
         wpb_ref, wout_ref, o_ref, ys_ref) = refs
        x = _add_pos(x_ref[...], er_ref, ec_ref, pl.program_id(0) % pos_blocks)
    else:
        (x_ref, mod_ref, g_ref, y_ref, pa_ref, wgb_ref, wglu_ref, bglu_ref,
         wpb_ref, wout_ref, o_ref, ys_ref) = refs
        x = x_ref[...]
    tm = x.shape[0]
    h = _modnorm(x, g_ref[...], mod_ref[0:1, :], mod_ref[1:2, :]).astype(BF16)
    for cb in range(ys_ref.shape[0]):
        for t in range(T1):
            ys_ref[cb, pl.ds(t, tm // T1, stride=T1), :] = y_ref[t, :, cb * LANES:(cb + 1) * LANES].astype(F32)
    y = jnp.concatenate([ys_ref[cb] for cb in range(ys_ref.shape[0])], axis=1)
    gate = _sigmoid(jnp.dot(y.astype(BF16), wglu_ref[...], preferred_element_type=F32) + bglu_ref[...])
    yb = (y * gate).astype(BF16)
    pb = jnp.dot(yb, wpb_ref[...], preferred_element_type=F32)
    pb = _sigmoid(jnp.dot(h, wgb_ref[...], preferred_element_type=F32)) * pb
    m = (pa_ref[...].astype(F32) + pb).astype(BF16)
    o_ref[...] = x + mod_ref[2:3, :] * jnp.dot(m, wout_ref[...], preferred_element_type=F32)


def _mixb_call(x2, pos_tabs, mod, mod_row, pos_blocks, g_norm, y, pa, w_gb, w_glu, b_glu, w_pb, w_out, tm):
    ntok, d = x2.shape
    db = y.shape[2]
    in_specs = [pl.BlockSpec((tm, d), lambda i: (i, 0))]
    args = [x2]
    if pos_tabs is not None:
        in_specs += [_resident(pos_tabs[0].shape), _resident(pos_tabs[1].shape)]
        args += list(pos_tabs)
    in_specs += [pl.BlockSpec((None, N_MOD, d), lambda i: (mod_row(i), 0, 0)),
                 _resident((1, d)),
                 pl.BlockSpec((T1, tm // T1, db), lambda i: (0, i, 0)),
                 pl.BlockSpec((tm, d), lambda i: (i, 0)),
                 _resident(w_gb.shape), _resident(w_glu.shape), _resident((1, db)),
                 _resident(w_pb.shape), _resident(w_out.shape)]
    args += [mod, g_norm, y, pa, w_gb, w_glu, b_glu, w_pb, w_out]
    return pl.pallas_call(
        functools.partial(_mixb_kernel, pos_blocks=pos_blocks if pos_tabs is not None else 0),
        grid=(ntok // tm,),
        in_specs=in_specs,
        out_specs=pl.BlockSpec((tm, d), lambda i: (i, 0)),
        out_shape=jax.ShapeDtypeStruct((ntok, d), F32),
        scratch_shapes=[pltpu.VMEM((db // LANES, tm, LANES), F32)],
        compiler_params=_params(("arbitrary",)),
        name="mixb",
    )(*args)


def _mlp_kernel(x_ref, mod_ref, g_ref, w1_ref, w2_ref, gf_ref, o_ref, h_ref, acc_ref):
    j = pl.program_id(1)
    nj = pl.num_programs(1)

    def ffn(h):
        hid = jnp.dot(h, w1_ref[...], preferred_element_type=F32)
        hid = jnp.square(jnp.maximum(hid, 0.0)).astype(BF16)
        return jnp.dot(hid, w2_ref[...], preferred_element_type=F32)

    @pl.when(j == 0)
    def _():
        h = _modnorm(x_ref[...], g_ref[...], mod_ref[3:4, :], mod_ref[4:5, :]).astype(BF16)
        h_ref[...] = h
        acc_ref[...] = ffn(h)

    @pl.when(jnp.logical_and(j > 0, j < nj - 1))
    def _():
        acc_ref[...] += ffn(h_ref[...])

    @pl.when(j == nj - 1)
    def _():
        x2 = x_ref[...] + mod_ref[5:6, :] * (acc_ref[...] + ffn(h_ref[...]))
        o_ref[...] = _rms(x2, gf_ref[...])


def _mlp_call(x1, mod, mod_row, g_norm, w1, w2, g_final, tm, tf):
    ntok, d = x1.shape
    dff = w1.shape[1]
    assert dff // tf >= 2
    return pl.pallas_call(
        _mlp_kernel,
        grid=(ntok // tm, dff // tf),
        in_specs=[pl.BlockSpec((tm, d), lambda i, j: (i, 0)),
                  pl.BlockSpec((None, N_MOD, d), lambda i, j: (mod_row(i), 0, 0)),
                  _resident((1, d)),
                  pl.BlockSpec((d, tf), lambda i, j: (0, j)),
                  pl.BlockSpec((tf, d), lambda i, j: (j, 0)),
                  _resident((1, d))],
        out_specs=pl.BlockSpec((tm, d), lambda i, j: (i, 0)),
        out_shape=jax.ShapeDtypeStruct((ntok, d), F32),
        scratch_shapes=[pltpu.VMEM((tm, d), BF16), pltpu.VMEM((tm, d), F32)],
        compiler_params=_params(("arbitrary", "arbitrary")),
        name="mlp",
    )(x1, mod, g_norm, w1, w2, g_final)


def _pos_tables(n_tokens, d):
    rows = n_tokens // GRID_W
    quarter = d // 4
    omega = 1.0 / (POS_BASE ** (jnp.arange(quarter, dtype=F32) / quarter))
    r = jnp.arange(rows, dtype=F32)[:, None] * omega
    col = jnp.arange(GRID_W, dtype=F32)[:, None] * omega
    e_r = jnp.concatenate([jnp.sin(r), jnp.cos(r)], axis=-1)
    e_c = jnp.concatenate([jnp.sin(col), jnp.cos(col)], axis=-1)
    return e_r, e_c


def _cmul(ar, ai, br, bi):
    return ar * br - ai * bi, ar * bi + ai * br


def _ssm_prepare(a_re, a_im, log_dt, b_re, b_im, c_re, c_im):
    ndir, g, p = a_re.shape
    ci = b_re.shape[-1]
    kt = T1 * ci
    hi = lax.Precision.HIGHEST
    dt = jnp.exp(log_dt)[..., None]
    mag = jnp.exp(a_re * dt)
    ab_re, ab_im = mag * jnp.cos(a_im * dt), mag * jnp.sin(a_im * dt)
    den = a_re * a_re + a_im * a_im
    f_re, f_im = _cmul(ab_re - 1.0, ab_im, a_re / den, -a_im / den)
    bb_re, bb_im = _cmul(f_re[..., None], f_im[..., None], b_re, b_im)

    pr, pi = [jnp.ones_like(ab_re)], [jnp.zeros_like(ab_re)]
    for _ in range(T1):
        nr, ni = _cmul(pr[-1], pi[-1], ab_re, ab_im)
        pr.append(nr)
        pi.append(ni)

    def by_lane(seq, axis):
        return jnp.stack([jnp.stack([seq[T1 - 1 - t][0] for t in range(T1)], axis=axis),
                          jnp.stack([seq[t][1] for t in range(T1)], axis=axis)])
    lane_t = (jnp.arange(kt) // ci)[None, :] == jnp.arange(T1)[:, None]
    lane_c = (jnp.arange(kt) % ci)[None, :] == jnp.arange(ci)[:, None]
    e_t, e_c = lane_t.astype(F32), lane_c.astype(F32)
    pw_l = jnp.einsum('xdgpt,tl->xdgpl', jnp.stack([by_lane(pr, -1), by_lane(pi, -1)]), e_t, precision=hi)
    bb_l = jnp.einsum('xdgpc,cl->xdgpl', jnp.stack([bb_re, bb_im]), e_c, precision=hi)
    ba_re, ba_im = _cmul(pw_l[0], pw_l[1], bb_l[0], bb_l[1])
    wend_t = jnp.concatenate([ba_re[0], ba_im[0], ba_re[1], ba_im[1]], axis=1)

    kk = (jnp.einsum('dgcp,dgpl->dgcl', c_re, ba_re, precision=hi)
          - jnp.einsum('dgcp,dgpl->dgcl', c_im, ba_im, precision=hi))
    kf, kb = kk[0], kk[1]
    mid = (T1 - 1) * ci
    r = jnp.concatenate([kf[..., :mid], kf[..., mid:] + kb[..., :ci], kb[..., ci:]], axis=-1)
    toep_t = jnp.stack([r[..., (T1 - 1 - t) * ci:(2 * T1 - 1 - t) * ci] for t in range(T1)], axis=1)
    toep_t = toep_t.reshape(g, kt, kt)
    w1t = jnp.concatenate([toep_t, wend_t], axis=1).astype(BF16)

    def by_row(seq):
        w = jnp.stack([jnp.stack([seq[t + 1][0] for t in range(T1)], axis=1),
                       jnp.stack([seq[T1 - t][1] for t in range(T1)], axis=1)])
        return jnp.concatenate([w, w], axis=-1)
    c2_re, c2_im = jnp.concatenate([c_re, c_re], axis=-1), jnp.concatenate([c_im, c_im], axis=-1)
    ca_re, ca_im = _cmul(c2_re[:, :, None], c2_im[:, :, None], by_row(pr)[:, :, :, None], by_row(pi)[:, :, :, None])
    ca_re, ca_im = ca_re.reshape(ndir, g, kt, 2 * p), ca_im.reshape(ndir, g, kt, 2 * p)
    wout2 = jnp.concatenate([ca_re[0], -ca_im[0], ca_re[1], -ca_im[1]], axis=-1).astype(BF16)

    pw = jnp.stack([pr[T1][0], pi[T1][0], pr[T1][1], pi[T1][1]], axis=0)
    pw = jnp.transpose(pw.reshape(4, g // 2, 2 * p), (1, 0, 2))
    return w1t, wout2, pw


def _trunk(x, pos_tabs, mod, mod_row, h0p, lw, nsb, emit_state, tm_a, tm_b):
    bsz, n, d = x.shape
    x2 = x.reshape(bsz * n, d)
    pa, xb = _mixa_call(x2, pos_tabs, mod, mod_row(tm_a), n // tm_a, lw["g_norm_mix"], lw["w_uvgx"],
                        lw["g_sgu"], lw["w_s"], lw["b_s"], lw["w_pa"], tm_a)
    res = _s5_call(xb, lw["w1t"], lw["wout2"], lw["pw"], lw["d_skip"], h0p, n, nsb, emit_state)
    x1 = _mixb_call(x2, pos_tabs, mod, mod_row(tm_b), n // tm_b, lw["g_norm_mix"], res[0], pa, lw["w_gb"],
                    lw["w_glu"], lw["b_glu"], lw["w_pb"], lw["w_out"], tm_b)
    return x1, (res[1] if emit_state else None)


def kernel(x_prompt, x_sample, state_ssm_re, state_ssm_im, c, c_ctx, w_ada, b_ada, g_norm_mix, w_in,
           g_sgu, w_spatial, b_spatial, ssm_a_re, ssm_a_im, ssm_log_dt, ssm_b_re, ssm_b_im, ssm_c_re,
           ssm_c_im, ssm_d, w_glu, b_glu, w_proj_a, w_proj_b, w_out, g_norm_mlp, w_mlp_in, w_mlp_out,
           g_final):
    bp, sp, d = x_prompt.shape
    bs, ss, _ = x_sample.shape
    depth = w_in.shape[0]
    assert depth == 1, "positional embedding and final norm are fused assuming a single trunk layer"
    da = w_proj_a.shape[1]
    db = w_proj_b.shape[1]
    g, p = ssm_a_re.shape[2], ssm_a_re.shape[3]
    tm_a, tm_b, tm_mlp = 256, 256, 512
    tf = min(1024, w_mlp_in.shape[2] // 2)

    rows = -(-(bs + 1) // 8) * 8
    cc = jnp.zeros((rows, d), F32).at[:bs].set(c).at[bs].set(c_ctx)
    pos_tabs = _pos_tables(ss, d)
    l = 0
    mod = _mod_call(cc, w_ada[l], b_ada[l]).reshape(rows, N_MOD, d)
    w1t, wout2, pw = _ssm_prepare(ssm_a_re[l], ssm_a_im[l], ssm_log_dt[l], ssm_b_re[l],
                                  ssm_b_im[l], ssm_c_re[l], ssm_c_im[l])
    wi = w_in[l]
    lw = dict(
        g_norm_mix=g_norm_mix[l].reshape(1, d),
        w_uvgx=jnp.concatenate([wi[:, :2 * da], wi[:, 2 * da + db:2 * da + db + d],
                                wi[:, 2 * da:2 * da + db]], axis=1).astype(BF16),
        w_gb=wi[:, 2 * da + db + d:].astype(BF16),
        g_sgu=g_sgu[l].reshape(1, da),
        w_s=w_spatial[l].astype(BF16),
        b_s=b_spatial[l][:, :, None],
        w_pa=w_proj_a[l].astype(BF16),
        w_glu=w_glu[l].astype(BF16),
        b_glu=b_glu[l].reshape(1, db),
        w_pb=w_proj_b[l].astype(BF16),
        w_out=w_out[l].astype(BF16),
        w1t=w1t, wout2=wout2, pw=pw, d_skip=ssm_d[l].reshape(1, db))

    def pairs(st):
        return jnp.transpose(st.reshape(bs, 2, g // 2, 2 * p), (1, 2, 0, 3))
    sre, sim = pairs(state_ssm_re[:, l]), pairs(state_ssm_im[:, l])
    h0p = jnp.stack([sre[0], sim[0], sre[1], sim[1]], axis=1)[None]

    ctx_row = lambda tm: (lambda i: bs)
    seq_row = lambda tm: (lambda i: i // (ss // tm))
    xp1, e = _trunk(x_prompt, None, mod, ctx_row, None, lw, bp, True, tm_a, tm_b)
    xs1, _ = _trunk(x_sample, pos_tabs, mod, seq_row, h0p, lw, bs, False, tm_a, tm_b)

    gf = g_final.reshape(1, d)
    w1, w2 = w_mlp_in[l].astype(BF16), w_mlp_out[l].astype(BF16)
    gm = g_norm_mlp[l].reshape(1, d)
    xp = _mlp_call(xp1, mod, ctx_row(tm_mlp), gm, w1, w2, gf, tm_mlp, tf).reshape(bp, sp, d)
    xs = _mlp_call(xs1, mod, seq_row(tm_mlp), gm, w1, w2, gf, tm_mlp, tf).reshape(bs, ss, d)

    ev = e.reshape(g // 2, 2, 2, bp, 2, p)
    fin = jnp.transpose(ev, (2, 3, 1, 0, 4, 5)).reshape(2, bp, 2, g, p)
    return xp, xs, fin[0][:, None], fin[1][:, None]
```

```python
import functools

import jax
import jax.numpy as jnp
from jax import lax
from jax.experimental import pallas as pl
from jax.experimental.pallas import tpu as pltpu

F32 = jnp.float32
BF16 = jnp.bfloat16

EPS = 1e-6
POS_BASE = 10000.0
GRID_W = 64
N_MOD = 6
T1 = 16
LANES = 128
SUBLANES = 8
VMEM_LIMIT = 56 * 1024 * 1024


def _gelu(x):
    return 0.5 * x * (1.0 + jnp.tanh(0.7978845608028654 * (x + 0.044715 * (x * x * x))))


def _sigmoid(x):
    return 1.0 / (1.0 + jnp.exp(-x))


def _rms(x, g):
    return x * lax.rsqrt(jnp.mean(x * x, axis=-1, keepdims=True) + EPS) * g


def _modnorm(x, g, shift, scale):
    return _rms(x, g) * (1.0 + scale) + shift


def _resident(shape):
    nd = len(shape)
    return pl.BlockSpec(shape, lambda *_: (0,) * nd, pipeline_mode=pl.Buffered(1))


def _params(sem):
    return pltpu.CompilerParams(dimension_semantics=sem, vmem_limit_bytes=VMEM_LIMIT)


def _add_pos(x, er_ref, ec_ref, tile_in_seq):
    tm, d = x.shape
    half = d // 2
    r0 = tile_in_seq * (tm // GRID_W)
    ec = ec_ref[...]
    parts = []
    for k in range(tm // GRID_W):
        er = jnp.broadcast_to(er_ref[pl.ds(r0 + k, 1), :], (GRID_W, half))
        parts.append(jnp.concatenate([er, ec], axis=1))
    return x + jnp.concatenate(parts, axis=0)


def _mod_kernel(c_ref, w_ref, b_ref, o_ref):
    cc = c_ref[...]
    s = cc * _sigmoid(cc)
    o_ref[...] = jnp.dot(s, w_ref[...], preferred_element_type=F32) + b_ref[...]


def _mod_call(cc, w_ada, b_ada):
    rows, d = cc.shape
    n = w_ada.shape[1]
    tn = n // (2 * N_MOD)
    assert n % tn == 0 and tn % LANES == 0
    return pl.pallas_call(
        _mod_kernel,
        grid=(n // tn,),
        in_specs=[pl.BlockSpec((rows, d), lambda j: (0, 0)),
                  pl.BlockSpec((d, tn), lambda j: (0, j)),
                  pl.BlockSpec((1, tn), lambda j: (0, j))],
        out_specs=pl.BlockSpec((rows, tn), lambda j: (0, j)),
        out_shape=jax.ShapeDtypeStruct((rows, n), F32),
        compiler_params=_params(("arbitrary",)),
        name="mod",
    )(cc, w_ada, b_ada.reshape(1, n))


def _mixa_kernel(*refs, pos_blocks, chunk, da):
    if pos_blocks:
        (x_ref, er_ref, ec_ref, mod_ref, g_ref, w_ref, gs_ref, ws_ref, bs_ref, wpa_ref,
         o_ref, xb_ref, ya_ref, xs_ref) = refs
        x = _add_pos(x_ref[...], er_ref, ec_ref, pl.program_id(0) % pos_blocks)
    else:
        (x_ref, mod_ref, g_ref, w_ref, gs_ref, ws_ref, bs_ref, wpa_ref,
         o_ref, xb_ref, ya_ref, xs_ref) = refs
        x = x_ref[...]
    tm, d = x.shape
    h = _modnorm(x, g_ref[...], mod_ref[0:1, :], mod_ref[1:2, :]).astype(BF16)
    z = jnp.dot(h, w_ref[...], preferred_element_type=F32)
    for cb in range(xs_ref.shape[0]):
        c0 = 2 * da + d + cb * LANES
        xs_ref[cb] = z[:, c0:c0 + LANES]
        for t in range(T1):
            xb_ref[t, :, cb * LANES:(cb + 1) * LANES] = (
                xs_ref[cb, pl.ds(t, tm // T1, stride=T1), :].astype(xb_ref.dtype))
    u = _gelu(z[:, :da])
    v = _rms(_gelu(z[:, da:2 * da]), gs_ref[...]).astype(BF16)
    ng = ws_ref.shape[0]
    cg = da // ng
    for ck in range(tm // chunk):
        r0 = ck * chunk
        for gi in range(ng):
            c0 = gi * cg
            s = jnp.dot(ws_ref[gi], v[r0:r0 + chunk, c0:c0 + cg], preferred_element_type=F32)
            ya_ref[r0:r0 + chunk, c0:c0 + cg] = (u[r0:r0 + chunk, c0:c0 + cg] * (s + bs_ref[gi])).astype(BF16)
    pa = jnp.dot(ya_ref[...], wpa_ref[...], preferred_element_type=F32)
    o_ref[...] = (_sigmoid(z[:, 2 * da:2 * da + d]) * pa).astype(o_ref.dtype)


def _mixa_call(x2, pos_tabs, mod, mod_row, pos_blocks, g_norm, w_uvgx, g_sgu, w_s, b_s, w_pa, tm):
    ntok, d = x2.shape
    da = w_pa.shape[0]
    db = w_uvgx.shape[1] - 2 * da - d
    chunk = w_s.shape[1]
    in_specs = [pl.BlockSpec((tm, d), lambda i: (i, 0))]
    args = [x2]
    if pos_tabs is not None:
        in_specs += [_resident(pos_tabs[0].shape), _resident(pos_tabs[1].shape)]
        args += list(pos_tabs)
    in_specs += [pl.BlockSpec((None, N_MOD, d), lambda i: (mod_row(i), 0, 0)),
                 _resident((1, d)), _resident(w_uvgx.shape), _resident((1, da)),
                 _resident(w_s.shape), _resident(b_s.shape), _resident(w_pa.shape)]
    args += [mod, g_norm, w_uvgx, g_sgu, w_s, b_s, w_pa]
    return pl.pallas_call(
        functools.partial(_mixa_kernel, pos_blocks=pos_blocks if pos_tabs is not None else 0,
                          chunk=chunk, da=da),
        grid=(ntok // tm,),
        in_specs=in_specs,
        out_specs=[pl.BlockSpec((tm, d), lambda i: (i, 0)),
                   pl.BlockSpec((T1, tm // T1, db), lambda i: (0, i, 0))],
        out_shape=[jax.ShapeDtypeStruct((ntok, d), BF16),
                   jax.ShapeDtypeStruct((T1, ntok // T1, db), BF16)],
        scratch_shapes=[pltpu.VMEM((tm, da), BF16), pltpu.VMEM((db // LANES, tm, LANES), F32)],
        compiler_params=_params(("arbitrary",)),
        name="mixa",
    )(*args)


def _s5_kernel(*refs, nsb, n_chunks, has_h0, emit_state, p, ci):
    it = iter(refs)
    x_ref, w1_ref, wout_ref, pw_ref, d_ref = next(it), next(it), next(it), next(it), next(it)
    h0_ref = next(it) if has_h0 else None
    y_ref = next(it)
    e_ref = next(it) if emit_state else None
    ut_ref, s_ref, hin_ref = next(it), next(it), next(it)

    ngl, kt, cols = ut_ref.shape
    npair = ngl // 2
    pitch = s_ref.shape[2] // nsb

    for t in range(T1):
        st = x_ref[t].astype(F32).T
        for gl in range(ngl):
            ut_ref[gl, t * ci:(t + 1) * ci, :] = st[gl * ci:(gl + 1) * ci, :]

    def local(pr, carry):
        r = []
        for hlf in range(2):
            gl = 2 * pr + hlf
            r1 = jnp.dot(w1_ref[gl], ut_ref[gl].astype(BF16), preferred_element_type=F32)
            ut_ref[gl] = r1[:kt]
            r.append(r1)
        for q in range(4):
            lo = kt + q * p
            sq = jnp.concatenate([r[0][lo:lo + p], r[1][lo:lo + p]], axis=0).T
            for b in range(nsb):
                s_ref[pr, q, b * pitch:b * pitch + n_chunks, :] = sq[b * n_chunks:(b + 1) * n_chunks]
        return carry

    lax.fori_loop(0, npair, local, 0, unroll=True)

    def rows(n):
        return pl.ds(n, nsb, stride=pitch) if nsb > 1 else pl.ds(n, 1)

    def step(n, hs):
        rf, rb = rows(n), rows(n_chunks - 1 - n)
        out = []
        for pr in range(npair):
            fr, fi, br, bi = hs[4 * pr:4 * pr + 4]
            afr, afi = pw_ref[pr, 0:1, :], pw_ref[pr, 1:2, :]
            abr, abi = pw_ref[pr, 2:3, :], pw_ref[pr, 3:4, :]
            sfr, sfi = s_ref[pr, 0, rf, :], s_ref[pr, 1, rf, :]
            sbr, sbi = s_ref[pr, 2, rb, :], s_ref[pr, 3, rb, :]
            hin_ref[pr, 0, rf, :] = fr
            hin_ref[pr, 1, rf, :] = fi
            hin_ref[pr, 2, rb, :] = br
            hin_ref[pr, 3, rb, :] = bi
            out += [afr * fr - afi * fi + sfr, afr * fi + afi * fr + sfi,
                    abr * br - abi * bi + sbr, abr * bi + abi * br + sbi]
        return tuple(out)

    if has_h0:
        init = tuple(h0_ref[pr, q] for pr in range(npair) for q in range(4))
    else:
        init = tuple(jnp.zeros((nsb, LANES), F32) for _ in range(4 * npair))
    fin = lax.fori_loop(0, n_chunks, step, init, unroll=4)
    if emit_state:
        for pr in range(npair):
            for q in range(4):
                e_ref[pr, q] = fin[4 * pr + q]

    first = lax.broadcasted_iota(jnp.int32, (cols, 8 * p), 1) % LANES < p

    def inter(pr, carry):
        hin = jnp.concatenate(
            [jnp.concatenate([hin_ref[pr, q, b * pitch:b * pitch + n_chunks, :] for b in range(nsb)], axis=0)
             for q in range(4)], axis=1)
        for hlf, hg in enumerate((jnp.where(first, hin, 0.0), jnp.where(first, 0.0, hin))):
            gl = 2 * pr + hlf
            ut_ref[gl] += lax.dot_general(wout_ref[gl], hg.astype(BF16), (((1,), (1,)), ((), ())),
                                          preferred_element_type=F32)
        return carry

    lax.fori_loop(0, npair, inter, 0, unroll=True)

    for t in range(T1):
        yt = jnp.concatenate([ut_ref[gl, t * ci:(t + 1) * ci, :] for gl in range(ngl)], axis=0)
        y_ref[t] = _gelu(yt.T + d_ref[...] * x_ref[t].astype(F32)).astype(y_ref.dtype)


def _s5_call(xb, w1t, wout2, pw, d_skip, h0p, n, nsb, emit_state):
    _, ncol, db = xb.shape
    g, kt = w1t.shape[0], w1t.shape[2]
    p = (w1t.shape[1] - kt) // 4
    ci = db // g
    ngl = LANES // ci
    npair = ngl // 2
    assert 2 * p == LANES and ngl % 2 == 0
    n_chunks = n // T1
    cols = nsb * n_chunks
    has_h0 = h0p is not None
    in_specs = [pl.BlockSpec((T1, cols, LANES), lambda i, j: (0, j, i)),
                pl.BlockSpec((ngl,) + w1t.shape[1:], lambda i, j: (i, 0, 0)),
                pl.BlockSpec((ngl,) + wout2.shape[1:], lambda i, j: (i, 0, 0)),
                pl.BlockSpec((npair,) + pw.shape[1:], lambda i, j: (i, 0, 0)),
                pl.BlockSpec((1, LANES), lambda i, j: (0, i))]
    args = [xb, w1t, wout2, pw, d_skip]
    if has_h0:
        in_specs.append(pl.BlockSpec((None, npair, 4, nsb, LANES), lambda i, j: (j, i, 0, 0, 0)))
        args.append(h0p)
    out_specs = [pl.BlockSpec((T1, cols, LANES), lambda i, j: (0, j, i))]
    out_shape = [jax.ShapeDtypeStruct((T1, ncol, db), xb.dtype)]
    if emit_state:
        out_specs.append(pl.BlockSpec((None, npair, 4, nsb, LANES), lambda i, j: (j, i, 0, 0, 0)))
        out_shape.append(jax.ShapeDtypeStruct((ncol // cols, g // 2, 4, nsb, LANES), F32))
    return pl.pallas_call(
        functools.partial(_s5_kernel, nsb=nsb, n_chunks=n_chunks, has_h0=has_h0, emit_state=emit_state,
                          p=p, ci=ci),
        grid=(g // ngl, ncol // cols),
        in_specs=in_specs,
        out_specs=out_specs,
        out_shape=out_shape,
        scratch_shapes=[pltpu.VMEM((ngl, kt, cols), F32)]
        + [pltpu.VMEM((npair, 4, nsb * (n_chunks + SUBLANES), LANES), F32)] * 2,
        compiler_params=_params(("arbitrary", "arbitrary")),
        name="s5",
    )(*args)


def _mixb_kernel(*refs, pos_blocks):
    if pos_blocks:
        (x_ref, er_ref, ec_ref, mod_ref, g_ref, y_ref, pa_ref, wgb_ref, wglu_ref, bglu_ref,
         wpb_ref, wout_ref, o_ref, ys_ref) = refs
        x = _add_pos(x_ref[...], er_ref, ec_ref, pl.program_id(0) % pos_blocks)
    else:
        (x_ref, mod_ref, g_ref, y_ref, pa_ref, wgb_ref, wglu_ref, bglu_ref,
         wpb_ref, wout_ref, o_ref, ys_ref) = refs
        x = x_ref[...]
    tm = x.shape[0]
    h = _modnorm(x, g_ref[...], mod_ref[0:1, :], mod_ref[1:2, :]).astype(BF16)
    for cb in range(ys_ref.shape[0]):
        for t in range(T1):
            ys_ref[cb, pl.ds(t, tm // T1, stride=T1), :] = y_ref[t, :, cb * LANES:(cb + 1) * LANES].astype(F32)
    y = jnp.concatenate([ys_ref[cb] for cb in range(ys_ref.shape[0])], axis=1)
    gate = _sigmoid(jnp.dot(y.astype(BF16), wglu_ref[...], preferred_element_type=F32) + bglu_ref[...])
    yb = (y * gate).astype(BF16)
    pb = jnp.dot(yb, wpb_ref[...], preferred_element_type=F32)
    pb = _sigmoid(jnp.dot(h, wgb_ref[...], preferred_element_type=F32)) * pb
    m = (pa_ref[...].astype(F32) + pb).astype(BF16)
    o_ref[...] = x + mod_ref[2:3, :] * jnp.dot(m, wout_ref[...], preferred_element_type=F32)


def _mixb_call(x2, pos_tabs, mod, mod_row, pos_blocks, g_norm, y, pa, w_gb, w_glu, b_glu, w_pb, w_out, tm):
    ntok, d = x2.shape
    db = y.shape[2]
    in_specs = [pl.BlockSpec((tm, d), lambda i: (i, 0))]
    args = [x2]
    if pos_tabs is not None:
        in_specs += [_resident(pos_tabs[0].shape), _resident(pos_tabs[1].shape)]
        args += list(pos_tabs)
    in_specs += [pl.BlockSpec((None, N_MOD, d), lambda i: (mod_row(i), 0, 0)),
                 _resident((1, d)),
                 pl.BlockSpec((T1, tm // T1, db), lambda i: (0, i, 0)),
                 pl.BlockSpec((tm, d), lambda i: (i, 0)),
                 _resident(w_gb.shape), _resident(w_glu.shape), _resident((1, db)),
                 _resident(w_pb.shape), _resident(w_out.shape)]
    args += [mod, g_norm, y, pa, w_gb, w_glu, b_glu, w_pb, w_out]
    return pl.pallas_call(
        functools.partial(_mixb_kernel, pos_blocks=pos_blocks if pos_tabs is not None else 0),
        grid=(ntok // tm,),
        in_specs=in_specs,
        out_specs=pl.BlockSpec((tm, d), lambda i: (i, 0)),
        out_shape=jax.ShapeDtypeStruct((ntok, d), F32),
        scratch_shapes=[pltpu.VMEM((db // LANES, tm, LANES), F32)],
        compiler_params=_params(("arbitrary",)),
        name="mixb",
    )(*args)


def _mlp_kernel(x_ref, mod_ref, g_ref, w1_ref, w2_ref, gf_ref, o_ref, h_ref, acc_ref):
    j = pl.program_id(1)
    nj = pl.num_programs(1)

    def ffn(h):
        hid = jnp.dot(h, w1_ref[...], preferred_element_type=F32)
        hid = jnp.square(jnp.maximum(hid, 0.0)).astype(BF16)
        return jnp.dot(hid, w2_ref[...], preferred_element_type=F32)

    @pl.when(j == 0)
    def _():
        h = _modnorm(x_ref[...], g_ref[...], mod_ref[3:4, :], mod_ref[4:5, :]).astype(BF16)
        h_ref[...] = h
        acc_ref[...] = ffn(h)

    @pl.when(jnp.logical_and(j > 0, j < nj - 1))
    def _():
        acc_ref[...] += ffn(h_ref[...])

    @pl.when(j == nj - 1)
    def _():
        x2 = x_ref[...] + mod_ref[5:6, :] * (acc_ref[...] + ffn(h_ref[...]))
        o_ref[...] = _rms(x2, gf_ref[...])


def _mlp_call(x1, mod, mod_row, g_norm, w1, w2, g_final, tm, tf):
    ntok, d = x1.shape
    dff = w1.shape[1]
    assert dff // tf >= 2
    return pl.pallas_call(
        _mlp_kernel,
        grid=(ntok // tm, dff // tf),
        in_specs=[pl.BlockSpec((tm, d), lambda i, j: (i, 0)),
                  pl.BlockSpec((None, N_MOD, d), lambda i, j: (mod_row(i), 0, 0)),
                  _resident((1, d)),
                  pl.BlockSpec((d, tf), lambda i, j: (0, j)),
                  pl.BlockSpec((tf, d), lambda i, j: (j, 0)),
                  _resident((1, d))],
        out_specs=pl.BlockSpec((tm, d), lambda i, j: (i, 0)),
        out_shape=jax.ShapeDtypeStruct((ntok, d), F32),
        scratch_shapes=[pltpu.VMEM((tm, d), BF16), pltpu.VMEM((tm, d), F32)],
        compiler_params=_params(("arbitrary", "arbitrary")),
        name="mlp",
    )(x1, mod, g_norm, w1, w2, g_final)


def _pos_tables(n_tokens, d):
    rows = n_tokens // GRID_W
    quarter = d // 4
    omega = 1.0 / (POS_BASE ** (jnp.arange(quarter, dtype=F32) / quarter))
    r = jnp.arange(rows, dtype=F32)[:, None] * omega
    col = jnp.arange(GRID_W, dtype=F32)[:, None] * omega
    e_r = jnp.concatenate([jnp.sin(r), jnp.cos(r)], axis=-1)
    e_c = jnp.concatenate([jnp.sin(col), jnp.cos(col)], axis=-1)
    return e_r, e_c


def _cmul(ar, ai, br, bi):
    return ar * br - ai * bi, ar * bi + ai * br


def _ssm_prepare(a_re, a_im, log_dt, b_re, b_im, c_re, c_im):
    ndir, g, p = a_re.shape
    ci = b_re.shape[-1]
    kt = T1 * ci
    hi = lax.Precision.HIGHEST
    dt = jnp.exp(log_dt)[..., None]
    mag = jnp.exp(a_re * dt)
    ab_re, ab_im = mag * jnp.cos(a_im * dt), mag * jnp.sin(a_im * dt)
    den = a_re * a_re + a_im * a_im
    f_re, f_im = _cmul(ab_re - 1.0, ab_im, a_re / den, -a_im / den)
    bb_re, bb_im = _cmul(f_re[..., None], f_im[..., None], b_re, b_im)

    pr, pi = [jnp.ones_like(ab_re)], [jnp.zeros_like(ab_re)]
    for _ in range(T1):
        nr, ni = _cmul(pr[-1], pi[-1], ab_re, ab_im)
        pr.append(nr)
        pi.append(ni)

    def by_lane(seq, axis):
        return jnp.stack([jnp.stack([seq[T1 - 1 - t][0] for t in range(T1)], axis=axis),
                          jnp.stack([seq[t][1] for t in range(T1)], axis=axis)])
    ba_re, ba_im = _cmul(jnp.repeat(by_lane(pr, -1), ci, axis=-1), jnp.repeat(by_lane(pi, -1), ci, axis=-1),
                         jnp.tile(bb_re, (1, 1, 1, T1)), jnp.tile(bb_im, (1, 1, 1, T1)))
    wend_t = jnp.concatenate([ba_re[0], ba_im[0], ba_re[1], ba_im[1]], axis=1)

    kk = (jnp.einsum('dgcp,dgpl->dgcl', c_re, ba_re, precision=hi)
          - jnp.einsum('dgcp,dgpl->dgcl', c_im, ba_im, precision=hi))
    kf, kb = kk[0], kk[1]
    mid = (T1 - 1) * ci
    r = jnp.concatenate([kf[..., :mid], kf[..., mid:] + kb[..., :ci], kb[..., ci:]], axis=-1)
    toep_t = jnp.stack([r[..., (T1 - 1 - t) * ci:(2 * T1 - 1 - t) * ci] for t in range(T1)], axis=1)
    toep_t = toep_t.reshape(g, kt, kt)
    w1t = jnp.concatenate([toep_t, wend_t], axis=1).astype(BF16)

    def by_row(seq):
        w = jnp.stack([jnp.stack([seq[t + 1][0] for t in range(T1)], axis=1),
                       jnp.stack([seq[T1 - t][1] for t in range(T1)], axis=1)])
        return jnp.concatenate([w, w], axis=-1)
    c2_re, c2_im = jnp.concatenate([c_re, c_re], axis=-1), jnp.concatenate([c_im, c_im], axis=-1)
    ca_re, ca_im = _cmul(c2_re[:, :, None], c2_im[:, :, None], by_row(pr)[:, :, :, None], by_row(pi)[:, :, :, None])
    ca_re, ca_im = ca_re.reshape(ndir, g, kt, 2 * p), ca_im.reshape(ndir, g, kt, 2 * p)
    wout2 = jnp.concatenate([ca_re[0], -ca_im[0], ca_re[1], -ca_im[1]], axis=-1).astype(BF16)

    pw = jnp.stack([pr[T1][0], pi[T1][0], pr[T1][1], pi[T1][1]], axis=0)
    pw = jnp.transpose(pw.reshape(4, g // 2, 2 * p), (1, 0, 2))
    return w1t, wout2, pw


def _trunk(x, pos_tabs, mod, mod_row, h0p, lw, nsb, emit_state, tm_a, tm_b):
    bsz, n, d = x.shape
    x2 = x.reshape(bsz * n, d)
    pa, xb = _mixa_call(x2, pos_tabs, mod, mod_row(tm_a), n // tm_a, lw["g_norm_mix"], lw["w_uvgx"],
                        lw["g_sgu"], lw["w_s"], lw["b_s"], lw["w_pa"], tm_a)
    res = _s5_call(xb, lw["w1t"], lw["wout2"], lw["pw"], lw["d_skip"], h0p, n, nsb, emit_state)
    x1 = _mixb_call(x2, pos_tabs, mod, mod_row(tm_b), n // tm_b, lw["g_norm_mix"], res[0], pa, lw["w_gb"],
                    lw["w_glu"], lw["b_glu"], lw["w_pb"], lw["w_out"], tm_b)
    return x1, (res[1] if emit_state else None)


def kernel(x_prompt, x_sample, state_ssm_re, state_ssm_im, c, c_ctx, w_ada, b_ada, g_norm_mix, w_in,
           g_sgu, w_spatial, b_spatial, ssm_a_re, ssm_a_im, ssm_log_dt, ssm_b_re, ssm_b_im, ssm_c_re,
           ssm_c_im, ssm_d, w_glu, b_glu, w_proj_a, w_proj_b, w_out, g_norm_mlp, w_mlp_in, w_mlp_out,
           g_final):
    bp, sp, d = x_prompt.shape
    bs, ss, _ = x_sample.shape
    depth = w_in.shape[0]
    assert depth == 1, "positional embedding and final norm are fused assuming a single trunk layer"
    da = w_proj_a.shape[1]
    db = w_proj_b.shape[1]
    g, p = ssm_a_re.shape[2], ssm_a_re.shape[3]
    tm_a, tm_b, tm_mlp = 256, 256, 512
    tf = min(1024, w_mlp_in.shape[2] // 2)

    rows = -(-(bs + 1) // 8) * 8
    cc = jnp.zeros((rows, d), F32).at[:bs].set(c).at[bs].set(c_ctx)
    pos_tabs = _pos_tables(ss, d)
    l = 0
    mod = _mod_call(cc, w_ada[l], b_ada[l]).reshape(rows, N_MOD, d)
    w1t, wout2, pw = _ssm_prepare(ssm_a_re[l], ssm_a_im[l], ssm_log_dt[l], ssm_b_re[l],
                                  ssm_b_im[l], ssm_c_re[l], ssm_c_im[l])
    wi = w_in[l]
    lw = dict(
        g_norm_mix=g_norm_mix[l].reshape(1, d),
        w_uvgx=jnp.concatenate([wi[:, :2 * da], wi[:, 2 * da + db:2 * da + db + d],
                                wi[:, 2 * da:2 * da + db]], axis=1).astype(BF16),
        w_gb=wi[:, 2 * da + db + d:].astype(BF16),
        g_sgu=g_sgu[l].reshape(1, da),
        w_s=w_spatial[l].astype(BF16),
        b_s=b_spatial[l][:, :, None],
        w_pa=w_proj_a[l].astype(BF16),
        w_glu=w_glu[l].astype(BF16),
        b_glu=b_glu[l].reshape(1, db),
        w_pb=w_proj_b[l].astype(BF16),
        w_out=w_out[l].astype(BF16),
        w1t=w1t, wout2=wout2, pw=pw, d_skip=ssm_d[l].reshape(1, db))

    def pairs(st):
        return jnp.transpose(st.reshape(bs, 2, g // 2, 2 * p), (1, 2, 0, 3))
    sre, sim = pairs(state_ssm_re[:, l]), pairs(state_ssm_im[:, l])
    h0p = jnp.stack([sre[0], sim[0], sre[1], sim[1]], axis=1)[None]

    ctx_row = lambda tm: (lambda i: bs)
    seq_row = lambda tm: (lambda i: i // (ss // tm))
    xp1, e = _trunk(x_prompt, None, mod, ctx_row, None, lw, bp, True, tm_a, tm_b)
    xs1, _ = _trunk(x_sample, pos_tabs, mod, seq_row, h0p, lw, bs, False, tm_a, tm_b)

    gf = g_final.reshape(1, d)
    w1, w2 = w_mlp_in[l].astype(BF16), w_mlp_out[l].astype(BF16)
    gm = g_norm_mlp[l].reshape(1, d)
    xp = _mlp_call(xp1, mod, ctx_row(tm_mlp), gm, w1, w2, gf, tm_mlp, tf).reshape(bp, sp, d)
    xs = _mlp_call(xs1, mod, seq_row(tm_mlp), gm, w1, w2, gf, tm_mlp, tf).reshape(bs, ss, d)

    ev = e.reshape(g // 2, 2, 2, bp, 2, p)
    fin = jnp.transpose(ev, (2, 3, 1, 0, 4, 5)).reshape(2, bp, 2, g, p)
    return xp, xs, fin[0][:, None], fin[1][:, None]
```

```python
import functools

import jax
import jax.numpy as jnp
from jax import lax
from jax.experimental import pallas as pl
from jax.experimental.pallas import tpu as pltpu

F32 = jnp.float32
BF16 = jnp.bfloat16

EPS = 1e-6
POS_BASE = 10000.0
GRID_W = 64
N_MOD = 6
T1 = 16
LANES = 128
SUBLANES = 8
VMEM_LIMIT = 56 * 1024 * 1024


def _gelu(x):
    return 0.5 * x * (1.0 + jnp.tanh(0.7978845608028654 * (x + 0.044715 * (x * x * x))))


def _sigmoid(x):
    return 1.0 / (1.0 + jnp.exp(-x))


def _rms(x, g):
    return x * lax.rsqrt(jnp.mean(x * x, axis=-1, keepdims=True) + EPS) * g


def _modnorm(x, g, shift, scale):
    return _rms(x, g) * (1.0 + scale) + shift


def _resident(shape):
    nd = len(shape)
    return pl.BlockSpec(shape, lambda *_: (0,) * nd, pipeline_mode=pl.Buffered(1))


def _params(sem):
    return pltpu.CompilerParams(dimension_semantics=sem, vmem_limit_bytes=VMEM_LIMIT)


def _add_pos(x, er_ref, ec_ref, tile_in_seq):
    tm, d = x.shape
    half = d // 2
    r0 = tile_in_seq * (tm // GRID_W)
    ec = ec_ref[...]
    parts = []
    for k in range(tm // GRID_W):
        er = jnp.broadcast_to(er_ref[pl.ds(r0 + k, 1), :], (GRID_W, half))
        parts.append(jnp.concatenate([er, ec], axis=1))
    return x + jnp.concatenate(parts, axis=0)


def _mod_kernel(c_ref, w_ref, b_ref, o_ref):
    cc = c_ref[...]
    s = cc * _sigmoid(cc)
    o_ref[...] = jnp.dot(s, w_ref[...], preferred_element_type=F32) + b_ref[...]


def _mod_call(cc, w_ada, b_ada):
    rows, d = cc.shape
    n = w_ada.shape[1]
    tn = n // (2 * N_MOD)
    assert n % tn == 0 and tn % LANES == 0
    return pl.pallas_call(
        _mod_kernel,
        grid=(n // tn,),
        in_specs=[pl.BlockSpec((rows, d), lambda j: (0, 0)),
                  pl.BlockSpec((d, tn), lambda j: (0, j)),
                  pl.BlockSpec((1, tn), lambda j: (0, j))],
        out_specs=pl.BlockSpec((rows, tn), lambda j: (0, j)),
        out_shape=jax.ShapeDtypeStruct((rows, n), F32),
        compiler_params=_params(("arbitrary",)),
        name="mod",
    )(cc, w_ada, b_ada.reshape(1, n))


def _mixa_kernel(*refs, pos_blocks, chunk, da):
    if pos_blocks:
        (x_ref, er_ref, ec_ref, mod_ref, g_ref, w_ref, gs_ref, ws_ref, bs_ref, wpa_ref,
         o_ref, xb_ref, ya_ref, xs_ref) = refs
        x = _add_pos(x_ref[...], er_ref, ec_ref, pl.program_id(0) % pos_blocks)
    else:
        (x_ref, mod_ref, g_ref, w_ref, gs_ref, ws_ref, bs_ref, wpa_ref,
         o_ref, xb_ref, ya_ref, xs_ref) = refs
        x = x_ref[...]
    tm, d = x.shape
    h = _modnorm(x, g_ref[...], mod_ref[0:1, :], mod_ref[1:2, :]).astype(BF16)
    z = jnp.dot(h, w_ref[...], preferred_element_type=F32)
    for cb in range(xs_ref.shape[0]):
        c0 = 2 * da + d + cb * LANES
        xs_ref[cb] = z[:, c0:c0 + LANES]
        for t in range(T1):
            xb_ref[t, :, cb * LANES:(cb + 1) * LANES] = (
                xs_ref[cb, pl.ds(t, tm // T1, stride=T1), :].astype(xb_ref.dtype))
    u = _gelu(z[:, :da])
    v = _rms(_gelu(z[:, da:2 * da]), gs_ref[...]).astype(BF16)
    ng = ws_ref.shape[0]
    cg = da // ng
    for ck in range(tm // chunk):
        r0 = ck * chunk
        for gi in range(ng):
            c0 = gi * cg
            s = jnp.dot(ws_ref[gi], v[r0:r0 + chunk, c0:c0 + cg], preferred_element_type=F32)
            ya_ref[r0:r0 + chunk, c0:c0 + cg] = (u[r0:r0 + chunk, c0:c0 + cg] * (s + bs_ref[gi])).astype(BF16)
    pa = jnp.dot(ya_ref[...], wpa_ref[...], preferred_element_type=F32)
    o_ref[...] = (_sigmoid(z[:, 2 * da:2 * da + d]) * pa).astype(o_ref.dtype)


def _mixa_call(x2, pos_tabs, mod, mod_row, pos_blocks, g_norm, w_uvgx, g_sgu, w_s, b_s, w_pa, tm):
    ntok, d = x2.shape
    da = w_pa.shape[0]
    db = w_uvgx.shape[1] - 2 * da - d
    chunk = w_s.shape[1]
    in_specs = [pl.BlockSpec((tm, d), lambda i: (i, 0))]
    args = [x2]
    if pos_tabs is not None:
        in_specs += [_resident(pos_tabs[0].shape), _resident(pos_tabs[1].shape)]
        args += list(pos_tabs)
    in_specs += [pl.BlockSpec((None, N_MOD, d), lambda i: (mod_row(i), 0, 0)),
                 _resident((1, d)), _resident(w_uvgx.shape), _resident((1, da)),
                 _resident(w_s.shape), _resident(b_s.shape), _resident(w_pa.shape)]
    args += [mod, g_norm, w_uvgx, g_sgu, w_s, b_s, w_pa]
    return pl.pallas_call(
        functools.partial(_mixa_kernel, pos_blocks=pos_blocks if pos_tabs is not None else 0,
                          chunk=chunk, da=da),
        grid=(ntok // tm,),
        in_specs=in_specs,
        out_specs=[pl.BlockSpec((tm, d), lambda i: (i, 0)),
                   pl.BlockSpec((T1, tm // T1, db), lambda i: (0, i, 0))],
        out_shape=[jax.ShapeDtypeStruct((ntok, d), BF16),
                   jax.ShapeDtypeStruct((T1, ntok // T1, db), BF16)],
        scratch_shapes=[pltpu.VMEM((tm, da), BF16), pltpu.VMEM((db // LANES, tm, LANES), F32)],
        compiler_params=_params(("arbitrary",)),
        name="mixa",
    )(*args)


def _s5_kernel(*refs, nsb, n_chunks, has_h0, emit_state, p, ci):
    it = iter(refs)
    x_ref, w1_ref, wout_ref, pw_ref, d_ref = next(it), next(it), next(it), next(it), next(it)
    h0_ref = next(it) if has_h0 else None
    y_ref = next(it)
    e_ref = next(it) if emit_state else None
    ut_ref, s_ref, hin_ref = next(it), next(it), next(it)

    ngl, kt, cols = ut_ref.shape
    npair = ngl // 2
    pitch = s_ref.shape[2] // nsb

    for t in range(T1):
        st = x_ref[t].astype(F32).T
        for gl in range(ngl):
            ut_ref[gl, t * ci:(t + 1) * ci, :] = st[gl * ci:(gl + 1) * ci, :]

    def local(pr, carry):
        r = []
        for hlf in range(2):
            gl = 2 * pr + hlf
            r1 = jnp.dot(w1_ref[gl], ut_ref[gl].astype(BF16), preferred_element_type=F32)
            ut_ref[gl] = r1[:kt]
            r.append(r1)
        for q in range(4):
            lo = kt + q * p
            sq = jnp.concatenate([r[0][lo:lo + p], r[1][lo:lo + p]], axis=0).T
            for b in range(nsb):
                s_ref[pr, q, b * pitch:b * pitch + n_chunks, :] = sq[b * n_chunks:(b + 1) * n_chunks]
        return carry

    lax.fori_loop(0, npair, local, 0, unroll=True)

    def rows(n):
        return pl.ds(n, nsb, stride=pitch) if nsb > 1 else pl.ds(n, 1)

    def step(n, hs):
        rf, rb = rows(n), rows(n_chunks - 1 - n)
        out = []
        for pr in range(npair):
            fr, fi, br, bi = hs[4 * pr:4 * pr + 4]
            afr, afi = pw_ref[pr, 0:1, :], pw_ref[pr, 1:2, :]
            abr, abi = pw_ref[pr, 2:3, :], pw_ref[pr, 3:4, :]
            sfr, sfi = s_ref[pr, 0, rf, :], s_ref[pr, 1, rf, :]
            sbr, sbi = s_ref[pr, 2, rb, :], s_ref[pr, 3, rb, :]
            hin_ref[pr, 0, rf, :] = fr
            hin_ref[pr, 1, rf, :] = fi
            hin_ref[pr, 2, rb, :] = br
            hin_ref[pr, 3, rb, :] = bi
            out += [afr * fr - afi * fi + sfr, afr * fi + afi * fr + sfi,
                    abr * br - abi * bi + sbr, abr * bi + abi * br + sbi]
        return tuple(out)

    if has_h0:
        init = tuple(h0_ref[pr, q] for pr in range(npair) for q in range(4))
    else:
        init = tuple(jnp.zeros((nsb, LANES), F32) for _ in range(4 * npair))
    fin = lax.fori_loop(0, n_chunks, step, init, unroll=4)
    if emit_state:
        for pr in range(npair):
            for q in range(4):
                e_ref[pr, q] = fin[4 * pr + q]

    first = lax.broadcasted_iota(jnp.int32, (cols, 8 * p), 1) % LANES < p

    def inter(pr, carry):
        hin = jnp.concatenate(
            [jnp.concatenate([hin_ref[pr, q, b * pitch:b * pitch + n_chunks, :] for b in range(nsb)], axis=0)
             for q in range(4)], axis=1)
        for hlf, hg in enumerate((jnp.where(first, hin, 0.0), jnp.where(first, 0.0, hin))):
            gl = 2 * pr + hlf
            ut_ref[gl] += lax.dot_general(wout_ref[gl], hg.astype(BF16), (((1,), (1,)), ((), ())),
                                          preferred_element_type=F32)
        return carry

    lax.fori_loop(0, npair, inter, 0, unroll=True)

    for t in range(T1):
        yt = jnp.concatenate([ut_ref[gl, t * ci:(t + 1) * ci, :] for gl in range(ngl)], axis=0)
        y_ref[t] = _gelu(yt.T + d_ref[...] * x_ref[t].astype(F32)).astype(y_ref.dtype)


def _s5_call(xb, w1t, wout2, pw, d_skip, h0p, n, nsb, emit_state):
    _, ncol, db = xb.shape
    g, kt = w1t.shape[0], w1t.shape[2]
    p = (w1t.shape[1] - kt) // 4
    ci = db // g
    ngl = LANES // ci
    npair = ngl // 2
    assert 2 * p == LANES and ngl % 2 == 0
    n_chunks = n // T1
    cols = nsb * n_chunks
    has_h0 = h0p is not None
    in_specs = [pl.BlockSpec((T1, cols, LANES), lambda i, j: (0, j, i)),
                pl.BlockSpec((ngl,) + w1t.shape[1:], lambda i, j: (i, 0, 0)),
                pl.BlockSpec((ngl,) + wout2.shape[1:], lambda i, j: (i, 0, 0)),
                pl.BlockSpec((npair,) + pw.shape[1:], lambda i, j: (i, 0, 0)),
                pl.BlockSpec((1, LANES), lambda i, j: (0, i))]
    args = [xb, w1t, wout2, pw, d_skip]
    if has_h0:
        in_specs.append(pl.BlockSpec((None, npair, 4, nsb, LANES), lambda i, j: (j, i, 0, 0, 0)))
        args.append(h0p)
    out_specs = [pl.BlockSpec((T1, cols, LANES), lambda i, j: (0, j, i))]
    out_shape = [jax.ShapeDtypeStruct((T1, ncol, db), xb.dtype)]
    if emit_state:
        out_specs.append(pl.BlockSpec((None, npair, 4, nsb, LANES), lambda i, j: (j, i, 0, 0, 0)))
        out_shape.append(jax.ShapeDtypeStruct((ncol // cols, g // 2, 4, nsb, LANES), F32))
    return pl.pallas_call(
        functools.partial(_s5_kernel, nsb=nsb, n_chunks=n_chunks, has_h0=has_h0, emit_state=emit_state,
                          p=p, ci=ci),
        grid=(g // ngl, ncol // cols),
        in_specs=in_specs,
        out_specs=out_specs,
        out_shape=out_shape,
        scratch_shapes=[pltpu.VMEM((ngl, kt, cols), F32)]
        + [pltpu.VMEM((npair, 4, nsb * (n_chunks + SUBLANES), LANES), F32)] * 2,
        compiler_params=_params(("arbitrary", "arbitrary")),
        name="s5",
    )(*args)


def _mixb_kernel(*refs, pos_blocks):
    if pos_blocks:
        (x_ref, er_ref, ec_ref, mod_ref, g_ref, y_ref, pa_ref, wgb_ref, wglu_ref, bglu_ref,
         wpb_ref, wout_ref, o_ref, ys_ref) = refs
        x = _add_pos(x_ref[...], er_ref, ec_ref, pl.program_id(0) % pos_blocks)
    else:
        (x_ref, mod_ref, g_ref, y_ref, pa_ref, wgb_ref, wglu_ref, bglu_ref,
         wpb_ref, wout_ref, o_ref, ys_ref) = refs
        x = x_ref[...]
    tm = x.shape[0]
    h = _modnorm(x, g_ref[...], mod_ref[0:1, :], mod_ref[1:2, :]).astype(BF16)
    for cb in range(ys_ref.shape[0]):
        for t in range(T1):
            ys_ref[cb, pl.ds(t, tm // T1, stride=T1), :] = y_ref[t, :, cb * LANES:(cb + 1) * LANES].astype(F32)
    y = jnp.concatenate([ys_ref[cb] for cb in range(ys_ref.shape[0])], axis=1)
    gate = _sigmoid(jnp.dot(y.astype(BF16), wglu_ref[...], preferred_element_type=F32) + bglu_ref[...])
    yb = (y * gate).astype(BF16)
    pb = jnp.dot(yb, wpb_ref[...], preferred_element_type=F32)
    pb = _sigmoid(jnp.dot(h, wgb_ref[...], preferred_element_type=F32)) * pb
    m = (pa_ref[...].astype(F32) + pb).astype(BF16)
    o_ref[...] = x + mod_ref[2:3, :] * jnp.dot(m, wout_ref[...], preferred_element_type=F32)


def _mixb_call(x2, pos_tabs, mod, mod_row, pos_blocks, g_norm, y, pa, w_gb, w_glu, b_glu, w_pb, w_out, tm):
    ntok, d = x2.shape
    db = y.shape[2]
    in_specs = [pl.BlockSpec((tm, d), lambda i: (i, 0))]
    args = [x2]
    if pos_tabs is not None:
        in_specs += [_resident(pos_tabs[0].shape), _resident(pos_tabs[1].shape)]
        args += list(pos_tabs)
    in_specs += [pl.BlockSpec((None, N_MOD, d), lambda i: (mod_row(i), 0, 0)),
                 _resident((1, d)),
                 pl.BlockSpec((T1, tm // T1, db), lambda i: (0, i, 0)),
                 pl.BlockSpec((tm, d), lambda i: (i, 0)),
                 _resident(w_gb.shape), _resident(w_glu.shape), _resident((1, db)),
                 _resident(w_pb.shape), _resident(w_out.shape)]
    args += [mod, g_norm, y, pa, w_gb, w_glu, b_glu, w_pb, w_out]
    return pl.pallas_call(
        functools.partial(_mixb_kernel, pos_blocks=pos_blocks if pos_tabs is not None else 0),
        grid=(ntok // tm,),
        in_specs=in_specs,
        out_specs=pl.BlockSpec((tm, d), lambda i: (i, 0)),
        out_shape=jax.ShapeDtypeStruct((ntok, d), F32),
        scratch_shapes=[pltpu.VMEM((db // LANES, tm, LANES), F32)],
        compiler_params=_params(("arbitrary",)),
        name="mixb",
    )(*args)


def _mlp_kernel(x_ref, mod_ref, g_ref, w1_ref, w2_ref, gf_ref, o_ref, h_ref, acc_ref):
    j = pl.program_id(1)
    nj = pl.num_programs(1)

    def ffn(h):
        hid = jnp.dot(h, w1_ref[...], preferred_element_type=F32)
        hid = jnp.square(jnp.maximum(hid, 0.0)).astype(BF16)
        return jnp.dot(hid, w2_ref[...], preferred_element_type=F32)

    @pl.when(j == 0)
    def _():
        h = _modnorm(x_ref[...], g_ref[...], mod_ref[3:4, :], mod_ref[4:5, :]).astype(BF16)
        h_ref[...] = h
        acc_ref[...] = ffn(h)

    @pl.when(jnp.logical_and(j > 0, j < nj - 1))
    def _():
        acc_ref[...] += ffn(h_ref[...])

    @pl.when(j == nj - 1)
    def _():
        x2 = x_ref[...] + mod_ref[5:6, :] * (acc_ref[...] + ffn(h_ref[...]))
        o_ref[...] = _rms(x2, gf_ref[...])


def _mlp_call(x1, mod, mod_row, g_norm, w1, w2, g_final, tm, tf):
    ntok, d = x1.shape
    dff = w1.shape[1]
    assert dff // tf >= 2
    return pl.pallas_call(
        _mlp_kernel,
        grid=(ntok // tm, dff // tf),
        in_specs=[pl.BlockSpec((tm, d), lambda i, j: (i, 0)),
                  pl.BlockSpec((None, N_MOD, d), lambda i, j: (mod_row(i), 0, 0)),
                  _resident((1, d)),
                  pl.BlockSpec((d, tf), lambda i, j: (0, j)),
                  pl.BlockSpec((tf, d), lambda i, j: (j, 0)),
                  _resident((1, d))],
        out_specs=pl.BlockSpec((tm, d), lambda i, j: (i, 0)),
        out_shape=jax.ShapeDtypeStruct((ntok, d), F32),
        scratch_shapes=[pltpu.VMEM((tm, d), BF16), pltpu.VMEM((tm, d), F32)],
        compiler_params=_params(("arbitrary", "arbitrary")),
        name="mlp",
    )(x1, mod, g_norm, w1, w2, g_final)


def _pos_tables(n_tokens, d):
    rows = n_tokens // GRID_W
    quarter = d // 4
    omega = 1.0 / (POS_BASE ** (jnp.arange(quarter, dtype=F32) / quarter))
    r = jnp.arange(rows, dtype=F32)[:, None] * omega
    col = jnp.arange(GRID_W, dtype=F32)[:, None] * omega
    e_r = jnp.concatenate([jnp.sin(r), jnp.cos(r)], axis=-1)
    e_c = jnp.concatenate([jnp.sin(col), jnp.cos(col)], axis=-1)
    return e_r, e_c


def _cmul(ar, ai, br, bi):
    return ar * br - ai * bi, ar * bi + ai * br


def _ssm_prepare(a_re, a_im, log_dt, b_re, b_im, c_re, c_im):
    ndir, g, p = a_re.shape
    ci = b_re.shape[-1]
    kt = T1 * ci
    hi = lax.Precision.HIGHEST
    dt = jnp.exp(log_dt)[..., None]
    lr, ang = a_re * dt, a_im * dt

    def power(j, lr_, ang_):
        m = jnp.exp(j * lr_)
        return m * jnp.cos(j * ang_), m * jnp.sin(j * ang_)

    ab_re, ab_im = power(1.0, lr, ang)
    den = a_re * a_re + a_im * a_im
    f_re, f_im = _cmul(ab_re - 1.0, ab_im, a_re / den, -a_im / den)
    bb_re, bb_im = _cmul(f_re[..., None], f_im[..., None], b_re, b_im)

    tt = jnp.arange(T1, dtype=F32)
    j_in = jnp.stack([T1 - 1 - tt, tt])[:, None, None, :]
    pl_re, pl_im = power(j_in, lr[..., None], ang[..., None])
    ba_re, ba_im = _cmul(jnp.repeat(pl_re, ci, axis=-1), jnp.repeat(pl_im, ci, axis=-1),
                         jnp.tile(bb_re, (1, 1, 1, T1)), jnp.tile(bb_im, (1, 1, 1, T1)))
    wend_t = jnp.concatenate([ba_re[0], ba_im[0], ba_re[1], ba_im[1]], axis=1)

    kk = (jnp.einsum('dgcp,dgpl->dgcl', c_re, ba_re, precision=hi)
          - jnp.einsum('dgcp,dgpl->dgcl', c_im, ba_im, precision=hi))
    kf, kb = kk[0], kk[1]
    mid = (T1 - 1) * ci
    r = jnp.concatenate([kf[..., :mid], kf[..., mid:] + kb[..., :ci], kb[..., ci:]], axis=-1)
    toep_t = jnp.stack([r[..., (T1 - 1 - t) * ci:(2 * T1 - 1 - t) * ci] for t in range(T1)], axis=1)
    toep_t = toep_t.reshape(g, kt, kt)
    w1t = jnp.concatenate([toep_t, wend_t], axis=1).astype(BF16)

    dup = lambda w: jnp.concatenate([w, w], axis=-1)
    j_out = jnp.stack([tt + 1.0, T1 - tt])[:, None, :, None]
    po_re, po_im = power(j_out, dup(lr)[:, :, None, :], dup(ang)[:, :, None, :])
    ca_re, ca_im = _cmul(dup(c_re)[:, :, None], dup(c_im)[:, :, None], po_re[:, :, :, None], po_im[:, :, :, None])
    ca_re, ca_im = ca_re.reshape(ndir, g, kt, 2 * p), ca_im.reshape(ndir, g, kt, 2 * p)
    wout2 = jnp.concatenate([ca_re[0], -ca_im[0], ca_re[1], -ca_im[1]], axis=-1).astype(BF16)

    pt_re, pt_im = power(float(T1), lr, ang)
    pw = jnp.stack([pt_re[0], pt_im[0], pt_re[1], pt_im[1]], axis=0)
    pw = jnp.transpose(pw.reshape(4, g // 2, 2 * p), (1, 0, 2))
    return w1t, wout2, pw


def _trunk(x, pos_tabs, mod, mod_row, h0p, lw, nsb, emit_state, tm_a, tm_b):
    bsz, n, d = x.shape
    x2 = x.reshape(bsz * n, d)
    pa, xb = _mixa_call(x2, pos_tabs, mod, mod_row(tm_a), n // tm_a, lw["g_norm_mix"], lw["w_uvgx"],
                        lw["g_sgu"], lw["w_s"], lw["b_s"], lw["w_pa"], tm_a)
    res = _s5_call(xb, lw["w1t"], lw["wout2"], lw["pw"], lw["d_skip"], h0p, n, nsb, emit_state)
    x1 = _mixb_call(x2, pos_tabs, mod, mod_row(tm_b), n // tm_b, lw["g_norm_mix"], res[0], pa, lw["w_gb"],
                    lw["w_glu"], lw["b_glu"], lw["w_pb"], lw["w_out"], tm_b)
    return x1, (res[1] if emit_state else None)


def kernel(x_prompt, x_sample, state_ssm_re, state_ssm_im, c, c_ctx, w_ada, b_ada, g_norm_mix, w_in,
           g_sgu, w_spatial, b_spatial, ssm_a_re, ssm_a_im, ssm_log_dt, ssm_b_re, ssm_b_im, ssm_c_re,
           ssm_c_im, ssm_d, w_glu, b_glu, w_proj_a, w_proj_b, w_out, g_norm_mlp, w_mlp_in, w_mlp_out,
           g_final):
    bp, sp, d = x_prompt.shape
    bs, ss, _ = x_sample.shape
    depth = w_in.shape[0]
    assert depth == 1, "positional embedding and final norm are fused assuming a single trunk layer"
    da = w_proj_a.shape[1]
    db = w_proj_b.shape[1]
    g, p = ssm_a_re.shape[2], ssm_a_re.shape[3]
    tm_a, tm_b, tm_mlp = 256, 256, 512
    tf = min(1024, w_mlp_in.shape[2] // 2)

    rows = -(-(bs + 1) // 8) * 8
    cc = jnp.zeros((rows, d), F32).at[:bs].set(c).at[bs].set(c_ctx)
    pos_tabs = _pos_tables(ss, d)
    l = 0
    mod = _mod_call(cc, w_ada[l], b_ada[l]).reshape(rows, N_MOD, d)
    w1t, wout2, pw = _ssm_prepare(ssm_a_re[l], ssm_a_im[l], ssm_log_dt[l], ssm_b_re[l],
                                  ssm_b_im[l], ssm_c_re[l], ssm_c_im[l])
    wi = w_in[l]
    lw = dict(
        g_norm_mix=g_norm_mix[l].reshape(1, d),
        w_uvgx=jnp.concatenate([wi[:, :2 * da], wi[:, 2 * da + db:2 * da + db + d],
                                wi[:, 2 * da:2 * da + db]], axis=1).astype(BF16),
        w_gb=wi[:, 2 * da + db + d:].astype(BF16),
        g_sgu=g_sgu[l].reshape(1, da),
        w_s=w_spatial[l].astype(BF16),
        b_s=b_spatial[l][:, :, None],
        w_pa=w_proj_a[l].astype(BF16),
        w_glu=w_glu[l].astype(BF16),
        b_glu=b_glu[l].reshape(1, db),
        w_pb=w_proj_b[l].astype(BF16),
        w_out=w_out[l].astype(BF16),
        w1t=w1t, wout2=wout2, pw=pw, d_skip=ssm_d[l].reshape(1, db))

    def pairs(st):
        return jnp.transpose(st.reshape(bs, 2, g // 2, 2 * p), (1, 2, 0, 3))
    sre, sim = pairs(state_ssm_re[:, l]), pairs(state_ssm_im[:, l])
    h0p = jnp.stack([sre[0], sim[0], sre[1], sim[1]], axis=1)[None]

    ctx_row = lambda tm: (lambda i: bs)
    seq_row = lambda tm: (lambda i: i // (ss // tm))
    xp1, e = _trunk(x_prompt, None, mod, ctx_row, None, lw, bp, True, tm_a, tm_b)
    xs1, _ = _trunk(x_sample, pos_tabs, mod, seq_row, h0p, lw, bs, False, tm_a, tm_b)

    gf = g_final.reshape(1, d)
    w1, w2 = w_mlp_in[l].astype(BF16), w_mlp_out[l].astype(BF16)
    gm = g_norm_mlp[l].reshape(1, d)
    xp = _mlp_call(xp1, mod, ctx_row(tm_mlp), gm, w1, w2, gf, tm_mlp, tf).reshape(bp, sp, d)
    xs = _mlp_call(xs1, mod, seq_row(tm_mlp), gm, w1, w2, gf, tm_mlp, tf).reshape(bs, ss, d)

    ev = e.reshape(g // 2, 2, 2, bp, 2, p)
    fin = jnp.transpose(ev, (2, 3, 1, 0, 4, 5)).reshape(2, bp, 2, g, p)
    return xp, xs, fin[0][:, None], fin[1][:, None]
```

```python
import functools
import math

import jax
import jax.numpy as jnp
from jax import lax
from jax.experimental import pallas as pl
from jax.experimental.pallas import tpu as pltpu

F32 = jnp.float32
BF16 = jnp.bfloat16

EPS = 1e-6
POS_BASE = 10000.0
GRID_W = 64
N_MOD = 6
T1 = 16
LANES = 128
SUBLANES = 8
VMEM_LIMIT = 63 * 1024 * 1024


def _gelu(x):
    return 0.5 * x * (1.0 + jnp.tanh(0.7978845608028654 * (x + 0.044715 * (x * x * x))))


def _sigmoid(x):
    return 1.0 / (1.0 + jnp.exp(-x))


def _rms(x, g):
    return x * lax.rsqrt(jnp.mean(x * x, axis=-1, keepdims=True) + EPS) * g


def _modnorm(x, g, shift, scale):
    return _rms(x, g) * (1.0 + scale) + shift


def _resident(shape):
    nd = len(shape)
    return pl.BlockSpec(shape, lambda *_: (0,) * nd, pipeline_mode=pl.Buffered(1))


def _params(sem, flags=None):
    return pltpu.CompilerParams(dimension_semantics=sem, vmem_limit_bytes=VMEM_LIMIT, flags=flags)


def _add_pos(x, er_ref, ec_ref, tile_in_seq):
    tm, d = x.shape
    half = d // 2
    r0 = tile_in_seq * (tm // GRID_W)
    ec = ec_ref[...]
    parts = []
    for k in range(tm // GRID_W):
        er = jnp.broadcast_to(er_ref[pl.ds(r0 + k, 1), :], (GRID_W, half))
        parts.append(jnp.concatenate([er, ec], axis=1))
    return x + jnp.concatenate(parts, axis=0)


def _mod_kernel(c_ref, w_ref, b_ref, o_ref):
    cc = c_ref[...]
    s = cc * _sigmoid(cc)
    o_ref[...] = jnp.dot(s, w_ref[...], preferred_element_type=F32) + b_ref[...]


def _mod_call(cc, w_ada, b_ada):
    rows, d = cc.shape
    n = w_ada.shape[1]
    tn = n // (2 * N_MOD)
    assert n % tn == 0 and tn % LANES == 0
    return pl.pallas_call(
        _mod_kernel,
        grid=(n // tn,),
        in_specs=[pl.BlockSpec((rows, d), lambda j: (0, 0)),
                  pl.BlockSpec((d, tn), lambda j: (0, j)),
                  pl.BlockSpec((1, tn), lambda j: (0, j))],
        out_specs=pl.BlockSpec((rows, tn), lambda j: (0, j)),
        out_shape=jax.ShapeDtypeStruct((rows, n), F32),
        compiler_params=_params(("arbitrary",)),
        name="mod",
    )(cc, w_ada, b_ada.reshape(1, n))


def _mixa_kernel(*refs, pos_blocks, chunk, da):
    if pos_blocks:
        (x_ref, er_ref, ec_ref, mod_ref, g_ref, w_ref, gs_ref, ws_ref, bs_ref, wpa_ref,
         o_ref, xb_ref, ya_ref, xs_ref) = refs
        x = _add_pos(x_ref[...], er_ref, ec_ref, pl.program_id(0) % pos_blocks)
    else:
        (x_ref, mod_ref, g_ref, w_ref, gs_ref, ws_ref, bs_ref, wpa_ref,
         o_ref, xb_ref, ya_ref, xs_ref) = refs
        x = x_ref[...]
    tm, d = x.shape
    h = _modnorm(x, g_ref[...], mod_ref[0:1, :], mod_ref[1:2, :]).astype(BF16)
    z = jnp.dot(h, w_ref[...], preferred_element_type=F32)
    db = xs_ref.shape[0] * LANES
    for cb in range(xs_ref.shape[0]):
        c0 = 2 * da + cb * LANES
        xs_ref[cb] = z[:, c0:c0 + LANES]
        for t in range(T1):
            xb_ref[t, :, cb * LANES:(cb + 1) * LANES] = (
                xs_ref[cb, pl.ds(t, tm // T1, stride=T1), :].astype(xb_ref.dtype))
    u = _gelu(z[:, :da])
    v = _rms(_gelu(z[:, da:2 * da]), gs_ref[...]).astype(BF16)
    ng = ws_ref.shape[0]
    cg = da // ng
    for ck in range(tm // chunk):
        r0 = ck * chunk
        for gi in range(ng):
            c0 = gi * cg
            s = jnp.dot(ws_ref[gi], v[r0:r0 + chunk, c0:c0 + cg], preferred_element_type=F32)
            ya_ref[r0:r0 + chunk, c0:c0 + cg] = (u[r0:r0 + chunk, c0:c0 + cg] * (s + bs_ref[gi])).astype(BF16)
    pa = jnp.dot(ya_ref[...], wpa_ref[...], preferred_element_type=F32)
    o_ref[...] = (_sigmoid(z[:, 2 * da + db:2 * da + db + d]) * pa).astype(o_ref.dtype)


def _mixa_call(x2, pos_tabs, mod, mod_row, pos_blocks, g_norm, w_in, db, g_sgu, w_s, b_s, w_pa, tm):
    ntok, d = x2.shape
    da = w_pa.shape[0]
    chunk = w_s.shape[1]
    in_specs = [pl.BlockSpec((tm, d), lambda i: (i, 0))]
    args = [x2]
    if pos_tabs is not None:
        in_specs += [_resident(pos_tabs[0].shape), _resident(pos_tabs[1].shape)]
        args += list(pos_tabs)
    in_specs += [pl.BlockSpec((None, N_MOD, d), lambda i: (mod_row(i), 0, 0)),
                 _resident((1, d)), _resident((d, 2 * da + db + d)), _resident((1, da)),
                 _resident(w_s.shape), _resident(b_s.shape), _resident(w_pa.shape)]
    args += [mod, g_norm, w_in, g_sgu, w_s, b_s, w_pa]
    return pl.pallas_call(
        functools.partial(_mixa_kernel, pos_blocks=pos_blocks if pos_tabs is not None else 0,
                          chunk=chunk, da=da),
        grid=(ntok // tm,),
        in_specs=in_specs,
        out_specs=[pl.BlockSpec((tm, d), lambda i: (i, 0)),
                   pl.BlockSpec((T1, tm // T1, db), lambda i: (0, i, 0))],
        out_shape=[jax.ShapeDtypeStruct((ntok, d), BF16),
                   jax.ShapeDtypeStruct((T1, ntok // T1, db), BF16)],
        scratch_shapes=[pltpu.VMEM((tm, da), BF16), pltpu.VMEM((db // LANES, tm, LANES), F32)],
        compiler_params=_params(("arbitrary",)),
        name="mixa",
    )(*args)


def _s5_kernel(*refs, nsb, n_chunks, has_h0, emit_state, p, ci):
    it = iter(refs)
    x_ref, w1_ref, wout_ref, pw_ref, d_ref = next(it), next(it), next(it), next(it), next(it)
    h0_ref = next(it) if has_h0 else None
    y_ref = next(it)
    e_ref = next(it) if emit_state else None
    ut_ref, s_ref, hin_ref = next(it), next(it), next(it)

    ngl, kt, cols = ut_ref.shape
    npair = ngl // 2
    pitch = s_ref.shape[2] // nsb

    for t in range(T1):
        st = x_ref[t].astype(F32).T
        for gl in range(ngl):
            ut_ref[gl, t * ci:(t + 1) * ci, :] = st[gl * ci:(gl + 1) * ci, :]

    def local(pr, carry):
        r = []
        for hlf in range(2):
            gl = 2 * pr + hlf
            r1 = jnp.dot(w1_ref[gl], ut_ref[gl].astype(BF16), preferred_element_type=F32)
            ut_ref[gl] = r1[:kt]
            r.append(r1)
        for q in range(4):
            lo = kt + q * p
            sq = jnp.concatenate([r[0][lo:lo + p], r[1][lo:lo + p]], axis=0).T
            for b in range(nsb):
                s_ref[pr, q, b * pitch:b * pitch + n_chunks, :] = sq[b * n_chunks:(b + 1) * n_chunks]
        return carry

    lax.fori_loop(0, npair, local, 0, unroll=True)

    def rows(n):
        return pl.ds(n, nsb, stride=pitch) if nsb > 1 else pl.ds(n, 1)

    def step(n, hs):
        rf, rb = rows(n), rows(n_chunks - 1 - n)
        out = []
        for pr in range(npair):
            fr, fi, br, bi = hs[4 * pr:4 * pr + 4]
            afr, afi = pw_ref[pr, 0:1, :], pw_ref[pr, 1:2, :]
            abr, abi = pw_ref[pr, 2:3, :], pw_ref[pr, 3:4, :]
            sfr, sfi = s_ref[pr, 0, rf, :], s_ref[pr, 1, rf, :]
            sbr, sbi = s_ref[pr, 2, rb, :], s_ref[pr, 3, rb, :]
            hin_ref[pr, 0, rf, :] = fr
            hin_ref[pr, 1, rf, :] = fi
            hin_ref[pr, 2, rb, :] = br
            hin_ref[pr, 3, rb, :] = bi
            out += [afr * fr - afi * fi + sfr, afr * fi + afi * fr + sfi,
                    abr * br - abi * bi + sbr, abr * bi + abi * br + sbi]
        return tuple(out)

    if has_h0:
        init = tuple(h0_ref[pr, q] for pr in range(npair) for q in range(4))
    else:
        init = tuple(jnp.zeros((nsb, LANES), F32) for _ in range(4 * npair))
    fin = lax.fori_loop(0, n_chunks, step, init, unroll=4)
    if emit_state:
        for pr in range(npair):
            for q in range(4):
                e_ref[pr, q] = fin[4 * pr + q]

    first = lax.broadcasted_iota(jnp.int32, (cols, 8 * p), 1) % LANES < p

    def inter(pr, carry):
        hin = jnp.concatenate(
            [jnp.concatenate([hin_ref[pr, q, b * pitch:b * pitch + n_chunks, :] for b in range(nsb)], axis=0)
             for q in range(4)], axis=1)
        for hlf, hg in enumerate((jnp.where(first, hin, 0.0), jnp.where(first, 0.0, hin))):
            gl = 2 * pr + hlf
            ut_ref[gl] += lax.dot_general(wout_ref[gl], hg.astype(BF16), (((1,), (1,)), ((), ())),
                                          preferred_element_type=F32)
        return carry

    lax.fori_loop(0, npair, inter, 0, unroll=True)

    for t in range(T1):
        yt = jnp.concatenate([ut_ref[gl, t * ci:(t + 1) * ci, :] for gl in range(ngl)], axis=0)
        y_ref[t] = _gelu(yt.T + d_ref[...] * x_ref[t].astype(F32)).astype(y_ref.dtype)


def _s5_call(xb, w1t, wout2, pw, d_skip, h0p, n, nsb, emit_state):
    _, ncol, db = xb.shape
    g, kt = w1t.shape[0], w1t.shape[2]
    p = (w1t.shape[1] - kt) // 4
    ci = db // g
    ngl = LANES // ci
    npair = ngl // 2
    assert 2 * p == LANES and ngl % 2 == 0
    n_chunks = n // T1
    cols = nsb * n_chunks
    has_h0 = h0p is not None
    in_specs = [pl.BlockSpec((T1, cols, LANES), lambda i, j: (0, j, i)),
                pl.BlockSpec((ngl,) + w1t.shape[1:], lambda i, j: (i, 0, 0)),
                pl.BlockSpec((ngl,) + wout2.shape[1:], lambda i, j: (i, 0, 0)),
                pl.BlockSpec((npair,) + pw.shape[1:], lambda i, j: (i, 0, 0)),
                pl.BlockSpec((1, LANES), lambda i, j: (0, i))]
    args = [xb, w1t, wout2, pw, d_skip]
    if has_h0:
        in_specs.append(pl.BlockSpec((None, npair, 4, nsb, LANES), lambda i, j: (j, i, 0, 0, 0)))
        args.append(h0p)
    out_specs = [pl.BlockSpec((T1, cols, LANES), lambda i, j: (0, j, i))]
    out_shape = [jax.ShapeDtypeStruct((T1, ncol, db), xb.dtype)]
    if emit_state:
        out_specs.append(pl.BlockSpec((None, npair, 4, nsb, LANES), lambda i, j: (j, i, 0, 0, 0)))
        out_shape.append(jax.ShapeDtypeStruct((ncol // cols, g // 2, 4, nsb, LANES), F32))
    return pl.pallas_call(
        functools.partial(_s5_kernel, nsb=nsb, n_chunks=n_chunks, has_h0=has_h0, emit_state=emit_state,
                          p=p, ci=ci),
        grid=(g // ngl, ncol // cols),
        in_specs=in_specs,
        out_specs=out_specs,
        out_shape=out_shape,
        scratch_shapes=[pltpu.VMEM((ngl, kt, cols), F32)]
        + [pltpu.VMEM((npair, 4, nsb * (n_chunks + SUBLANES), LANES), F32)] * 2,
        compiler_params=_params(("arbitrary", "arbitrary")),
        name="s5",
    )(*args)


def _mixb_kernel(*refs, pos_blocks, n_gb):
    if pos_blocks:
        x_ref, er_ref, ec_ref, mod_ref, g_ref, y_ref, pa_ref = refs[:7]
        x = _add_pos(x_ref[...], er_ref, ec_ref, pl.program_id(0) % pos_blocks)
        refs = refs[7:]
    else:
        x_ref, mod_ref, g_ref, y_ref, pa_ref = refs[:5]
        x = x_ref[...]
        refs = refs[5:]
    wgb_refs = refs[:n_gb]
    wglu_ref, bglu_ref, wpb_ref, wout_ref, o_ref, ys_ref = refs[n_gb:]
    tm = x.shape[0]
    h = _modnorm(x, g_ref[...], mod_ref[0:1, :], mod_ref[1:2, :]).astype(BF16)
    for cb in range(ys_ref.shape[0]):
        for t in range(T1):
            ys_ref[cb, pl.ds(t, tm // T1, stride=T1), :] = y_ref[t, :, cb * LANES:(cb + 1) * LANES].astype(F32)
    y = jnp.concatenate([ys_ref[cb] for cb in range(ys_ref.shape[0])], axis=1)
    gate = _sigmoid(jnp.dot(y.astype(BF16), wglu_ref[...], preferred_element_type=F32) + bglu_ref[...])
    yb = (y * gate).astype(BF16)
    pb = jnp.dot(yb, wpb_ref[...], preferred_element_type=F32)
    glb = jnp.concatenate([jnp.dot(h, w[...], preferred_element_type=F32) for w in wgb_refs], axis=1)
    pb = _sigmoid(glb) * pb
    m = (pa_ref[...].astype(F32) + pb).astype(BF16)
    o_ref[...] = x + mod_ref[2:3, :] * jnp.dot(m, wout_ref[...], preferred_element_type=F32)


def _mixb_call(x2, pos_tabs, mod, mod_row, pos_blocks, g_norm, y, pa, w_in, w_glu, b_glu, w_pb, w_out, tm):
    ntok, d = x2.shape
    db = y.shape[2]
    gb0 = w_in.shape[1] - d
    bw = math.gcd(gb0, d)
    n_gb = d // bw
    in_specs = [pl.BlockSpec((tm, d), lambda i: (i, 0))]
    args = [x2]
    if pos_tabs is not None:
        in_specs += [_resident(pos_tabs[0].shape), _resident(pos_tabs[1].shape)]
        args += list(pos_tabs)
    in_specs += [pl.BlockSpec((None, N_MOD, d), lambda i: (mod_row(i), 0, 0)),
                 _resident((1, d)),
                 pl.BlockSpec((T1, tm // T1, db), lambda i: (0, i, 0)),
                 pl.BlockSpec((tm, d), lambda i: (i, 0)),
                 *[pl.BlockSpec((d, bw), functools.partial(lambda k, i: (0, gb0 // bw + k), k),
                                pipeline_mode=pl.Buffered(1)) for k in range(n_gb)],
                 _resident(w_glu.shape), _resident((1, db)),
                 _resident(w_pb.shape), _resident(w_out.shape)]
    args += [mod, g_norm, y, pa] + [w_in] * n_gb + [w_glu, b_glu, w_pb, w_out]
    return pl.pallas_call(
        functools.partial(_mixb_kernel, pos_blocks=pos_blocks if pos_tabs is not None else 0, n_gb=n_gb),
        grid=(ntok // tm,),
        in_specs=in_specs,
        out_specs=pl.BlockSpec((tm, d), lambda i: (i, 0)),
        out_shape=jax.ShapeDtypeStruct((ntok, d), F32),
        scratch_shapes=[pltpu.VMEM((db // LANES, tm, LANES), F32)],
        compiler_params=_params(("arbitrary",)),
        name="mixb",
    )(*args)


def _mlp_kernel(x_ref, mod_ref, g_ref, w1_ref, w2_ref, gf_ref, o_ref, h_ref, acc_ref):
    j = pl.program_id(1)
    nj = pl.num_programs(1)

    def ffn(h):
        hid = jnp.dot(h, w1_ref[...], preferred_element_type=F32)
        hid = jnp.square(jnp.maximum(hid, 0.0)).astype(BF16)
        return jnp.dot(hid, w2_ref[...], preferred_element_type=F32)

    @pl.when(j == 0)
    def _():
        h = _modnorm(x_ref[...], g_ref[...], mod_ref[3:4, :], mod_ref[4:5, :]).astype(BF16)
        h_ref[...] = h
        acc_ref[...] = ffn(h)

    @pl.when(jnp.logical_and(j > 0, j < nj - 1))
    def _():
        acc_ref[...] += ffn(h_ref[...])

    @pl.when(j == nj - 1)
    def _():
        x2 = x_ref[...] + mod_ref[5:6, :] * (acc_ref[...] + ffn(h_ref[...]))
        o_ref[...] = _rms(x2, gf_ref[...])


def _mlp_call(x1, mod, mod_row, g_norm, w1, w2, g_final, tm, tf):
    ntok, d = x1.shape
    dff = w1.shape[1]
    assert dff // tf >= 2
    return pl.pallas_call(
        _mlp_kernel,
        grid=(ntok // tm, dff // tf),
        in_specs=[pl.BlockSpec((tm, d), lambda i, j: (i, 0)),
                  pl.BlockSpec((None, N_MOD, d), lambda i, j: (mod_row(i), 0, 0)),
                  _resident((1, d)),
                  pl.BlockSpec((d, tf), lambda i, j: (0, j)),
                  pl.BlockSpec((tf, d), lambda i, j: (j, 0)),
                  _resident((1, d))],
        out_specs=pl.BlockSpec((tm, d), lambda i, j: (i, 0)),
        out_shape=jax.ShapeDtypeStruct((ntok, d), F32),
        scratch_shapes=[pltpu.VMEM((tm, d), BF16), pltpu.VMEM((tm, d), F32)],
        compiler_params=_params(("arbitrary", "arbitrary")),
        name="mlp",
    )(x1, mod, g_norm, w1, w2, g_final)


def _pos_tables(n_tokens, d):
    rows = n_tokens // GRID_W
    quarter = d // 4
    omega = 1.0 / (POS_BASE ** (jnp.arange(quarter, dtype=F32) / quarter))
    r = jnp.arange(rows, dtype=F32)[:, None] * omega
    col = jnp.arange(GRID_W, dtype=F32)[:, None] * omega
    e_r = jnp.concatenate([jnp.sin(r), jnp.cos(r)], axis=-1)
    e_c = jnp.concatenate([jnp.sin(col), jnp.cos(col)], axis=-1)
    return e_r, e_c


def _cmul(ar, ai, br, bi):
    return ar * br - ai * bi, ar * bi + ai * br


def _ssm_prepare(a_re, a_im, log_dt, b_re, b_im, c_re, c_im):
    ndir, g, p = a_re.shape
    ci = b_re.shape[-1]
    kt = T1 * ci
    hi = lax.Precision.HIGHEST
    dt = jnp.exp(log_dt)[..., None]
    lr, ang = a_re * dt, a_im * dt

    def power(j, lr_, ang_):
        m = jnp.exp(j * lr_)
        return m * jnp.cos(j * ang_), m * jnp.sin(j * ang_)

    ab_re, ab_im = power(1.0, lr, ang)
    den = a_re * a_re + a_im * a_im
    f_re, f_im = _cmul(ab_re - 1.0, ab_im, a_re / den, -a_im / den)
    bb_re, bb_im = _cmul(f_re[..., None], f_im[..., None], b_re, b_im)

    tt = jnp.arange(T1, dtype=F32)
    j_in = jnp.stack([T1 - 1 - tt, tt])[:, None, None, :]
    pl_re, pl_im = power(j_in, lr[..., None], ang[..., None])
    ba_re, ba_im = _cmul(jnp.repeat(pl_re, ci, axis=-1), jnp.repeat(pl_im, ci, axis=-1),
                         jnp.tile(bb_re, (1, 1, 1, T1)), jnp.tile(bb_im, (1, 1, 1, T1)))
    wend_t = jnp.concatenate([ba_re[0], ba_im[0], ba_re[1], ba_im[1]], axis=1)

    kk = (jnp.einsum('dgcp,dgpl->dgcl', c_re, ba_re, precision=hi)
          - jnp.einsum('dgcp,dgpl->dgcl', c_im, ba_im, precision=hi))
    kf, kb = kk[0], kk[1]
    mid = (T1 - 1) * ci
    r = jnp.concatenate([kf[..., :mid], kf[..., mid:] + kb[..., :ci], kb[..., ci:]], axis=-1)
    toep_t = jnp.stack([r[..., (T1 - 1 - t) * ci:(2 * T1 - 1 - t) * ci] for t in range(T1)], axis=1)
    toep_t = toep_t.reshape(g, kt, kt)
    w1t = jnp.concatenate([toep_t, wend_t], axis=1).astype(BF16)

    dup = lambda w: jnp.concatenate([w, w], axis=-1)
    j_out = jnp.stack([tt + 1.0, T1 - tt])[:, None, :, None]
    po_re, po_im = power(j_out, dup(lr)[:, :, None, :], dup(ang)[:, :, None, :])
    ca_re, ca_im = _cmul(dup(c_re)[:, :, None], dup(c_im)[:, :, None], po_re[:, :, :, None], po_im[:, :, :, None])
    ca_re, ca_im = ca_re.reshape(ndir, g, kt, 2 * p), ca_im.reshape(ndir, g, kt, 2 * p)
    wout2 = jnp.concatenate([ca_re[0], -ca_im[0], ca_re[1], -ca_im[1]], axis=-1).astype(BF16)

    pt_re, pt_im = power(float(T1), lr, ang)
    pw = jnp.stack([pt_re[0], pt_im[0], pt_re[1], pt_im[1]], axis=0)
    pw = jnp.transpose(pw.reshape(4, g // 2, 2 * p), (1, 0, 2))
    return w1t, wout2, pw


def _trunk(x, pos_tabs, mod, mod_row, h0p, lw, nsb, emit_state, tm_a, tm_b):
    bsz, n, d = x.shape
    x2 = x.reshape(bsz * n, d)
    pa, xb = _mixa_call(x2, pos_tabs, mod, mod_row(tm_a), n // tm_a, lw["g_norm_mix"], lw["w_in"], lw["db"],
                        lw["g_sgu"], lw["w_s"], lw["b_s"], lw["w_pa"], tm_a)
    res = _s5_call(xb, lw["w1t"], lw["wout2"], lw["pw"], lw["d_skip"], h0p, n, nsb, emit_state)
    x1 = _mixb_call(x2, pos_tabs, mod, mod_row(tm_b), n // tm_b, lw["g_norm_mix"], res[0], pa, lw["w_in"],
                    lw["w_glu"], lw["b_glu"], lw["w_pb"], lw["w_out"], tm_b)
    return x1, (res[1] if emit_state else None)


def kernel(x_prompt, x_sample, state_ssm_re, state_ssm_im, c, c_ctx, w_ada, b_ada, g_norm_mix, w_in,
           g_sgu, w_spatial, b_spatial, ssm_a_re, ssm_a_im, ssm_log_dt, ssm_b_re, ssm_b_im, ssm_c_re,
           ssm_c_im, ssm_d, w_glu, b_glu, w_proj_a, w_proj_b, w_out, g_norm_mlp, w_mlp_in, w_mlp_out,
           g_final):
    bp, sp, d = x_prompt.shape
    bs, ss, _ = x_sample.shape
    depth = w_in.shape[0]
    assert depth == 1, "positional embedding and final norm are fused assuming a single trunk layer"
    da = w_proj_a.shape[1]
    db = w_proj_b.shape[1]
    g, p = ssm_a_re.shape[2], ssm_a_re.shape[3]
    tm_a, tm_b, tm_mlp = 512, 512, 1024
    tf = min(512, w_mlp_in.shape[2] // 2)

    rows = -(-(bs + 1) // 8) * 8
    cc = jnp.concatenate([c, c_ctx[None], jnp.zeros((rows - bs - 1, d), F32)], axis=0)
    pos_tabs = _pos_tables(ss, d)
    l = 0
    mod = _mod_call(cc, w_ada[l], b_ada[l]).reshape(rows, N_MOD, d)
    w1t, wout2, pw = _ssm_prepare(ssm_a_re[l], ssm_a_im[l], ssm_log_dt[l], ssm_b_re[l],
                                  ssm_b_im[l], ssm_c_re[l], ssm_c_im[l])
    wi = w_in[l]
    lw = dict(
        g_norm_mix=g_norm_mix[l].reshape(1, d),
        w_in=wi.astype(BF16), db=db,
        g_sgu=g_sgu[l].reshape(1, da),
        w_s=w_spatial[l].astype(BF16),
        b_s=b_spatial[l][:, :, None],
        w_pa=w_proj_a[l].astype(BF16),
        w_glu=w_glu[l].astype(BF16),
        b_glu=b_glu[l].reshape(1, db),
        w_pb=w_proj_b[l].astype(BF16),
        w_out=w_out[l].astype(BF16),
        w1t=w1t, wout2=wout2, pw=pw, d_skip=ssm_d[l].reshape(1, db))

    def pairs(st):
        return jnp.transpose(st.reshape(bs, 2, g // 2, 2 * p), (1, 2, 0, 3))
    sre, sim = pairs(state_ssm_re[:, l]), pairs(state_ssm_im[:, l])
    h0p = jnp.stack([sre[0], sim[0], sre[1], sim[1]], axis=1)[None]

    ctx_row = lambda tm: (lambda i: bs)
    seq_row = lambda tm: (lambda i: i // (ss // tm))
    xp1, e = _trunk(x_prompt, None, mod, ctx_row, None, lw, bp, True, tm_a, tm_b)
    xs1, _ = _trunk(x_sample, pos_tabs, mod, seq_row, h0p, lw, bs, False, tm_a, tm_b)

    gf = g_final.reshape(1, d)
    w1, w2 = w_mlp_in[l].astype(BF16), w_mlp_out[l].astype(BF16)
    gm = g_norm_mlp[l].reshape(1, d)
    xp = _mlp_call(xp1, mod, ctx_row(tm_mlp), gm, w1, w2, gf, tm_mlp, tf).reshape(bp, sp, d)
    xs = _mlp_call(xs1, mod, seq_row(tm_mlp), gm, w1, w2, gf, tm_mlp, tf).reshape(bs, ss, d)

    ev = e.reshape(g // 2, 2, 2, bp, 2, p)
    fin = jnp.transpose(ev, (2, 3, 1, 0, 4, 5)).reshape(2, bp, 2, g, p)
    return xp, xs, fin[0][:, None], fin[1][:, None]
```

```python
import functools
import math

import jax
import jax.numpy as jnp
from jax import lax
from jax.experimental import pallas as pl
from jax.experimental.pallas import tpu as pltpu

F32 = jnp.float32
BF16 = jnp.bfloat16

EPS = 1e-6
POS_BASE = 10000.0
GRID_W = 64
N_MOD = 6
T1 = 16
LANES = 128
SUBLANES = 8
VMEM_LIMIT = 63 * 1024 * 1024


def _gelu(x):
    return 0.5 * x * (1.0 + jnp.tanh(0.7978845608028654 * (x + 0.044715 * (x * x * x))))


def _sigmoid(x):
    return 1.0 / (1.0 + jnp.exp(-x))


def _rms(x, g):
    return x * lax.rsqrt(jnp.mean(x * x, axis=-1, keepdims=True) + EPS) * g


def _modnorm(x, g, shift, scale):
    return _rms(x, g) * (1.0 + scale) + shift


def _resident(shape):
    nd = len(shape)
    return pl.BlockSpec(shape, lambda *_: (0,) * nd, pipeline_mode=pl.Buffered(1))


def _params(sem, flags=None):
    return pltpu.CompilerParams(dimension_semantics=sem, vmem_limit_bytes=VMEM_LIMIT, flags=flags)


def _add_pos(x, er_ref, ec_ref, tile_in_seq):
    tm, d = x.shape
    half = d // 2
    r0 = tile_in_seq * (tm // GRID_W)
    ec = ec_ref[...]
    parts = []
    for k in range(tm // GRID_W):
        er = jnp.broadcast_to(er_ref[pl.ds(r0 + k, 1), :], (GRID_W, half))
        parts.append(jnp.concatenate([er, ec], axis=1))
    return x + jnp.concatenate(parts, axis=0)


def _mod_kernel(c_ref, w_ref, b_ref, o_ref):
    cc = c_ref[...]
    s = cc * _sigmoid(cc)
    o_ref[...] = jnp.dot(s, w_ref[...], preferred_element_type=F32) + b_ref[...]


def _mod_call(cc, w_ada, b_ada):
    rows, d = cc.shape
    n = w_ada.shape[1]
    tn = n // (2 * N_MOD)
    assert n % tn == 0 and tn % LANES == 0
    return pl.pallas_call(
        _mod_kernel,
        grid=(n // tn,),
        in_specs=[pl.BlockSpec((rows, d), lambda j: (0, 0)),
                  pl.BlockSpec((d, tn), lambda j: (0, j)),
                  pl.BlockSpec((1, tn), lambda j: (0, j))],
        out_specs=pl.BlockSpec((rows, tn), lambda j: (0, j)),
        out_shape=jax.ShapeDtypeStruct((rows, n), F32),
        compiler_params=_params(("arbitrary",)),
        name="mod",
    )(cc, w_ada, b_ada.reshape(1, n))


def _mixa_kernel(*refs, pos_blocks, chunk, da):
    if pos_blocks:
        (x_ref, er_ref, ec_ref, mod_ref, g_ref, w_ref, gs_ref, ws_ref, bs_ref, wpa_ref,
         o_ref, xb_ref, ya_ref, xs_ref) = refs
        x = _add_pos(x_ref[...], er_ref, ec_ref, pl.program_id(0) % pos_blocks)
    else:
        (x_ref, mod_ref, g_ref, w_ref, gs_ref, ws_ref, bs_ref, wpa_ref,
         o_ref, xb_ref, ya_ref, xs_ref) = refs
        x = x_ref[...]
    tm, d = x.shape
    h = _modnorm(x, g_ref[...], mod_ref[0:1, :], mod_ref[1:2, :]).astype(BF16)
    z = jnp.dot(h, w_ref[...], preferred_element_type=F32)
    db = xs_ref.shape[0] * LANES
    for cb in range(xs_ref.shape[0]):
        c0 = 2 * da + cb * LANES
        xs_ref[cb] = z[:, c0:c0 + LANES]
        for t in range(T1):
            xb_ref[t, :, cb * LANES:(cb + 1) * LANES] = (
                xs_ref[cb, pl.ds(t, tm // T1, stride=T1), :].astype(xb_ref.dtype))
    u = _gelu(z[:, :da])
    v = _rms(_gelu(z[:, da:2 * da]), gs_ref[...]).astype(BF16)
    ng = ws_ref.shape[0]
    cg = da // ng
    for ck in range(tm // chunk):
        r0 = ck * chunk
        for gi in range(ng):
            c0 = gi * cg
            s = jnp.dot(ws_ref[gi], v[r0:r0 + chunk, c0:c0 + cg], preferred_element_type=F32)
            ya_ref[r0:r0 + chunk, c0:c0 + cg] = (u[r0:r0 + chunk, c0:c0 + cg] * (s + bs_ref[gi])).astype(BF16)
    pa = jnp.dot(ya_ref[...], wpa_ref[...], preferred_element_type=F32)
    o_ref[...] = (_sigmoid(z[:, 2 * da + db:2 * da + db + d]) * pa).astype(o_ref.dtype)


def _mixa_call(x2, pos_tabs, mod, mod_row, pos_blocks, g_norm, w_in, db, g_sgu, w_s, b_s, w_pa, tm):
    ntok, d = x2.shape
    da = w_pa.shape[0]
    chunk = w_s.shape[1]
    in_specs = [pl.BlockSpec((tm, d), lambda i: (i, 0))]
    args = [x2]
    if pos_tabs is not None:
        in_specs += [_resident(pos_tabs[0].shape), _resident(pos_tabs[1].shape)]
        args += list(pos_tabs)
    in_specs += [pl.BlockSpec((None, N_MOD, d), lambda i: (mod_row(i), 0, 0)),
                 _resident((1, d)), _resident((d, 2 * da + db + d)), _resident((1, da)),
                 _resident(w_s.shape), _resident(b_s.shape), _resident(w_pa.shape)]
    args += [mod, g_norm, w_in, g_sgu, w_s, b_s, w_pa]
    return pl.pallas_call(
        functools.partial(_mixa_kernel, pos_blocks=pos_blocks if pos_tabs is not None else 0,
                          chunk=chunk, da=da),
        grid=(ntok // tm,),
        in_specs=in_specs,
        out_specs=[pl.BlockSpec((tm, d), lambda i: (i, 0)),
                   pl.BlockSpec((T1, tm // T1, db), lambda i: (0, i, 0))],
        out_shape=[jax.ShapeDtypeStruct((ntok, d), BF16),
                   jax.ShapeDtypeStruct((T1, ntok // T1, db), BF16)],
        scratch_shapes=[pltpu.VMEM((tm, da), BF16), pltpu.VMEM((db // LANES, tm, LANES), F32)],
        compiler_params=_params(("arbitrary",)),
        name="mixa",
    )(*args)


def _s5_kernel(*refs, nsb, n_chunks, has_h0, emit_state, p, ci):
    it = iter(refs)
    x_ref, w1_ref, wout_ref, pw_ref, d_ref = next(it), next(it), next(it), next(it), next(it)
    h0_ref = next(it) if has_h0 else None
    y_ref = next(it)
    e_ref = next(it) if emit_state else None
    ut_ref, s_ref, hin_ref = next(it), next(it), next(it)

    ngl, kt, cols = ut_ref.shape
    npair = ngl // 2
    pitch = s_ref.shape[2] // nsb

    for t in range(T1):
        st = x_ref[t].astype(F32).T
        for gl in range(ngl):
            ut_ref[gl, t * ci:(t + 1) * ci, :] = st[gl * ci:(gl + 1) * ci, :]

    def local(pr, carry):
        r = []
        for hlf in range(2):
            gl = 2 * pr + hlf
            r1 = jnp.dot(w1_ref[gl], ut_ref[gl].astype(BF16), preferred_element_type=F32)
            ut_ref[gl] = r1[:kt]
            r.append(r1)
        for q in range(4):
            lo = kt + q * p
            sq = jnp.concatenate([r[0][lo:lo + p], r[1][lo:lo + p]], axis=0).T
            for b in range(nsb):
                s_ref[pr, q, b * pitch:b * pitch + n_chunks, :] = sq[b * n_chunks:(b + 1) * n_chunks]
        return carry

    lax.fori_loop(0, npair, local, 0, unroll=True)

    def rows(n):
        return pl.ds(n, nsb, stride=pitch) if nsb > 1 else pl.ds(n, 1)

    def step(n, hs):
        rf, rb = rows(n), rows(n_chunks - 1 - n)
        out = []
        for pr in range(npair):
            fr, fi, br, bi = hs[4 * pr:4 * pr + 4]
            afr, afi = pw_ref[pr, 0:1, :], pw_ref[pr, 1:2, :]
            abr, abi = pw_ref[pr, 2:3, :], pw_ref[pr, 3:4, :]
            sfr, sfi = s_ref[pr, 0, rf, :], s_ref[pr, 1, rf, :]
            sbr, sbi = s_ref[pr, 2, rb, :], s_ref[pr, 3, rb, :]
            hin_ref[pr, 0, rf, :] = fr
            hin_ref[pr, 1, rf, :] = fi
            hin_ref[pr, 2, rb, :] = br
            hin_ref[pr, 3, rb, :] = bi
            out += [afr * fr - afi * fi + sfr, afr * fi + afi * fr + sfi,
                    abr * br - abi * bi + sbr, abr * bi + abi * br + sbi]
        return tuple(out)

    if has_h0:
        init = tuple(h0_ref[pr, q] for pr in range(npair) for q in range(4))
    else:
        init = tuple(jnp.zeros((nsb, LANES), F32) for _ in range(4 * npair))
    fin = lax.fori_loop(0, n_chunks, step, init, unroll=4)
    if emit_state:
        for pr in range(npair):
            for q in range(4):
                e_ref[pr, q] = fin[4 * pr + q]

    first = lax.broadcasted_iota(jnp.int32, (cols, 8 * p), 1) % LANES < p

    def inter(pr, carry):
        hin = jnp.concatenate(
            [jnp.concatenate([hin_ref[pr, q, b * pitch:b * pitch + n_chunks, :] for b in range(nsb)], axis=0)
             for q in range(4)], axis=1)
        for hlf, hg in enumerate((jnp.where(first, hin, 0.0), jnp.where(first, 0.0, hin))):
            gl = 2 * pr + hlf
            ut_ref[gl] += lax.dot_general(wout_ref[gl], hg.astype(BF16), (((1,), (1,)), ((), ())),
                                          preferred_element_type=F32)
        return carry

    lax.fori_loop(0, npair, inter, 0, unroll=True)

    for t in range(T1):
        yt = jnp.concatenate([ut_ref[gl, t * ci:(t + 1) * ci, :] for gl in range(ngl)], axis=0)
        y_ref[t] = _gelu(yt.T + d_ref[...] * x_ref[t].astype(F32)).astype(y_ref.dtype)


def _s5_call(xb, w1t, wout2, pw, d_skip, h0p, n, nsb, emit_state):
    _, ncol, db = xb.shape
    g, kt = w1t.shape[0], w1t.shape[2]
    p = (w1t.shape[1] - kt) // 4
    ci = db // g
    ngl = LANES // ci
    npair = ngl // 2
    assert 2 * p == LANES and ngl % 2 == 0
    n_chunks = n // T1
    cols = nsb * n_chunks
    has_h0 = h0p is not None
    in_specs = [pl.BlockSpec((T1, cols, LANES), lambda i, j: (0, j, i)),
                pl.BlockSpec((ngl,) + w1t.shape[1:], lambda i, j: (i, 0, 0)),
                pl.BlockSpec((ngl,) + wout2.shape[1:], lambda i, j: (i, 0, 0)),
                pl.BlockSpec((npair,) + pw.shape[1:], lambda i, j: (i, 0, 0)),
                pl.BlockSpec((1, LANES), lambda i, j: (0, i))]
    args = [xb, w1t, wout2, pw, d_skip]
    if has_h0:
        in_specs.append(pl.BlockSpec((None, npair, 4, nsb, LANES), lambda i, j: (j, i, 0, 0, 0)))
        args.append(h0p)
    out_specs = [pl.BlockSpec((T1, cols, LANES), lambda i, j: (0, j, i))]
    out_shape = [jax.ShapeDtypeStruct((T1, ncol, db), xb.dtype)]
    if emit_state:
        out_specs.append(pl.BlockSpec((None, npair, 4, nsb, LANES), lambda i, j: (j, i, 0, 0, 0)))
        out_shape.append(jax.ShapeDtypeStruct((ncol // cols, g // 2, 4, nsb, LANES), F32))
    return pl.pallas_call(
        functools.partial(_s5_kernel, nsb=nsb, n_chunks=n_chunks, has_h0=has_h0, emit_state=emit_state,
                          p=p, ci=ci),
        grid=(g // ngl, ncol // cols),
        in_specs=in_specs,
        out_specs=out_specs,
        out_shape=out_shape,
        scratch_shapes=[pltpu.VMEM((ngl, kt, cols), F32)]
        + [pltpu.VMEM((npair, 4, nsb * (n_chunks + SUBLANES), LANES), F32)] * 2,
        compiler_params=_params(("arbitrary", "arbitrary")),
        name="s5",
    )(*args)


def _mixb_kernel(*refs, pos_blocks, n_gb):
    if pos_blocks:
        x_ref, er_ref, ec_ref, mod_ref, g_ref, y_ref, pa_ref = refs[:7]
        x = _add_pos(x_ref[...], er_ref, ec_ref, pl.program_id(0) % pos_blocks)
        refs = refs[7:]
    else:
        x_ref, mod_ref, g_ref, y_ref, pa_ref = refs[:5]
        x = x_ref[...]
        refs = refs[5:]
    wgb_refs = refs[:n_gb]
    wglu_ref, bglu_ref, wpb_ref, wout_ref, o_ref, ys_ref = refs[n_gb:]
    tm = x.shape[0]
    h = _modnorm(x, g_ref[...], mod_ref[0:1, :], mod_ref[1:2, :]).astype(BF16)
    for cb in range(ys_ref.shape[0]):
        for t in range(T1):
            ys_ref[cb, pl.ds(t, tm // T1, stride=T1), :] = y_ref[t, :, cb * LANES:(cb + 1) * LANES].astype(F32)
    y = jnp.concatenate([ys_ref[cb] for cb in range(ys_ref.shape[0])], axis=1)
    gate = _sigmoid(jnp.dot(y.astype(BF16), wglu_ref[...], preferred_element_type=F32) + bglu_ref[...])
    yb = (y * gate).astype(BF16)
    pb = jnp.dot(yb, wpb_ref[...], preferred_element_type=F32)
    glb = jnp.concatenate([jnp.dot(h, w[...], preferred_element_type=F32) for w in wgb_refs], axis=1)
    pb = _sigmoid(glb) * pb
    m = (pa_ref[...].astype(F32) + pb).astype(BF16)
    o_ref[...] = x + mod_ref[2:3, :] * jnp.dot(m, wout_ref[...], preferred_element_type=F32)


def _mixb_call(x2, pos_tabs, mod, mod_row, pos_blocks, g_norm, y, pa, w_in, w_glu, b_glu, w_pb, w_out, tm):
    ntok, d = x2.shape
    db = y.shape[2]
    gb0 = w_in.shape[1] - d
    bw = math.gcd(gb0, d)
    n_gb = d // bw
    in_specs = [pl.BlockSpec((tm, d), lambda i: (i, 0))]
    args = [x2]
    if pos_tabs is not None:
        in_specs += [_resident(pos_tabs[0].shape), _resident(pos_tabs[1].shape)]
        args += list(pos_tabs)
    in_specs += [pl.BlockSpec((None, N_MOD, d), lambda i: (mod_row(i), 0, 0)),
                 _resident((1, d)),
                 pl.BlockSpec((T1, tm // T1, db), lambda i: (0, i, 0)),
                 pl.BlockSpec((tm, d), lambda i: (i, 0)),
                 *[pl.BlockSpec((d, bw), functools.partial(lambda k, i: (0, gb0 // bw + k), k),
                                pipeline_mode=pl.Buffered(1)) for k in range(n_gb)],
                 _resident(w_glu.shape), _resident((1, db)),
                 _resident(w_pb.shape), _resident(w_out.shape)]
    args += [mod, g_norm, y, pa] + [w_in] * n_gb + [w_glu, b_glu, w_pb, w_out]
    return pl.pallas_call(
        functools.partial(_mixb_kernel, pos_blocks=pos_blocks if pos_tabs is not None else 0, n_gb=n_gb),
        grid=(ntok // tm,),
        in_specs=in_specs,
        out_specs=pl.BlockSpec((tm, d), lambda i: (i, 0)),
        out_shape=jax.ShapeDtypeStruct((ntok, d), F32),
        scratch_shapes=[pltpu.VMEM((db // LANES, tm, LANES), F32)],
        compiler_params=_params(("arbitrary",)),
        name="mixb",
    )(*args)


def _mlp_kernel(x_ref, mod_ref, g_ref, w1_ref, w2_ref, gf_ref, o_ref, h_ref, acc_ref):
    j = pl.program_id(1)
    nj = pl.num_programs(1)

    def ffn(h):
        hid = jnp.dot(h, w1_ref[...], preferred_element_type=F32)
        hid = jnp.square(jnp.maximum(hid, 0.0)).astype(BF16)
        return jnp.dot(hid, w2_ref[...], preferred_element_type=F32)

    @pl.when(j == 0)
    def _():
        h = _modnorm(x_ref[...], g_ref[...], mod_ref[3:4, :], mod_ref[4:5, :]).astype(BF16)
        h_ref[...] = h
        acc_ref[...] = ffn(h)

    @pl.when(jnp.logical_and(j > 0, j < nj - 1))
    def _():
        acc_ref[...] += ffn(h_ref[...])

    @pl.when(j == nj - 1)
    def _():
        x2 = x_ref[...] + mod_ref[5:6, :] * (acc_ref[...] + ffn(h_ref[...]))
        o_ref[...] = _rms(x2, gf_ref[...])


def _mlp_call(x1, mod, mod_row, g_norm, w1, w2, g_final, tm, tf):
    ntok, d = x1.shape
    dff = w1.shape[1]
    assert dff // tf >= 2
    return pl.pallas_call(
        _mlp_kernel,
        grid=(ntok // tm, dff // tf),
        in_specs=[pl.BlockSpec((tm, d), lambda i, j: (i, 0)),
                  pl.BlockSpec((None, N_MOD, d), lambda i, j: (mod_row(i), 0, 0)),
                  _resident((1, d)),
                  pl.BlockSpec((d, tf), lambda i, j: (0, j)),
                  pl.BlockSpec((tf, d), lambda i, j: (j, 0)),
                  _resident((1, d))],
        out_specs=pl.BlockSpec((tm, d), lambda i, j: (i, 0)),
        out_shape=jax.ShapeDtypeStruct((ntok, d), F32),
        scratch_shapes=[pltpu.VMEM((tm, d), BF16), pltpu.VMEM((tm, d), F32)],
        compiler_params=_params(("arbitrary", "arbitrary")),
        name="mlp",
    )(x1, mod, g_norm, w1, w2, g_final)


def _pos_tables(n_tokens, d):
    rows = n_tokens // GRID_W
    quarter = d // 4
    omega = 1.0 / (POS_BASE ** (jnp.arange(quarter, dtype=F32) / quarter))
    r = jnp.arange(rows, dtype=F32)[:, None] * omega
    col = jnp.arange(GRID_W, dtype=F32)[:, None] * omega
    e_r = jnp.concatenate([jnp.sin(r), jnp.cos(r)], axis=-1)
    e_c = jnp.concatenate([jnp.sin(col), jnp.cos(col)], axis=-1)
    return e_r, e_c


def _cmul(ar, ai, br, bi):
    return ar * br - ai * bi, ar * bi + ai * br


def _ssm_prepare(a_re, a_im, log_dt, b_re, b_im, c_re, c_im):
    ndir, g, p = a_re.shape
    ci = b_re.shape[-1]
    kt = T1 * ci
    hi = lax.Precision.HIGHEST
    dt = jnp.exp(log_dt)[..., None]
    lr, ang = a_re * dt, a_im * dt

    def power(j, lr_, ang_):
        m = jnp.exp(j * lr_)
        return m * jnp.cos(j * ang_), m * jnp.sin(j * ang_)

    ab_re, ab_im = power(1.0, lr, ang)
    den = a_re * a_re + a_im * a_im
    f_re, f_im = _cmul(ab_re - 1.0, ab_im, a_re / den, -a_im / den)
    bb_re, bb_im = _cmul(f_re[..., None], f_im[..., None], b_re, b_im)

    tt = jnp.arange(T1, dtype=F32)
    j_in = jnp.stack([T1 - 1 - tt, tt])[:, None, None, :]
    pl_re, pl_im = power(j_in, lr[..., None], ang[..., None])
    ba_re, ba_im = _cmul(jnp.repeat(pl_re, ci, axis=-1), jnp.repeat(pl_im, ci, axis=-1),
                         jnp.tile(bb_re, (1, 1, 1, T1)), jnp.tile(bb_im, (1, 1, 1, T1)))
    wend_t = jnp.concatenate([ba_re[0], ba_im[0], ba_re[1], ba_im[1]], axis=1)

    kk = (jnp.einsum('dgcp,dgpl->dgcl', c_re, ba_re, precision=hi)
          - jnp.einsum('dgcp,dgpl->dgcl', c_im, ba_im, precision=hi))
    kf, kb = kk[0], kk[1]
    mid = (T1 - 1) * ci
    r = jnp.concatenate([kf[..., :mid], kf[..., mid:] + kb[..., :ci], kb[..., ci:]], axis=-1)
    toep_t = jnp.stack([r[..., (T1 - 1 - t) * ci:(2 * T1 - 1 - t) * ci] for t in range(T1)], axis=1)
    toep_t = toep_t.reshape(g, kt, kt)
    w1t = jnp.concatenate([toep_t, wend_t], axis=1).astype(BF16)

    dup = lambda w: jnp.concatenate([w, w], axis=-1)
    j_out = jnp.stack([tt + 1.0, T1 - tt])[:, None, :, None]
    po_re, po_im = power(j_out, dup(lr)[:, :, None, :], dup(ang)[:, :, None, :])
    ca_re, ca_im = _cmul(dup(c_re)[:, :, None], dup(c_im)[:, :, None], po_re[:, :, :, None], po_im[:, :, :, None])
    ca_re, ca_im = ca_re.reshape(ndir, g, kt, 2 * p), ca_im.reshape(ndir, g, kt, 2 * p)
    wout2 = jnp.concatenate([ca_re[0], -ca_im[0], ca_re[1], -ca_im[1]], axis=-1).astype(BF16)

    pt_re, pt_im = power(float(T1), lr, ang)
    pw = jnp.stack([pt_re[0], pt_im[0], pt_re[1], pt_im[1]], axis=0)
    pw = jnp.transpose(pw.reshape(4, g // 2, 2 * p), (1, 0, 2))
    return w1t, wout2, pw


def _trunk(x, pos_tabs, mod, mod_row, h0p, lw, nsb, emit_state, tm_a, tm_b):
    bsz, n, d = x.shape
    x2 = x.reshape(bsz * n, d)
    pa, xb = _mixa_call(x2, pos_tabs, mod, mod_row(tm_a), n // tm_a, lw["g_norm_mix"], lw["w_in"], lw["db"],
                        lw["g_sgu"], lw["w_s"], lw["b_s"], lw["w_pa"], tm_a)
    res = _s5_call(xb, lw["w1t"], lw["wout2"], lw["pw"], lw["d_skip"], h0p, n, nsb, emit_state)
    x1 = _mixb_call(x2, pos_tabs, mod, mod_row(tm_b), n // tm_b, lw["g_norm_mix"], res[0], pa, lw["w_in"],
                    lw["w_glu"], lw["b_glu"], lw["w_pb"], lw["w_out"], tm_b)
    return x1, (res[1] if emit_state else None)


def kernel(x_prompt, x_sample, state_ssm_re, state_ssm_im, c, c_ctx, w_ada, b_ada, g_norm_mix, w_in,
           g_sgu, w_spatial, b_spatial, ssm_a_re, ssm_a_im, ssm_log_dt, ssm_b_re, ssm_b_im, ssm_c_re,
           ssm_c_im, ssm_d, w_glu, b_glu, w_proj_a, w_proj_b, w_out, g_norm_mlp, w_mlp_in, w_mlp_out,
           g_final):
    bp, sp, d = x_prompt.shape
    bs, ss, _ = x_sample.shape
    depth = w_in.shape[0]
    assert depth == 1, "positional embedding and final norm are fused assuming a single trunk layer"
    da = w_proj_a.shape[1]
    db = w_proj_b.shape[1]
    g, p = ssm_a_re.shape[2], ssm_a_re.shape[3]
    tm_a, tm_b, tm_mlp = 512, 512, 512
    tf = min(2048, w_mlp_in.shape[2] // 2)

    rows = -(-(bs + 1) // 8) * 8
    cc = jnp.concatenate([c, c_ctx[None], jnp.zeros((rows - bs - 1, d), F32)], axis=0)
    pos_tabs = _pos_tables(ss, d)
    l = 0
    mod = _mod_call(cc, w_ada[l], b_ada[l]).reshape(rows, N_MOD, d)
    w1t, wout2, pw = _ssm_prepare(ssm_a_re[l], ssm_a_im[l], ssm_log_dt[l], ssm_b_re[l],
                                  ssm_b_im[l], ssm_c_re[l], ssm_c_im[l])
    wi = w_in[l]
    lw = dict(
        g_norm_mix=g_norm_mix[l].reshape(1, d),
        w_in=wi.astype(BF16), db=db,
        g_sgu=g_sgu[l].reshape(1, da),
        w_s=w_spatial[l].astype(BF16),
        b_s=b_spatial[l][:, :, None],
        w_pa=w_proj_a[l].astype(BF16),
        w_glu=w_glu[l].astype(BF16),
        b_glu=b_glu[l].reshape(1, db),
        w_pb=w_proj_b[l].astype(BF16),
        w_out=w_out[l].astype(BF16),
        w1t=w1t, wout2=wout2, pw=pw, d_skip=ssm_d[l].reshape(1, db))

    def pairs(st):
        return jnp.transpose(st.reshape(bs, 2, g // 2, 2 * p), (1, 2, 0, 3))
    sre, sim = pairs(state_ssm_re[:, l]), pairs(state_ssm_im[:, l])
    h0p = jnp.stack([sre[0], sim[0], sre[1], sim[1]], axis=1)[None]

    ctx_row = lambda tm: (lambda i: bs)
    seq_row = lambda tm: (lambda i: i // (ss // tm))
    xp1, e = _trunk(x_prompt, None, mod, ctx_row, None, lw, bp, True, tm_a, tm_b)
    xs1, _ = _trunk(x_sample, pos_tabs, mod, seq_row, h0p, lw, bs, False, tm_a, tm_b)

    gf = g_final.reshape(1, d)
    w1, w2 = w_mlp_in[l].astype(BF16), w_mlp_out[l].astype(BF16)
    gm = g_norm_mlp[l].reshape(1, d)
    xp = _mlp_call(xp1, mod, ctx_row(tm_mlp), gm, w1, w2, gf, tm_mlp, tf).reshape(bp, sp, d)
    xs = _mlp_call(xs1, mod, seq_row(tm_mlp), gm, w1, w2, gf, tm_mlp, tf).reshape(bs, ss, d)

    ev = e.reshape(g // 2, 2, 2, bp, 2, p)
    fin = jnp.transpose(ev, (2, 3, 1, 0, 4, 5)).reshape(2, bp, 2, g, p)
    return xp, xs, fin[0][:, None], fin[1][:, None]
```

```python
import functools
import math

import jax
import jax.numpy as jnp
from jax import lax
from jax.experimental import pallas as pl
from jax.experimental.pallas import tpu as pltpu

F32 = jnp.float32
BF16 = jnp.bfloat16

EPS = 1e-6
POS_BASE = 10000.0
GRID_W = 64
N_MOD = 6
T1 = 16
LANES = 128
SUBLANES = 8
VMEM_LIMIT = 63 * 1024 * 1024


def _gelu(x):
    return 0.5 * x * (1.0 + jnp.tanh(0.7978845608028654 * (x + 0.044715 * (x * x * x))))


def _sigmoid(x):
    return 1.0 / (1.0 + jnp.exp(-x))


def _rms(x, g):
    return x * lax.rsqrt(jnp.mean(x * x, axis=-1, keepdims=True) + EPS) * g


def _modnorm(x, g, shift, scale):
    return _rms(x, g) * (1.0 + scale) + shift


def _resident(shape):
    nd = len(shape)
    return pl.BlockSpec(shape, lambda *_: (0,) * nd, pipeline_mode=pl.Buffered(1))


def _params(sem):
    return pltpu.CompilerParams(dimension_semantics=sem, vmem_limit_bytes=VMEM_LIMIT)


def _add_pos(x, er_ref, ec_ref, tile_in_seq):
    tm, d = x.shape
    half = d // 2
    r0 = tile_in_seq * (tm // GRID_W)
    ec = ec_ref[...]
    parts = []
    for k in range(tm // GRID_W):
        er = jnp.broadcast_to(er_ref[pl.ds(r0 + k, 1), :], (GRID_W, half))
        parts.append(jnp.concatenate([er, ec], axis=1))
    return x + jnp.concatenate(parts, axis=0)


def _mod_kernel(c_ref, w_ref, b_ref, o_ref):
    cc = c_ref[...]
    s = cc * _sigmoid(cc)
    o_ref[...] = jnp.dot(s, w_ref[...], preferred_element_type=F32) + b_ref[...]


def _mod_call(cc, w_ada, b_ada):
    rows, d = cc.shape
    n = w_ada.shape[1]
    tn = n // (2 * N_MOD)
    assert n % tn == 0 and tn % LANES == 0
    return pl.pallas_call(
        _mod_kernel,
        grid=(n // tn,),
        in_specs=[pl.BlockSpec((rows, d), lambda j: (0, 0)),
                  pl.BlockSpec((d, tn), lambda j: (0, j)),
                  pl.BlockSpec((1, tn), lambda j: (0, j))],
        out_specs=pl.BlockSpec((rows, tn), lambda j: (0, j)),
        out_shape=jax.ShapeDtypeStruct((rows, n), F32),
        compiler_params=_params(("arbitrary",)),
        name="mod",
    )(cc, w_ada, b_ada.reshape(1, n))


def _mixa_kernel(*refs, pos_blocks, chunk, da, n_cast):
    if pos_blocks:
        x_ref, er_ref, ec_ref = refs[:3]
        x = _add_pos(x_ref[...], er_ref, ec_ref, pl.program_id(0) % pos_blocks)
        refs = refs[3:]
    else:
        x_ref = refs[0]
        x = x_ref[...]
        refs = refs[1:]
    mod_ref, g_ref, w_ref, gs_ref, ws_ref, bs_ref, wpa_ref = refs[:7]
    cast_in, (o_ref, xb_ref) = refs[7:7 + n_cast], refs[7 + n_cast:9 + n_cast]
    cast_out, (ya_ref, xs_ref) = refs[9 + n_cast:9 + 2 * n_cast], refs[9 + 2 * n_cast:]
    for ci_ref, co_ref in zip(cast_in, cast_out):
        co_ref[...] = ci_ref[...].astype(co_ref.dtype)
    tm, d = x.shape
    h = _modnorm(x, g_ref[...], mod_ref[0:1, :], mod_ref[1:2, :]).astype(BF16)
    z = jnp.dot(h, w_ref[...], preferred_element_type=F32)
    db = xs_ref.shape[0] * LANES
    for cb in range(xs_ref.shape[0]):
        c0 = 2 * da + cb * LANES
        xs_ref[cb] = z[:, c0:c0 + LANES]
        for t in range(T1):
            xb_ref[t, :, cb * LANES:(cb + 1) * LANES] = (
                xs_ref[cb, pl.ds(t, tm // T1, stride=T1), :].astype(xb_ref.dtype))
    u = _gelu(z[:, :da])
    v = _rms(_gelu(z[:, da:2 * da]), gs_ref[...]).astype(BF16)
    ng = ws_ref.shape[0]
    cg = da // ng
    for ck in range(tm // chunk):
        r0 = ck * chunk
        for gi in range(ng):
            c0 = gi * cg
            s = jnp.dot(ws_ref[gi], v[r0:r0 + chunk, c0:c0 + cg], preferred_element_type=F32)
            ya_ref[r0:r0 + chunk, c0:c0 + cg] = (u[r0:r0 + chunk, c0:c0 + cg] * (s + bs_ref[gi])).astype(BF16)
    pa = jnp.dot(ya_ref[...], wpa_ref[...], preferred_element_type=F32)
    o_ref[...] = (_sigmoid(z[:, 2 * da + db:2 * da + db + d]) * pa).astype(o_ref.dtype)


def _mixa_call(x2, pos_tabs, mod, mod_row, pos_blocks, g_norm, w_in, db, g_sgu, w_s, b_s, w_pa, tm, casts=()):
    ntok, d = x2.shape
    steps = ntok // tm
    da = w_pa.shape[0]
    chunk = w_s.shape[1]
    in_specs = [pl.BlockSpec((tm, d), lambda i: (i, 0))]
    args = [x2]
    if pos_tabs is not None:
        in_specs += [_resident(pos_tabs[0].shape), _resident(pos_tabs[1].shape)]
        args += list(pos_tabs)
    in_specs += [pl.BlockSpec((None, N_MOD, d), lambda i: (mod_row(i), 0, 0)),
                 _resident((1, d)), _resident((d, 2 * da + db + d)), _resident((1, da)),
                 _resident(w_s.shape), _resident(b_s.shape), _resident(w_pa.shape)]
    args += [mod, g_norm, w_in, g_sgu, w_s, b_s, w_pa] + list(casts)
    cast_specs = [pl.BlockSpec((w.shape[0] // steps, w.shape[1]), lambda i: (i, 0)) for w in casts]
    assert all(w.shape[0] % (steps * 2 * SUBLANES) == 0 for w in casts)
    return pl.pallas_call(
        functools.partial(_mixa_kernel, pos_blocks=pos_blocks if pos_tabs is not None else 0,
                          chunk=chunk, da=da, n_cast=len(casts)),
        grid=(steps,),
        in_specs=in_specs + cast_specs,
        out_specs=[pl.BlockSpec((tm, d), lambda i: (i, 0)),
                   pl.BlockSpec((T1, tm // T1, db), lambda i: (0, i, 0))] + cast_specs,
        out_shape=[jax.ShapeDtypeStruct((ntok, d), BF16),
                   jax.ShapeDtypeStruct((T1, ntok // T1, db), BF16)]
        + [jax.ShapeDtypeStruct(w.shape, BF16) for w in casts],
        scratch_shapes=[pltpu.VMEM((tm, da), BF16), pltpu.VMEM((db // LANES, tm, LANES), F32)],
        compiler_params=_params(("arbitrary",)),
        name="mixa",
    )(*args)


def _s5_kernel(*refs, nsb, n_chunks, has_h0, emit_state, p, ci):
    it = iter(refs)
    x_ref, w1_ref, wout_ref, pw_ref, d_ref = next(it), next(it), next(it), next(it), next(it)
    h0_ref = next(it) if has_h0 else None
    y_ref = next(it)
    e_ref = next(it) if emit_state else None
    ut_ref, s_ref, hin_ref = next(it), next(it), next(it)

    ngl, kt, cols = ut_ref.shape
    npair = ngl // 2
    pitch = s_ref.shape[2] // nsb

    for t in range(T1):
        st = x_ref[t].astype(F32).T
        for gl in range(ngl):
            ut_ref[gl, t * ci:(t + 1) * ci, :] = st[gl * ci:(gl + 1) * ci, :]

    def local(pr, carry):
        r = []
        for hlf in range(2):
            gl = 2 * pr + hlf
            r1 = jnp.dot(w1_ref[gl], ut_ref[gl].astype(BF16), preferred_element_type=F32)
            ut_ref[gl] = r1[:kt]
            r.append(r1)
        for q in range(4):
            lo = kt + q * p
            sq = jnp.concatenate([r[0][lo:lo + p], r[1][lo:lo + p]], axis=0).T
            for b in range(nsb):
                s_ref[pr, q, b * pitch:b * pitch + n_chunks, :] = sq[b * n_chunks:(b + 1) * n_chunks]
        return carry

    lax.fori_loop(0, npair, local, 0, unroll=True)

    def rows(n):
        return pl.ds(n, nsb, stride=pitch) if nsb > 1 else pl.ds(n, 1)

    def step(n, hs):
        rf, rb = rows(n), rows(n_chunks - 1 - n)
        out = []
        for pr in range(npair):
            fr, fi, br, bi = hs[4 * pr:4 * pr + 4]
            afr, afi = pw_ref[pr, 0:1, :], pw_ref[pr, 1:2, :]
            abr, abi = pw_ref[pr, 2:3, :], pw_ref[pr, 3:4, :]
            sfr, sfi = s_ref[pr, 0, rf, :], s_ref[pr, 1, rf, :]
            sbr, sbi = s_ref[pr, 2, rb, :], s_ref[pr, 3, rb, :]
            hin_ref[pr, 0, rf, :] = fr
            hin_ref[pr, 1, rf, :] = fi
            hin_ref[pr, 2, rb, :] = br
            hin_ref[pr, 3, rb, :] = bi
            out += [afr * fr - afi * fi + sfr, afr * fi + afi * fr + sfi,
                    abr * br - abi * bi + sbr, abr * bi + abi * br + sbi]
        return tuple(out)

    if has_h0:
        init = tuple(h0_ref[pr, q] for pr in range(npair) for q in range(4))
    else:
        init = tuple(jnp.zeros((nsb, LANES), F32) for _ in range(4 * npair))
    fin = lax.fori_loop(0, n_chunks, step, init, unroll=4)
    if emit_state:
        for pr in range(npair):
            for q in range(4):
                e_ref[pr, q] = fin[4 * pr + q]

    first = lax.broadcasted_iota(jnp.int32, (cols, 8 * p), 1) % LANES < p

    def inter(pr, carry):
        hin = jnp.concatenate(
            [jnp.concatenate([hin_ref[pr, q, b * pitch:b * pitch + n_chunks, :] for b in range(nsb)], axis=0)
             for q in range(4)], axis=1)
        for hlf, hg in enumerate((jnp.where(first, hin, 0.0), jnp.where(first, 0.0, hin))):
            gl = 2 * pr + hlf
            ut_ref[gl] += lax.dot_general(wout_ref[gl], hg.astype(BF16), (((1,), (1,)), ((), ())),
                                          preferred_element_type=F32)
        return carry

    lax.fori_loop(0, npair, inter, 0, unroll=True)

    for t in range(T1):
        yt = jnp.concatenate([ut_ref[gl, t * ci:(t + 1) * ci, :] for gl in range(ngl)], axis=0)
        y_ref[t] = _gelu(yt.T + d_ref[...] * x_ref[t].astype(F32)).astype(y_ref.dtype)


def _s5_call(xb, w1t, wout2, pw, d_skip, h0p, n, nsb, emit_state):
    _, ncol, db = xb.shape
    g, kt = w1t.shape[0], w1t.shape[2]
    p = (w1t.shape[1] - kt) // 4
    ci = db // g
    ngl = LANES // ci
    npair = ngl // 2
    assert 2 * p == LANES and ngl % 2 == 0
    n_chunks = n // T1
    cols = nsb * n_chunks
    has_h0 = h0p is not None
    in_specs = [pl.BlockSpec((T1, cols, LANES), lambda i, j: (0, j, i)),
                pl.BlockSpec((ngl,) + w1t.shape[1:], lambda i, j: (i, 0, 0)),
                pl.BlockSpec((ngl,) + wout2.shape[1:], lambda i, j: (i, 0, 0)),
                pl.BlockSpec((npair,) + pw.shape[1:], lambda i, j: (i, 0, 0)),
                pl.BlockSpec((1, LANES), lambda i, j: (0, i))]
    args = [xb, w1t, wout2, pw, d_skip]
    if has_h0:
        in_specs.append(pl.BlockSpec((None, npair, 4, nsb, LANES), lambda i, j: (j, i, 0, 0, 0)))
        args.append(h0p)
    out_specs = [pl.BlockSpec((T1, cols, LANES), lambda i, j: (0, j, i))]
    out_shape = [jax.ShapeDtypeStruct((T1, ncol, db), xb.dtype)]
    if emit_state:
        out_specs.append(pl.BlockSpec((None, npair, 4, nsb, LANES), lambda i, j: (j, i, 0, 0, 0)))
        out_shape.append(jax.ShapeDtypeStruct((ncol // cols, g // 2, 4, nsb, LANES), F32))
    return pl.pallas_call(
        functools.partial(_s5_kernel, nsb=nsb, n_chunks=n_chunks, has_h0=has_h0, emit_state=emit_state,
                          p=p, ci=ci),
        grid=(g // ngl, ncol // cols),
        in_specs=in_specs,
        out_specs=out_specs,
        out_shape=out_shape,
        scratch_shapes=[pltpu.VMEM((ngl, kt, cols), F32)]
        + [pltpu.VMEM((npair, 4, nsb * (n_chunks + SUBLANES), LANES), F32)] * 2,
        compiler_params=_params(("arbitrary", "arbitrary")),
        name="s5",
    )(*args)


def _mixb_kernel(*refs, pos_blocks, n_gb):
    if pos_blocks:
        x_ref, er_ref, ec_ref, mod_ref, g_ref, y_ref, pa_ref = refs[:7]
        x = _add_pos(x_ref[...], er_ref, ec_ref, pl.program_id(0) % pos_blocks)
        refs = refs[7:]
    else:
        x_ref, mod_ref, g_ref, y_ref, pa_ref = refs[:5]
        x = x_ref[...]
        refs = refs[5:]
    wgb_refs = refs[:n_gb]
    wglu_ref, bglu_ref, wpb_ref, wout_ref, o_ref, ys_ref = refs[n_gb:]
    tm = x.shape[0]
    h = _modnorm(x, g_ref[...], mod_ref[0:1, :], mod_ref[1:2, :]).astype(BF16)
    for cb in range(ys_ref.shape[0]):
        for t in range(T1):
            ys_ref[cb, pl.ds(t, tm // T1, stride=T1), :] = y_ref[t, :, cb * LANES:(cb + 1) * LANES].astype(F32)
    y = jnp.concatenate([ys_ref[cb] for cb in range(ys_ref.shape[0])], axis=1)
    gate = _sigmoid(jnp.dot(y.astype(BF16), wglu_ref[...], preferred_element_type=F32) + bglu_ref[...])
    yb = (y * gate).astype(BF16)
    pb = jnp.dot(yb, wpb_ref[...], preferred_element_type=F32)
    glb = jnp.concatenate([jnp.dot(h, w[...], preferred_element_type=F32) for w in wgb_refs], axis=1)
    pb = _sigmoid(glb) * pb
    m = (pa_ref[...].astype(F32) + pb).astype(BF16)
    o_ref[...] = x + mod_ref[2:3, :] * jnp.dot(m, wout_ref[...], preferred_element_type=F32)


def _mixb_call(x2, pos_tabs, mod, mod_row, pos_blocks, g_norm, y, pa, w_in, w_glu, b_glu, w_pb, w_out, tm):
    ntok, d = x2.shape
    db = y.shape[2]
    gb0 = w_in.shape[1] - d
    bw = math.gcd(gb0, d)
    n_gb = d // bw
    in_specs = [pl.BlockSpec((tm, d), lambda i: (i, 0))]
    args = [x2]
    if pos_tabs is not None:
        in_specs += [_resident(pos_tabs[0].shape), _resident(pos_tabs[1].shape)]
        args += list(pos_tabs)
    in_specs += [pl.BlockSpec((None, N_MOD, d), lambda i: (mod_row(i), 0, 0)),
                 _resident((1, d)),
                 pl.BlockSpec((T1, tm // T1, db), lambda i: (0, i, 0)),
                 pl.BlockSpec((tm, d), lambda i: (i, 0)),
                 *[pl.BlockSpec((d, bw), functools.partial(lambda k, i: (0, gb0 // bw + k), k),
                                pipeline_mode=pl.Buffered(1)) for k in range(n_gb)],
                 _resident(w_glu.shape), _resident((1, db)),
                 _resident(w_pb.shape), _resident(w_out.shape)]
    args += [mod, g_norm, y, pa] + [w_in] * n_gb + [w_glu, b_glu, w_pb, w_out]
    return pl.pallas_call(
        functools.partial(_mixb_kernel, pos_blocks=pos_blocks if pos_tabs is not None else 0, n_gb=n_gb),
        grid=(ntok // tm,),
        in_specs=in_specs,
        out_specs=pl.BlockSpec((tm, d), lambda i: (i, 0)),
        out_shape=jax.ShapeDtypeStruct((ntok, d), F32),
        scratch_shapes=[pltpu.VMEM((db // LANES, tm, LANES), F32)],
        compiler_params=_params(("arbitrary",)),
        name="mixb",
    )(*args)


def _mlp_kernel(x_ref, mod_ref, g_ref, w1_ref, w2_ref, gf_ref, o_ref, h_ref, acc_ref):
    j = pl.program_id(1)
    nj = pl.num_programs(1)

    def ffn(h):
        hid = jnp.dot(h, w1_ref[...], preferred_element_type=F32)
        hid = jnp.square(jnp.maximum(hid, 0.0)).astype(BF16)
        return jnp.dot(hid, w2_ref[...], preferred_element_type=F32)

    @pl.when(j == 0)
    def _():
        h = _modnorm(x_ref[...], g_ref[...], mod_ref[3:4, :], mod_ref[4:5, :]).astype(BF16)
        h_ref[...] = h
        acc_ref[...] = ffn(h)

    @pl.when(jnp.logical_and(j > 0, j < nj - 1))
    def _():
        acc_ref[...] += ffn(h_ref[...])

    @pl.when(j == nj - 1)
    def _():
        x2 = x_ref[...] + mod_ref[5:6, :] * (acc_ref[...] + ffn(h_ref[...]))
        o_ref[...] = _rms(x2, gf_ref[...])


def _mlp_call(x1, mod, mod_row, g_norm, w1, w2, g_final, tm, tf):
    ntok, d = x1.shape
    dff = w1.shape[1]
    assert dff // tf >= 2
    return pl.pallas_call(
        _mlp_kernel,
        grid=(ntok // tm, dff // tf),
        in_specs=[pl.BlockSpec((tm, d), lambda i, j: (i, 0)),
                  pl.BlockSpec((None, N_MOD, d), lambda i, j: (mod_row(i), 0, 0)),
                  _resident((1, d)),
                  pl.BlockSpec((d, tf), lambda i, j: (0, j)),
                  pl.BlockSpec((tf, d), lambda i, j: (j, 0)),
                  _resident((1, d))],
        out_specs=pl.BlockSpec((tm, d), lambda i, j: (i, 0)),
        out_shape=jax.ShapeDtypeStruct((ntok, d), F32),
        scratch_shapes=[pltpu.VMEM((tm, d), BF16), pltpu.VMEM((tm, d), F32)],
        compiler_params=_params(("arbitrary", "arbitrary")),
        name="mlp",
    )(x1, mod, g_norm, w1, w2, g_final)


def _pos_tables(n_tokens, d):
    rows = n_tokens // GRID_W
    quarter = d // 4
    omega = 1.0 / (POS_BASE ** (jnp.arange(quarter, dtype=F32) / quarter))
    r = jnp.arange(rows, dtype=F32)[:, None] * omega
    col = jnp.arange(GRID_W, dtype=F32)[:, None] * omega
    e_r = jnp.concatenate([jnp.sin(r), jnp.cos(r)], axis=-1)
    e_c = jnp.concatenate([jnp.sin(col), jnp.cos(col)], axis=-1)
    return e_r, e_c


def _cmul(ar, ai, br, bi):
    return ar * br - ai * bi, ar * bi + ai * br


def _ssm_prepare(a_re, a_im, log_dt, b_re, b_im, c_re, c_im):
    ndir, g, p = a_re.shape
    ci = b_re.shape[-1]
    kt = T1 * ci
    hi = lax.Precision.HIGHEST
    dt = jnp.exp(log_dt)[..., None]
    lr, ang = a_re * dt, a_im * dt

    def power(j, lr_, ang_):
        m = jnp.exp(j * lr_)
        return m * jnp.cos(j * ang_), m * jnp.sin(j * ang_)

    ab_re, ab_im = power(1.0, lr, ang)
    den = a_re * a_re + a_im * a_im
    f_re, f_im = _cmul(ab_re - 1.0, ab_im, a_re / den, -a_im / den)
    bb_re, bb_im = _cmul(f_re[..., None], f_im[..., None], b_re, b_im)

    tt = jnp.arange(T1, dtype=F32)
    j_in = jnp.stack([T1 - 1 - tt, tt])[:, None, None, :]
    pl_re, pl_im = power(j_in, lr[..., None], ang[..., None])
    ba_re, ba_im = _cmul(jnp.repeat(pl_re, ci, axis=-1), jnp.repeat(pl_im, ci, axis=-1),
                         jnp.tile(bb_re, (1, 1, 1, T1)), jnp.tile(bb_im, (1, 1, 1, T1)))
    wend_t = jnp.concatenate([ba_re[0], ba_im[0], ba_re[1], ba_im[1]], axis=1)

    kk = (jnp.einsum('dgcp,dgpl->dgcl', c_re, ba_re, precision=hi)
          - jnp.einsum('dgcp,dgpl->dgcl', c_im, ba_im, precision=hi))
    kf, kb = kk[0], kk[1]
    mid = (T1 - 1) * ci
    r = jnp.concatenate([kf[..., :mid], kf[..., mid:] + kb[..., :ci], kb[..., ci:]], axis=-1)
    toep_t = jnp.stack([r[..., (T1 - 1 - t) * ci:(2 * T1 - 1 - t) * ci] for t in range(T1)], axis=1)
    toep_t = toep_t.reshape(g, kt, kt)
    w1t = jnp.concatenate([toep_t, wend_t], axis=1).astype(BF16)

    dup = lambda w: jnp.concatenate([w, w], axis=-1)
    j_out = jnp.stack([tt + 1.0, T1 - tt])[:, None, :, None]
    po_re, po_im = power(j_out, dup(lr)[:, :, None, :], dup(ang)[:, :, None, :])
    ca_re, ca_im = _cmul(dup(c_re)[:, :, None], dup(c_im)[:, :, None], po_re[:, :, :, None], po_im[:, :, :, None])
    ca_re, ca_im = ca_re.reshape(ndir, g, kt, 2 * p), ca_im.reshape(ndir, g, kt, 2 * p)
    wout2 = jnp.concatenate([ca_re[0], -ca_im[0], ca_re[1], -ca_im[1]], axis=-1).astype(BF16)

    pt_re, pt_im = power(float(T1), lr, ang)
    pw = jnp.stack([pt_re[0], pt_im[0], pt_re[1], pt_im[1]], axis=0)
    pw = jnp.transpose(pw.reshape(4, g // 2, 2 * p), (1, 0, 2))
    return w1t, wout2, pw


def _trunk(x, pos_tabs, mod, mod_row, h0p, lw, nsb, emit_state, tm_a, tm_b, casts=()):
    bsz, n, d = x.shape
    x2 = x.reshape(bsz * n, d)
    pa, xb, *cast_out = _mixa_call(x2, pos_tabs, mod, mod_row(tm_a), n // tm_a, lw["g_norm_mix"], lw["w_in"],
                                   lw["db"], lw["g_sgu"], lw["w_s"], lw["b_s"], lw["w_pa"], tm_a, casts)
    res = _s5_call(xb, lw["w1t"], lw["wout2"], lw["pw"], lw["d_skip"], h0p, n, nsb, emit_state)
    x1 = _mixb_call(x2, pos_tabs, mod, mod_row(tm_b), n // tm_b, lw["g_norm_mix"], res[0], pa, lw["w_in"],
                    lw["w_glu"], lw["b_glu"], lw["w_pb"], lw["w_out"], tm_b)
    return x1, (res[1] if emit_state else None), cast_out


def kernel(x_prompt, x_sample, state_ssm_re, state_ssm_im, c, c_ctx, w_ada, b_ada, g_norm_mix, w_in,
           g_sgu, w_spatial, b_spatial, ssm_a_re, ssm_a_im, ssm_log_dt, ssm_b_re, ssm_b_im, ssm_c_re,
           ssm_c_im, ssm_d, w_glu, b_glu, w_proj_a, w_proj_b, w_out, g_norm_mlp, w_mlp_in, w_mlp_out,
           g_final):
    bp, sp, d = x_prompt.shape
    bs, ss, _ = x_sample.shape
    depth = w_in.shape[0]
    assert depth == 1, "positional embedding and final norm are fused assuming a single trunk layer"
    da = w_proj_a.shape[1]
    db = w_proj_b.shape[1]
    g, p = ssm_a_re.shape[2], ssm_a_re.shape[3]
    tm_a, tm_b, tm_mlp = 512, 512, 512
    tf = min(2048, w_mlp_in.shape[2] // 2)

    rows = -(-(bs + 1) // 8) * 8
    cc = jnp.concatenate([c, c_ctx[None], jnp.zeros((rows - bs - 1, d), F32)], axis=0)
    pos_tabs = _pos_tables(ss, d)
    l = 0
    mod = _mod_call(cc, w_ada[l], b_ada[l]).reshape(rows, N_MOD, d)
    w1t, wout2, pw = _ssm_prepare(ssm_a_re[l], ssm_a_im[l], ssm_log_dt[l], ssm_b_re[l],
                                  ssm_b_im[l], ssm_c_re[l], ssm_c_im[l])
    wi = w_in[l]
    lw = dict(
        g_norm_mix=g_norm_mix[l].reshape(1, d),
        w_in=wi.astype(BF16), db=db,
        g_sgu=g_sgu[l].reshape(1, da),
        w_s=w_spatial[l].astype(BF16),
        b_s=b_spatial[l][:, :, None],
        w_pa=w_proj_a[l].astype(BF16),
        w_glu=w_glu[l].astype(BF16),
        b_glu=b_glu[l].reshape(1, db),
        w_pb=w_proj_b[l].astype(BF16),
        w_out=w_out[l].astype(BF16),
        w1t=w1t, wout2=wout2, pw=pw, d_skip=ssm_d[l].reshape(1, db))

    def pairs(st):
        return jnp.transpose(st.reshape(bs, 2, g // 2, 2 * p), (1, 2, 0, 3))
    sre, sim = pairs(state_ssm_re[:, l]), pairs(state_ssm_im[:, l])
    h0p = jnp.stack([sre[0], sim[0], sre[1], sim[1]], axis=1)[None]

    ctx_row = lambda tm: (lambda i: bs)
    seq_row = lambda tm: (lambda i: i // (ss // tm))
    xp1, e, _ = _trunk(x_prompt, None, mod, ctx_row, None, lw, bp, True, tm_a, tm_b)
    xs1, _, (w1, w2) = _trunk(x_sample, pos_tabs, mod, seq_row, h0p, lw, bs, False, tm_a, tm_b,
                              casts=(w_mlp_in[l], w_mlp_out[l]))

    gf = g_final.reshape(1, d)
    gm = g_norm_mlp[l].reshape(1, d)
    xp = _mlp_call(xp1, mod, ctx_row(tm_mlp), gm, w1, w2, gf, tm_mlp, tf).reshape(bp, sp, d)
    xs = _mlp_call(xs1, mod, seq_row(tm_mlp), gm, w1, w2, gf, tm_mlp, tf).reshape(bs, ss, d)

    ev = e.reshape(g // 2, 2, 2, bp, 2, p)
    fin = jnp.transpose(ev, (2, 3, 1, 0, 4, 5)).reshape(2, bp, 2, g, p)
    return xp, xs, fin[0][:, None], fin[1][:, None]
```

```python
import functools
import math

import jax
import jax.numpy as jnp
from jax import lax
from jax.experimental import pallas as pl
from jax.experimental.pallas import tpu as pltpu

F32 = jnp.float32
BF16 = jnp.bfloat16

EPS = 1e-6
POS_BASE = 10000.0
GRID_W = 64
N_MOD = 6
T1 = 16
LANES = 128
SUBLANES = 8
VMEM_LIMIT = 63 * 1024 * 1024


def _gelu(x):
    return 0.5 * x * (1.0 + jnp.tanh(0.7978845608028654 * (x + 0.044715 * (x * x * x))))


def _sigmoid(x):
    return 1.0 / (1.0 + jnp.exp(-x))


def _rms(x, g):
    return x * lax.rsqrt(jnp.mean(x * x, axis=-1, keepdims=True) + EPS) * g


def _modnorm(x, g, shift, scale):
    return _rms(x, g) * (1.0 + scale) + shift


def _resident(shape):
    nd = len(shape)
    return pl.BlockSpec(shape, lambda *_: (0,) * nd, pipeline_mode=pl.Buffered(1))


def _params(sem):
    return pltpu.CompilerParams(dimension_semantics=sem, vmem_limit_bytes=VMEM_LIMIT)


def _add_pos(x, er_ref, ec_ref, tile_in_seq):
    tm, d = x.shape
    half = d // 2
    r0 = tile_in_seq * (tm // GRID_W)
    ec = ec_ref[...]
    parts = []
    for k in range(tm // GRID_W):
        er = jnp.broadcast_to(er_ref[pl.ds(r0 + k, 1), :], (GRID_W, half))
        parts.append(jnp.concatenate([er, ec], axis=1))
    return x + jnp.concatenate(parts, axis=0)


def _mod_kernel(c_ref, w_ref, b_ref, o_ref):
    cc = c_ref[...]
    s = cc * _sigmoid(cc)
    o_ref[...] = jnp.dot(s, w_ref[...], preferred_element_type=F32) + b_ref[...]


def _mod_call(cc, w_ada, b_ada):
    rows, d = cc.shape
    n = w_ada.shape[1]
    tn = n // (2 * N_MOD)
    assert n % tn == 0 and tn % LANES == 0
    return pl.pallas_call(
        _mod_kernel,
        grid=(n // tn,),
        in_specs=[pl.BlockSpec((rows, d), lambda j: (0, 0)),
                  pl.BlockSpec((d, tn), lambda j: (0, j)),
                  pl.BlockSpec((1, tn), lambda j: (0, j))],
        out_specs=pl.BlockSpec((rows, tn), lambda j: (0, j)),
        out_shape=jax.ShapeDtypeStruct((rows, n), F32),
        compiler_params=_params(("arbitrary",)),
        name="mod",
    )(cc, w_ada, b_ada.reshape(1, n))


def _mixa_kernel(*refs, pos_blocks, chunk, da, n_cast):
    if pos_blocks:
        x_ref, er_ref, ec_ref = refs[:3]
        x = _add_pos(x_ref[...], er_ref, ec_ref, pl.program_id(0) % pos_blocks)
        refs = refs[3:]
    else:
        x_ref = refs[0]
        x = x_ref[...]
        refs = refs[1:]
    mod_ref, g_ref, w_ref, gs_ref, ws_ref, bs_ref, wpa_ref = refs[:7]
    cast_in, (o_ref, xb_ref) = refs[7:7 + n_cast], refs[7 + n_cast:9 + n_cast]
    cast_out, (ya_ref, xs_ref) = refs[9 + n_cast:9 + 2 * n_cast], refs[9 + 2 * n_cast:]
    for ci_ref, co_ref in zip(cast_in, cast_out):
        co_ref[...] = ci_ref[...].astype(co_ref.dtype)
    tm, d = x.shape
    h = _modnorm(x, g_ref[...], mod_ref[0:1, :], mod_ref[1:2, :]).astype(BF16)
    z = jnp.dot(h, w_ref[...], preferred_element_type=F32)
    db = xs_ref.shape[0] * LANES
    for cb in range(xs_ref.shape[0]):
        c0 = 2 * da + cb * LANES
        xs_ref[cb] = z[:, c0:c0 + LANES]
        for t in range(T1):
            xb_ref[t, :, cb * LANES:(cb + 1) * LANES] = (
                xs_ref[cb, pl.ds(t, tm // T1, stride=T1), :].astype(xb_ref.dtype))
    u = _gelu(z[:, :da])
    v = _rms(_gelu(z[:, da:2 * da]), gs_ref[...]).astype(BF16)
    ng = ws_ref.shape[0]
    cg = da // ng
    for ck in range(tm // chunk):
        r0 = ck * chunk
        for gi in range(ng):
            c0 = gi * cg
            s = jnp.dot(ws_ref[gi], v[r0:r0 + chunk, c0:c0 + cg], preferred_element_type=F32)
            ya_ref[r0:r0 + chunk, c0:c0 + cg] = (u[r0:r0 + chunk, c0:c0 + cg] * (s + bs_ref[gi])).astype(BF16)
    pa = jnp.dot(ya_ref[...], wpa_ref[...], preferred_element_type=F32)
    o_ref[...] = (_sigmoid(z[:, 2 * da + db:2 * da + db + d]) * pa).astype(o_ref.dtype)


def _mixa_call(x2, pos_tabs, mod, mod_row, pos_blocks, g_norm, w_in, db, g_sgu, w_s, b_s, w_pa, tm, casts=()):
    ntok, d = x2.shape
    steps = ntok // tm
    da = w_pa.shape[0]
    chunk = w_s.shape[1]
    in_specs = [pl.BlockSpec((tm, d), lambda i: (i, 0))]
    args = [x2]
    if pos_tabs is not None:
        in_specs += [_resident(pos_tabs[0].shape), _resident(pos_tabs[1].shape)]
        args += list(pos_tabs)
    in_specs += [pl.BlockSpec((None, N_MOD, d), lambda i: (mod_row(i), 0, 0)),
                 _resident((1, d)), _resident((d, 2 * da + db + d)), _resident((1, da)),
                 _resident(w_s.shape), _resident(b_s.shape), _resident(w_pa.shape)]
    args += [mod, g_norm, w_in, g_sgu, w_s, b_s, w_pa] + list(casts)
    cast_specs = [pl.BlockSpec((w.shape[0] // steps, w.shape[1]), lambda i: (i, 0)) for w in casts]
    assert all(w.shape[0] % (steps * 2 * SUBLANES) == 0 for w in casts)
    return pl.pallas_call(
        functools.partial(_mixa_kernel, pos_blocks=pos_blocks if pos_tabs is not None else 0,
                          chunk=chunk, da=da, n_cast=len(casts)),
        grid=(steps,),
        in_specs=in_specs + cast_specs,
        out_specs=[pl.BlockSpec((tm, d), lambda i: (i, 0)),
                   pl.BlockSpec((T1, tm // T1, db), lambda i: (0, i, 0))] + cast_specs,
        out_shape=[jax.ShapeDtypeStruct((ntok, d), BF16),
                   jax.ShapeDtypeStruct((T1, ntok // T1, db), BF16)]
        + [jax.ShapeDtypeStruct(w.shape, BF16) for w in casts],
        scratch_shapes=[pltpu.VMEM((tm, da), BF16), pltpu.VMEM((db // LANES, tm, LANES), F32)],
        compiler_params=_params(("arbitrary",)),
        name="mixa",
    )(*args)


def _s5_kernel(*refs, nsb, n_chunks, has_h0, emit_state, p, ci):
    it = iter(refs)
    x_ref, w1_ref, wout_ref, pw_ref, d_ref = next(it), next(it), next(it), next(it), next(it)
    h0_ref = next(it) if has_h0 else None
    y_ref = next(it)
    e_ref = next(it) if emit_state else None
    ut_ref, s_ref, hin_ref = next(it), next(it), next(it)

    ngl, kt, cols = ut_ref.shape
    npair = ngl // 2
    pitch = s_ref.shape[2] // nsb

    for t in range(T1):
        st = x_ref[t].astype(F32).T
        for gl in range(ngl):
            ut_ref[gl, t * ci:(t + 1) * ci, :] = st[gl * ci:(gl + 1) * ci, :]

    def local(pr, carry):
        r = []
        for hlf in range(2):
            gl = 2 * pr + hlf
            r1 = jnp.dot(w1_ref[gl], ut_ref[gl].astype(BF16), preferred_element_type=F32)
            ut_ref[gl] = r1[:kt]
            r.append(r1)
        for q in range(4):
            lo = kt + q * p
            sq = jnp.concatenate([r[0][lo:lo + p], r[1][lo:lo + p]], axis=0).T
            for b in range(nsb):
                s_ref[pr, q, b * pitch:b * pitch + n_chunks, :] = sq[b * n_chunks:(b + 1) * n_chunks]
        return carry

    lax.fori_loop(0, npair, local, 0, unroll=True)

    def rows(n):
        return pl.ds(n, nsb, stride=pitch) if nsb > 1 else pl.ds(n, 1)

    def step(n, hs):
        rf, rb = rows(n), rows(n_chunks - 1 - n)
        out = []
        for pr in range(npair):
            fr, fi, br, bi = hs[4 * pr:4 * pr + 4]
            afr, afi = pw_ref[pr, 0:1, :], pw_ref[pr, 1:2, :]
            abr, abi = pw_ref[pr, 2:3, :], pw_ref[pr, 3:4, :]
            sfr, sfi = s_ref[pr, 0, rf, :], s_ref[pr, 1, rf, :]
            sbr, sbi = s_ref[pr, 2, rb, :], s_ref[pr, 3, rb, :]
            hin_ref[pr, 0, rf, :] = fr
            hin_ref[pr, 1, rf, :] = fi
            hin_ref[pr, 2, rb, :] = br
            hin_ref[pr, 3, rb, :] = bi
            out += [afr * fr - afi * fi + sfr, afr * fi + afi * fr + sfi,
                    abr * br - abi * bi + sbr, abr * bi + abi * br + sbi]
        return tuple(out)

    if has_h0:
        init = tuple(h0_ref[pr, q] for pr in range(npair) for q in range(4))
    else:
        init = tuple(jnp.zeros((nsb, LANES), F32) for _ in range(4 * npair))
    fin = lax.fori_loop(0, n_chunks, step, init, unroll=4)
    if emit_state:
        for pr in range(npair):
            for q in range(4):
                e_ref[pr, q] = fin[4 * pr + q]

    first = lax.broadcasted_iota(jnp.int32, (cols, 8 * p), 1) % LANES < p

    def inter(pr, carry):
        hin = jnp.concatenate(
            [jnp.concatenate([hin_ref[pr, q, b * pitch:b * pitch + n_chunks, :] for b in range(nsb)], axis=0)
             for q in range(4)], axis=1)
        for hlf, hg in enumerate((jnp.where(first, hin, 0.0), jnp.where(first, 0.0, hin))):
            gl = 2 * pr + hlf
            ut_ref[gl] += lax.dot_general(wout_ref[gl], hg.astype(BF16), (((1,), (1,)), ((), ())),
                                          preferred_element_type=F32)
        return carry

    lax.fori_loop(0, npair, inter, 0, unroll=True)

    for t in range(T1):
        yt = jnp.concatenate([ut_ref[gl, t * ci:(t + 1) * ci, :] for gl in range(ngl)], axis=0)
        y_ref[t] = _gelu(yt.T + d_ref[...] * x_ref[t].astype(F32)).astype(y_ref.dtype)


def _s5_call(xb, w1t, wout2, pw, d_skip, h0p, n, nsb, emit_state):
    _, ncol, db = xb.shape
    g, kt = w1t.shape[0], w1t.shape[2]
    p = (w1t.shape[1] - kt) // 4
    ci = db // g
    ngl = LANES // ci
    npair = ngl // 2
    assert 2 * p == LANES and ngl % 2 == 0
    n_chunks = n // T1
    cols = nsb * n_chunks
    has_h0 = h0p is not None
    in_specs = [pl.BlockSpec((T1, cols, LANES), lambda i, j: (0, j, i)),
                pl.BlockSpec((ngl,) + w1t.shape[1:], lambda i, j: (i, 0, 0)),
                pl.BlockSpec((ngl,) + wout2.shape[1:], lambda i, j: (i, 0, 0)),
                pl.BlockSpec((npair,) + pw.shape[1:], lambda i, j: (i, 0, 0)),
                pl.BlockSpec((1, LANES), lambda i, j: (0, i))]
    args = [xb, w1t, wout2, pw, d_skip]
    if has_h0:
        in_specs.append(pl.BlockSpec((None, npair, 4, nsb, LANES), lambda i, j: (j, i, 0, 0, 0)))
        args.append(h0p)
    out_specs = [pl.BlockSpec((T1, cols, LANES), lambda i, j: (0, j, i))]
    out_shape = [jax.ShapeDtypeStruct((T1, ncol, db), xb.dtype)]
    if emit_state:
        out_specs.append(pl.BlockSpec((None, npair, 4, nsb, LANES), lambda i, j: (j, i, 0, 0, 0)))
        out_shape.append(jax.ShapeDtypeStruct((ncol // cols, g // 2, 4, nsb, LANES), F32))
    return pl.pallas_call(
        functools.partial(_s5_kernel, nsb=nsb, n_chunks=n_chunks, has_h0=has_h0, emit_state=emit_state,
                          p=p, ci=ci),
        grid=(g // ngl, ncol // cols),
        in_specs=in_specs,
        out_specs=out_specs,
        out_shape=out_shape,
        scratch_shapes=[pltpu.VMEM((ngl, kt, cols), F32)]
        + [pltpu.VMEM((npair, 4, nsb * (n_chunks + SUBLANES), LANES), F32)] * 2,
        compiler_params=_params(("arbitrary", "arbitrary")),
        name="s5",
    )(*args)


def _mixb_kernel(*refs, pos_blocks, n_gb):
    if pos_blocks:
        x_ref, er_ref, ec_ref, mod_ref, g_ref, y_ref, pa_ref = refs[:7]
        x = _add_pos(x_ref[...], er_ref, ec_ref, pl.program_id(0) % pos_blocks)
        refs = refs[7:]
    else:
        x_ref, mod_ref, g_ref, y_ref, pa_ref = refs[:5]
        x = x_ref[...]
        refs = refs[5:]
    wgb_refs = refs[:n_gb]
    wglu_ref, bglu_ref, wpb_ref, wout_ref, o_ref, ys_ref = refs[n_gb:]
    tm = x.shape[0]
    h = _modnorm(x, g_ref[...], mod_ref[0:1, :], mod_ref[1:2, :]).astype(BF16)
    for cb in range(ys_ref.shape[0]):
        for t in range(T1):
            ys_ref[cb, pl.ds(t, tm // T1, stride=T1), :] = y_ref[t, :, cb * LANES:(cb + 1) * LANES].astype(F32)
    y = jnp.concatenate([ys_ref[cb] for cb in range(ys_ref.shape[0])], axis=1)
    gate = _sigmoid(jnp.dot(y.astype(BF16), wglu_ref[...], preferred_element_type=F32) + bglu_ref[...])
    yb = (y * gate).astype(BF16)
    pb = jnp.dot(yb, wpb_ref[...], preferred_element_type=F32)
    glb = jnp.concatenate([jnp.dot(h, w[...], preferred_element_type=F32) for w in wgb_refs], axis=1)
    pb = _sigmoid(glb) * pb
    m = (pa_ref[...].astype(F32) + pb).astype(BF16)
    o_ref[...] = x + mod_ref[2:3, :] * jnp.dot(m, wout_ref[...], preferred_element_type=F32)


def _mixb_call(x2, pos_tabs, mod, mod_row, pos_blocks, g_norm, y, pa, w_in, w_glu, b_glu, w_pb, w_out, tm):
    ntok, d = x2.shape
    db = y.shape[2]
    gb0 = w_in.shape[1] - d
    bw = math.gcd(gb0, d)
    n_gb = d // bw
    in_specs = [pl.BlockSpec((tm, d), lambda i: (i, 0))]
    args = [x2]
    if pos_tabs is not None:
        in_specs += [_resident(pos_tabs[0].shape), _resident(pos_tabs[1].shape)]
        args += list(pos_tabs)
    in_specs += [pl.BlockSpec((None, N_MOD, d), lambda i: (mod_row(i), 0, 0)),
                 _resident((1, d)),
                 pl.BlockSpec((T1, tm // T1, db), lambda i: (0, i, 0)),
                 pl.BlockSpec((tm, d), lambda i: (i, 0)),
                 *[pl.BlockSpec((d, bw), functools.partial(lambda k, i: (0, gb0 // bw + k), k),
                                pipeline_mode=pl.Buffered(1)) for k in range(n_gb)],
                 _resident(w_glu.shape), _resident((1, db)),
                 _resident(w_pb.shape), _resident(w_out.shape)]
    args += [mod, g_norm, y, pa] + [w_in] * n_gb + [w_glu, b_glu, w_pb, w_out]
    return pl.pallas_call(
        functools.partial(_mixb_kernel, pos_blocks=pos_blocks if pos_tabs is not None else 0, n_gb=n_gb),
        grid=(ntok // tm,),
        in_specs=in_specs,
        out_specs=pl.BlockSpec((tm, d), lambda i: (i, 0)),
        out_shape=jax.ShapeDtypeStruct((ntok, d), F32),
        scratch_shapes=[pltpu.VMEM((db // LANES, tm, LANES), F32)],
        compiler_params=_params(("arbitrary",)),
        name="mixb",
    )(*args)


def _mlp_kernel(x_ref, mod_ref, g_ref, w1_ref, w2_ref, gf_ref, o_ref, h_ref, acc_ref):
    j = pl.program_id(1)
    nj = pl.num_programs(1)

    def ffn(h):
        hid = jnp.dot(h, w1_ref[...], preferred_element_type=F32)
        hid = jnp.square(jnp.maximum(hid, 0.0)).astype(BF16)
        return jnp.dot(hid, w2_ref[...], preferred_element_type=F32)

    @pl.when(j == 0)
    def _():
        h = _modnorm(x_ref[...], g_ref[...], mod_ref[3:4, :], mod_ref[4:5, :]).astype(BF16)
        h_ref[...] = h
        acc_ref[...] = ffn(h)

    @pl.when(jnp.logical_and(j > 0, j < nj - 1))
    def _():
        acc_ref[...] += ffn(h_ref[...])

    @pl.when(j == nj - 1)
    def _():
        x2 = x_ref[...] + mod_ref[5:6, :] * (acc_ref[...] + ffn(h_ref[...]))
        o_ref[...] = _rms(x2, gf_ref[...])


def _mlp_call(x1, mod, mod_row, g_norm, w1, w2, g_final, tm, tf):
    ntok, d = x1.shape
    dff = w1.shape[1]
    assert dff // tf >= 2
    return pl.pallas_call(
        _mlp_kernel,
        grid=(ntok // tm, dff // tf),
        in_specs=[pl.BlockSpec((tm, d), lambda i, j: (i, 0)),
                  pl.BlockSpec((None, N_MOD, d), lambda i, j: (mod_row(i), 0, 0)),
                  _resident((1, d)),
                  pl.BlockSpec((d, tf), lambda i, j: (0, j)),
                  pl.BlockSpec((tf, d), lambda i, j: (j, 0)),
                  _resident((1, d))],
        out_specs=pl.BlockSpec((tm, d), lambda i, j: (i, 0)),
        out_shape=jax.ShapeDtypeStruct((ntok, d), F32),
        scratch_shapes=[pltpu.VMEM((tm, d), BF16), pltpu.VMEM((tm, d), F32)],
        compiler_params=_params(("arbitrary", "arbitrary")),
        name="mlp",
    )(x1, mod, g_norm, w1, w2, g_final)


def _pos_tables(n_tokens, d):
    rows = n_tokens // GRID_W
    quarter = d // 4
    omega = 1.0 / (POS_BASE ** (jnp.arange(quarter, dtype=F32) / quarter))
    r = jnp.arange(rows, dtype=F32)[:, None] * omega
    col = jnp.arange(GRID_W, dtype=F32)[:, None] * omega
    e_r = jnp.concatenate([jnp.sin(r), jnp.cos(r)], axis=-1)
    e_c = jnp.concatenate([jnp.sin(col), jnp.cos(col)], axis=-1)
    return e_r, e_c


def _cmul(ar, ai, br, bi):
    return ar * br - ai * bi, ar * bi + ai * br


def _ssm_prepare(a_re, a_im, log_dt, b_re, b_im, c_re, c_im):
    ndir, g, p = a_re.shape
    ci = b_re.shape[-1]
    kt = T1 * ci
    hi = lax.Precision.HIGHEST
    dt = jnp.exp(log_dt)[..., None]
    lr, ang = a_re * dt, a_im * dt

    def power(j, lr_, ang_):
        m = jnp.exp(j * lr_)
        return m * jnp.cos(j * ang_), m * jnp.sin(j * ang_)

    ab_re, ab_im = power(1.0, lr, ang)
    den = a_re * a_re + a_im * a_im
    f_re, f_im = _cmul(ab_re - 1.0, ab_im, a_re / den, -a_im / den)
    bb_re, bb_im = _cmul(f_re[..., None], f_im[..., None], b_re, b_im)

    tt = jnp.arange(T1, dtype=F32)
    j_in = jnp.stack([T1 - 1 - tt, tt])[:, None, None, :]
    pl_re, pl_im = power(j_in, lr[..., None], ang[..., None])
    ba_re, ba_im = _cmul(jnp.repeat(pl_re, ci, axis=-1), jnp.repeat(pl_im, ci, axis=-1),
                         jnp.tile(bb_re, (1, 1, 1, T1)), jnp.tile(bb_im, (1, 1, 1, T1)))
    wend_t = jnp.concatenate([ba_re[0], ba_im[0], ba_re[1], ba_im[1]], axis=1)

    kk = (jnp.einsum('dgcp,dgpl->dgcl', c_re, ba_re, precision=hi)
          - jnp.einsum('dgcp,dgpl->dgcl', c_im, ba_im, precision=hi))
    kf, kb = kk[0], kk[1]
    mid = (T1 - 1) * ci
    r = jnp.concatenate([kf[..., :mid], kf[..., mid:] + kb[..., :ci], kb[..., ci:]], axis=-1)
    toep_t = jnp.stack([r[..., (T1 - 1 - t) * ci:(2 * T1 - 1 - t) * ci] for t in range(T1)], axis=1)
    toep_t = toep_t.reshape(g, kt, kt)
    w1t = jnp.concatenate([toep_t, wend_t], axis=1).astype(BF16)

    dup = lambda w: jnp.concatenate([w, w], axis=-1)

    def wout(dd, j):
        po_re, po_im = power(j[None, :, None], dup(lr[dd])[:, None, :], dup(ang[dd])[:, None, :])
        ca_re, ca_im = _cmul(dup(c_re[dd])[:, None], dup(c_im[dd])[:, None], po_re[:, :, None], po_im[:, :, None])
        return ca_re.reshape(g, kt, 2 * p), (-ca_im).reshape(g, kt, 2 * p)
    wout2 = jnp.concatenate(wout(0, tt + 1.0) + wout(1, T1 - tt), axis=-1).astype(BF16)

    pt_re, pt_im = power(float(T1), lr, ang)
    pw = jnp.stack([pt_re[0], pt_im[0], pt_re[1], pt_im[1]], axis=0)
    pw = jnp.transpose(pw.reshape(4, g // 2, 2 * p), (1, 0, 2))
    return w1t, wout2, pw


def _trunk(x, pos_tabs, mod, mod_row, h0p, lw, nsb, emit_state, tm_a, tm_b, casts):
    bsz, n, d = x.shape
    x2 = x.reshape(bsz * n, d)
    pa, xb, *cast_out = _mixa_call(x2, pos_tabs, mod, mod_row(tm_a), n // tm_a, lw["g_norm_mix"], lw["w_in"],
                                   lw["db"], lw["g_sgu"], lw["w_s"], lw["b_s"], lw["w_pa"], tm_a,
                                   tuple(casts.values()))
    lw = dict(lw, **dict(zip(casts, cast_out)))
    res = _s5_call(xb, lw["w1t"], lw["wout2"], lw["pw"], lw["d_skip"], h0p, n, nsb, emit_state)
    x1 = _mixb_call(x2, pos_tabs, mod, mod_row(tm_b), n // tm_b, lw["g_norm_mix"], res[0], pa, lw["w_in"],
                    lw["w_glu"], lw["b_glu"], lw["w_pb"], lw["w_out"], tm_b)
    return x1, (res[1] if emit_state else None), lw


def kernel(x_prompt, x_sample, state_ssm_re, state_ssm_im, c, c_ctx, w_ada, b_ada, g_norm_mix, w_in,
           g_sgu, w_spatial, b_spatial, ssm_a_re, ssm_a_im, ssm_log_dt, ssm_b_re, ssm_b_im, ssm_c_re,
           ssm_c_im, ssm_d, w_glu, b_glu, w_proj_a, w_proj_b, w_out, g_norm_mlp, w_mlp_in, w_mlp_out,
           g_final):
    bp, sp, d = x_prompt.shape
    bs, ss, _ = x_sample.shape
    depth = w_in.shape[0]
    assert depth == 1, "positional embedding and final norm are fused assuming a single trunk layer"
    da = w_proj_a.shape[1]
    db = w_proj_b.shape[1]
    g, p = ssm_a_re.shape[2], ssm_a_re.shape[3]
    tm_a, tm_b, tm_mlp = 512, 512, 512
    tf = min(2048, w_mlp_in.shape[2] // 2)

    rows = -(-(bs + 1) // 8) * 8
    cc = jnp.concatenate([c, c_ctx[None], jnp.zeros((rows - bs - 1, d), F32)], axis=0)
    pos_tabs = _pos_tables(ss, d)
    l = 0
    mod = _mod_call(cc, w_ada[l], b_ada[l]).reshape(rows, N_MOD, d)
    w1t, wout2, pw = _ssm_prepare(ssm_a_re[l], ssm_a_im[l], ssm_log_dt[l], ssm_b_re[l],
                                  ssm_b_im[l], ssm_c_re[l], ssm_c_im[l])
    wi = w_in[l]
    lw = dict(
        g_norm_mix=g_norm_mix[l].reshape(1, d),
        w_in=wi.astype(BF16), db=db,
        g_sgu=g_sgu[l].reshape(1, da),
        w_s=w_spatial[l].astype(BF16),
        b_s=b_spatial[l][:, :, None],
        w_pa=w_proj_a[l].astype(BF16),
        b_glu=b_glu[l].reshape(1, db),
        w1t=w1t, wout2=wout2, pw=pw, d_skip=ssm_d[l].reshape(1, db))

    def pairs(st):
        return jnp.transpose(st.reshape(bs, 2, g // 2, 2 * p), (1, 2, 0, 3))
    sre, sim = pairs(state_ssm_re[:, l]), pairs(state_ssm_im[:, l])
    h0p = jnp.stack([sre[0], sim[0], sre[1], sim[1]], axis=1)[None]

    ctx_row = lambda tm: (lambda i: bs)
    seq_row = lambda tm: (lambda i: i // (ss // tm))
    xp1, e, lw = _trunk(x_prompt, None, mod, ctx_row, None, lw, bp, True, tm_a, tm_b,
                        casts=dict(w_glu=w_glu[l], w_pb=w_proj_b[l], w_out=w_out[l]))
    xs1, _, lw = _trunk(x_sample, pos_tabs, mod, seq_row, h0p, lw, bs, False, tm_a, tm_b,
                        casts=dict(w1=w_mlp_in[l], w2=w_mlp_out[l]))
    w1, w2 = lw["w1"], lw["w2"]

    gf = g_final.reshape(1, d)
    gm = g_norm_mlp[l].reshape(1, d)
    xp = _mlp_call(xp1, mod, ctx_row(tm_mlp), gm, w1, w2, gf, tm_mlp, tf).reshape(bp, sp, d)
    xs = _mlp_call(xs1, mod, seq_row(tm_mlp), gm, w1, w2, gf, tm_mlp, tf).reshape(bs, ss, d)

    ev = e.reshape(g // 2, 2, 2, bp, 2, p)
    fin = jnp.transpose(ev, (2, 3, 1, 0, 4, 5)).reshape(2, bp, 2, g, p)
    return xp, xs, fin[0][:, None], fin[1][:, None]
```

```python
import functools
import math

import jax
import jax.numpy as jnp
from jax import lax
from jax.experimental import pallas as pl
from jax.experimental.pallas import tpu as pltpu

F32 = jnp.float32
BF16 = jnp.bfloat16

EPS = 1e-6
POS_BASE = 10000.0
GRID_W = 64
N_MOD = 6
T1 = 16
LANES = 128
SUBLANES = 8
VMEM_LIMIT = 63 * 1024 * 1024


def _gelu(x):
    return 0.5 * x * (1.0 + jnp.tanh(0.7978845608028654 * (x + 0.044715 * (x * x * x))))


def _sigmoid(x):
    return 1.0 / (1.0 + jnp.exp(-x))


def _rms(x, g):
    return x * lax.rsqrt(jnp.mean(x * x, axis=-1, keepdims=True) + EPS) * g


def _modnorm(x, g, shift, scale):
    return _rms(x, g) * (1.0 + scale) + shift


def _resident(shape):
    nd = len(shape)
    return pl.BlockSpec(shape, lambda *_: (0,) * nd, pipeline_mode=pl.Buffered(1))


def _params(sem):
    return pltpu.CompilerParams(dimension_semantics=sem, vmem_limit_bytes=VMEM_LIMIT)


def _add_pos(x, er_ref, ec_ref, tile_in_seq):
    tm, d = x.shape
    half = d // 2
    r0 = tile_in_seq * (tm // GRID_W)
    ec = ec_ref[...]
    parts = []
    for k in range(tm // GRID_W):
        er = jnp.broadcast_to(er_ref[pl.ds(r0 + k, 1), :], (GRID_W, half))
        parts.append(jnp.concatenate([er, ec], axis=1))
    return x + jnp.concatenate(parts, axis=0)


def _mod_kernel(c_ref, w_ref, b_ref, o_ref):
    cc = c_ref[...]
    s = cc * _sigmoid(cc)
    o_ref[...] = jnp.dot(s, w_ref[...], preferred_element_type=F32) + b_ref[...]


def _mod_call(cc, w_ada, b_ada):
    rows, d = cc.shape
    n = w_ada.shape[1]
    tn = n // (2 * N_MOD)
    assert n % tn == 0 and tn % LANES == 0
    return pl.pallas_call(
        _mod_kernel,
        grid=(n // tn,),
        in_specs=[pl.BlockSpec((rows, d), lambda j: (0, 0)),
                  pl.BlockSpec((d, tn), lambda j: (0, j)),
                  pl.BlockSpec((1, tn), lambda j: (0, j))],
        out_specs=pl.BlockSpec((rows, tn), lambda j: (0, j)),
        out_shape=jax.ShapeDtypeStruct((rows, n), F32),
        compiler_params=_params(("arbitrary",)),
        name="mod",
    )(cc, w_ada, b_ada.reshape(1, n))


def _mixa_kernel(*refs, pos_blocks, chunk, da, n_cast):
    if pos_blocks:
        x_ref, er_ref, ec_ref = refs[:3]
        x = _add_pos(x_ref[...], er_ref, ec_ref, pl.program_id(0) % pos_blocks)
        refs = refs[3:]
    else:
        x_ref = refs[0]
        x = x_ref[...]
        refs = refs[1:]
    mod_ref, g_ref, w_ref, gs_ref, ws_ref, bs_ref, wpa_ref = refs[:7]
    cast_in, (o_ref, xb_ref) = refs[7:7 + n_cast], refs[7 + n_cast:9 + n_cast]
    cast_out, (ya_ref, xs_ref) = refs[9 + n_cast:9 + 2 * n_cast], refs[9 + 2 * n_cast:]
    for ci_ref, co_ref in zip(cast_in, cast_out):
        co_ref[...] = ci_ref[...].astype(co_ref.dtype)
    tm, d = x.shape
    h = _modnorm(x, g_ref[...], mod_ref[0:1, :], mod_ref[1:2, :]).astype(BF16)
    z = jnp.dot(h, w_ref[...], preferred_element_type=F32)
    db = xs_ref.shape[0] * LANES
    for cb in range(xs_ref.shape[0]):
        c0 = 2 * da + cb * LANES
        xs_ref[cb] = z[:, c0:c0 + LANES]
        for t in range(T1):
            xb_ref[t, :, cb * LANES:(cb + 1) * LANES] = (
                xs_ref[cb, pl.ds(t, tm // T1, stride=T1), :].astype(xb_ref.dtype))
    u = _gelu(z[:, :da])
    v = _rms(_gelu(z[:, da:2 * da]), gs_ref[...]).astype(BF16)
    ng = ws_ref.shape[0]
    cg = da // ng
    for ck in range(tm // chunk):
        r0 = ck * chunk
        for gi in range(ng):
            c0 = gi * cg
            s = jnp.dot(ws_ref[gi], v[r0:r0 + chunk, c0:c0 + cg], preferred_element_type=F32)
            ya_ref[r0:r0 + chunk, c0:c0 + cg] = (u[r0:r0 + chunk, c0:c0 + cg] * (s + bs_ref[gi])).astype(BF16)
    pa = jnp.dot(ya_ref[...], wpa_ref[...], preferred_element_type=F32)
    o_ref[...] = (_sigmoid(z[:, 2 * da + db:2 * da + db + d]) * pa).astype(o_ref.dtype)


def _mixa_call(x2, pos_tabs, mod, mod_row, pos_blocks, g_norm, w_in, db, g_sgu, w_s, b_s, w_pa, tm, casts=()):
    ntok, d = x2.shape
    steps = ntok // tm
    da = w_pa.shape[0]
    chunk = w_s.shape[1]
    in_specs = [pl.BlockSpec((tm, d), lambda i: (i, 0))]
    args = [x2]
    if pos_tabs is not None:
        in_specs += [_resident(pos_tabs[0].shape), _resident(pos_tabs[1].shape)]
        args += list(pos_tabs)
    in_specs += [pl.BlockSpec((None, N_MOD, d), lambda i: (mod_row(i), 0, 0)),
                 _resident((1, d)), _resident((d, 2 * da + db + d)), _resident((1, da)),
                 _resident(w_s.shape), _resident(b_s.shape), _resident(w_pa.shape)]
    args += [mod, g_norm, w_in, g_sgu, w_s, b_s, w_pa] + list(casts)
    cast_specs = [pl.BlockSpec((w.shape[0] // steps, w.shape[1]), lambda i: (i, 0)) for w in casts]
    assert all(w.shape[0] % (steps * 2 * SUBLANES) == 0 for w in casts)
    return pl.pallas_call(
        functools.partial(_mixa_kernel, pos_blocks=pos_blocks if pos_tabs is not None else 0,
                          chunk=chunk, da=da, n_cast=len(casts)),
        grid=(steps,),
        in_specs=in_specs + cast_specs,
        out_specs=[pl.BlockSpec((tm, d), lambda i: (i, 0)),
                   pl.BlockSpec((T1, tm // T1, db), lambda i: (0, i, 0))] + cast_specs,
        out_shape=[jax.ShapeDtypeStruct((ntok, d), BF16),
                   jax.ShapeDtypeStruct((T1, ntok // T1, db), BF16)]
        + [jax.ShapeDtypeStruct(w.shape, BF16) for w in casts],
        scratch_shapes=[pltpu.VMEM((tm, da), BF16), pltpu.VMEM((db // LANES, tm, LANES), F32)],
        compiler_params=_params(("arbitrary",)),
        name="mixa",
    )(*args)


def _s5_kernel(*refs, nsb, n_chunks, has_h0, emit_state, p, ci):
    it = iter(refs)
    x_ref, w1_ref, wout_ref, pw_ref, d_ref = next(it), next(it), next(it), next(it), next(it)
    h0_ref = next(it) if has_h0 else None
    y_ref = next(it)
    e_ref = next(it) if emit_state else None
    ut_ref, s_ref, hin_ref = next(it), next(it), next(it)

    ngl, kt, cols = ut_ref.shape
    npair = ngl // 2
    pitch = s_ref.shape[2] // nsb

    for t in range(T1):
        st = x_ref[t].astype(F32).T
        for gl in range(ngl):
            ut_ref[gl, t * ci:(t + 1) * ci, :] = st[gl * ci:(gl + 1) * ci, :]

    def local(pr, carry):
        r = []
        for hlf in range(2):
            gl = 2 * pr + hlf
            r1 = jnp.dot(w1_ref[gl], ut_ref[gl].astype(BF16), preferred_element_type=F32)
            ut_ref[gl] = r1[:kt]
            r.append(r1)
        for q in range(4):
            lo = kt + q * p
            sq = jnp.concatenate([r[0][lo:lo + p], r[1][lo:lo + p]], axis=0).T
            for b in range(nsb):
                s_ref[pr, q, b * pitch:b * pitch + n_chunks, :] = sq[b * n_chunks:(b + 1) * n_chunks]
        return carry

    lax.fori_loop(0, npair, local, 0, unroll=True)

    def rows(n):
        return pl.ds(n, nsb, stride=pitch) if nsb > 1 else pl.ds(n, 1)

    def step(n, hs):
        rf, rb = rows(n), rows(n_chunks - 1 - n)
        out = []
        for pr in range(npair):
            fr, fi, br, bi = hs[4 * pr:4 * pr + 4]
            afr, afi = pw_ref[pr, 0:1, :], pw_ref[pr, 1:2, :]
            abr, abi = pw_ref[pr, 2:3, :], pw_ref[pr, 3:4, :]
            sfr, sfi = s_ref[pr, 0, rf, :], s_ref[pr, 1, rf, :]
            sbr, sbi = s_ref[pr, 2, rb, :], s_ref[pr, 3, rb, :]
            hin_ref[pr, 0, rf, :] = fr
            hin_ref[pr, 1, rf, :] = fi
            hin_ref[pr, 2, rb, :] = br
            hin_ref[pr, 3, rb, :] = bi
            out += [afr * fr - afi * fi + sfr, afr * fi + afi * fr + sfi,
                    abr * br - abi * bi + sbr, abr * bi + abi * br + sbi]
        return tuple(out)

    if has_h0:
        init = tuple(h0_ref[pr, q] for pr in range(npair) for q in range(4))
    else:
        init = tuple(jnp.zeros((nsb, LANES), F32) for _ in range(4 * npair))
    fin = lax.fori_loop(0, n_chunks, step, init, unroll=4)
    if emit_state:
        for pr in range(npair):
            for q in range(4):
                e_ref[pr, q] = fin[4 * pr + q]

    first = lax.broadcasted_iota(jnp.int32, (cols, 8 * p), 1) % LANES < p

    def inter(pr, carry):
        hin = jnp.concatenate(
            [jnp.concatenate([hin_ref[pr, q, b * pitch:b * pitch + n_chunks, :] for b in range(nsb)], axis=0)
             for q in range(4)], axis=1)
        for hlf, hg in enumerate((jnp.where(first, hin, 0.0), jnp.where(first, 0.0, hin))):
            gl = 2 * pr + hlf
            ut_ref[gl] += lax.dot_general(wout_ref[gl], hg.astype(BF16), (((1,), (1,)), ((), ())),
                                          preferred_element_type=F32)
        return carry

    lax.fori_loop(0, npair, inter, 0, unroll=True)

    for t in range(T1):
        yt = jnp.concatenate([ut_ref[gl, t * ci:(t + 1) * ci, :] for gl in range(ngl)], axis=0)
        y_ref[t] = _gelu(yt.T + d_ref[...] * x_ref[t].astype(F32)).astype(y_ref.dtype)


def _s5_call(xb, w1t, wout2, pw, d_skip, h0p, n, nsb, emit_state):
    _, ncol, db = xb.shape
    g, kt = w1t.shape[0], w1t.shape[2]
    p = (w1t.shape[1] - kt) // 4
    ci = db // g
    ngl = LANES // ci
    npair = ngl // 2
    assert 2 * p == LANES and ngl % 2 == 0
    n_chunks = n // T1
    cols = nsb * n_chunks
    has_h0 = h0p is not None
    in_specs = [pl.BlockSpec((T1, cols, LANES), lambda i, j: (0, j, i)),
                pl.BlockSpec((ngl,) + w1t.shape[1:], lambda i, j: (i, 0, 0)),
                pl.BlockSpec((ngl,) + wout2.shape[1:], lambda i, j: (i, 0, 0)),
                pl.BlockSpec((npair,) + pw.shape[1:], lambda i, j: (i, 0, 0)),
                pl.BlockSpec((1, LANES), lambda i, j: (0, i))]
    args = [xb, w1t, wout2, pw, d_skip]
    if has_h0:
        in_specs.append(pl.BlockSpec((None, npair, 4, nsb, LANES), lambda i, j: (j, i, 0, 0, 0)))
        args.append(h0p)
    out_specs = [pl.BlockSpec((T1, cols, LANES), lambda i, j: (0, j, i))]
    out_shape = [jax.ShapeDtypeStruct((T1, ncol, db), xb.dtype)]
    if emit_state:
        out_specs.append(pl.BlockSpec((None, npair, 4, nsb, LANES), lambda i, j: (j, i, 0, 0, 0)))
        out_shape.append(jax.ShapeDtypeStruct((ncol // cols, g // 2, 4, nsb, LANES), F32))
    return pl.pallas_call(
        functools.partial(_s5_kernel, nsb=nsb, n_chunks=n_chunks, has_h0=has_h0, emit_state=emit_state,
                          p=p, ci=ci),
        grid=(g // ngl, ncol // cols),
        in_specs=in_specs,
        out_specs=out_specs,
        out_shape=out_shape,
        scratch_shapes=[pltpu.VMEM((ngl, kt, cols), F32)]
        + [pltpu.VMEM((npair, 4, nsb * (n_chunks + SUBLANES), LANES), F32)] * 2,
        compiler_params=_params(("arbitrary", "arbitrary")),
        name="s5",
    )(*args)


def _mixb_kernel(*refs, pos_blocks, n_gb):
    if pos_blocks:
        x_ref, er_ref, ec_ref, mod_ref, g_ref, y_ref, pa_ref = refs[:7]
        x = _add_pos(x_ref[...], er_ref, ec_ref, pl.program_id(0) % pos_blocks)
        refs = refs[7:]
    else:
        x_ref, mod_ref, g_ref, y_ref, pa_ref = refs[:5]
        x = x_ref[...]
        refs = refs[5:]
    wgb_refs = refs[:n_gb]
    wglu_ref, bglu_ref, wpb_ref, wout_ref, o_ref, ys_ref = refs[n_gb:]
    tm = x.shape[0]
    h = _modnorm(x, g_ref[...], mod_ref[0:1, :], mod_ref[1:2, :]).astype(BF16)
    for cb in range(ys_ref.shape[0]):
        for t in range(T1):
            ys_ref[cb, pl.ds(t, tm // T1, stride=T1), :] = y_ref[t, :, cb * LANES:(cb + 1) * LANES].astype(F32)
    y = jnp.concatenate([ys_ref[cb] for cb in range(ys_ref.shape[0])], axis=1)
    gate = _sigmoid(jnp.dot(y.astype(BF16), wglu_ref[...], preferred_element_type=F32) + bglu_ref[...])
    yb = (y * gate).astype(BF16)
    pb = jnp.dot(yb, wpb_ref[...], preferred_element_type=F32)
    glb = jnp.concatenate([jnp.dot(h, w[...], preferred_element_type=F32) for w in wgb_refs], axis=1)
    pb = _sigmoid(glb) * pb
    m = (pa_ref[...].astype(F32) + pb).astype(BF16)
    o_ref[...] = x + mod_ref[2:3, :] * jnp.dot(m, wout_ref[...], preferred_element_type=F32)


def _mixb_call(x2, pos_tabs, mod, mod_row, pos_blocks, g_norm, y, pa, w_in, w_glu, b_glu, w_pb, w_out, tm):
    ntok, d = x2.shape
    db = y.shape[2]
    gb0 = w_in.shape[1] - d
    bw = math.gcd(gb0, d)
    n_gb = d // bw
    in_specs = [pl.BlockSpec((tm, d), lambda i: (i, 0))]
    args = [x2]
    if pos_tabs is not None:
        in_specs += [_resident(pos_tabs[0].shape), _resident(pos_tabs[1].shape)]
        args += list(pos_tabs)
    in_specs += [pl.BlockSpec((None, N_MOD, d), lambda i: (mod_row(i), 0, 0)),
                 _resident((1, d)),
                 pl.BlockSpec((T1, tm // T1, db), lambda i: (0, i, 0)),
                 pl.BlockSpec((tm, d), lambda i: (i, 0)),
                 *[pl.BlockSpec((d, bw), functools.partial(lambda k, i: (0, gb0 // bw + k), k),
                                pipeline_mode=pl.Buffered(1)) for k in range(n_gb)],
                 _resident(w_glu.shape), _resident((1, db)),
                 _resident(w_pb.shape), _resident(w_out.shape)]
    args += [mod, g_norm, y, pa] + [w_in] * n_gb + [w_glu, b_glu, w_pb, w_out]
    return pl.pallas_call(
        functools.partial(_mixb_kernel, pos_blocks=pos_blocks if pos_tabs is not None else 0, n_gb=n_gb),
        grid=(ntok // tm,),
        in_specs=in_specs,
        out_specs=pl.BlockSpec((tm, d), lambda i: (i, 0)),
        out_shape=jax.ShapeDtypeStruct((ntok, d), F32),
        scratch_shapes=[pltpu.VMEM((db // LANES, tm, LANES), F32)],
        compiler_params=_params(("arbitrary",)),
        name="mixb",
    )(*args)


def _mlp_kernel(x_ref, mod_ref, g_ref, w1_ref, w2_ref, gf_ref, o_ref, h_ref, acc_ref):
    j = pl.program_id(1)
    nj = pl.num_programs(1)

    def ffn(h):
        hid = jnp.dot(h, w1_ref[...], preferred_element_type=F32)
        hid = jnp.square(jnp.maximum(hid, 0.0)).astype(BF16)
        return jnp.dot(hid, w2_ref[...], preferred_element_type=F32)

    @pl.when(j == 0)
    def _():
        h = _modnorm(x_ref[...], g_ref[...], mod_ref[3:4, :], mod_ref[4:5, :]).astype(BF16)
        h_ref[...] = h
        acc_ref[...] = ffn(h)

    @pl.when(jnp.logical_and(j > 0, j < nj - 1))
    def _():
        acc_ref[...] += ffn(h_ref[...])

    @pl.when(j == nj - 1)
    def _():
        x2 = x_ref[...] + mod_ref[5:6, :] * (acc_ref[...] + ffn(h_ref[...]))
        o_ref[...] = _rms(x2, gf_ref[...])


def _mlp_call(x1, mod, mod_row, g_norm, w1, w2, g_final, tm, tf):
    ntok, d = x1.shape
    dff = w1.shape[1]
    assert dff // tf >= 2
    return pl.pallas_call(
        _mlp_kernel,
        grid=(ntok // tm, dff // tf),
        in_specs=[pl.BlockSpec((tm, d), lambda i, j: (i, 0)),
                  pl.BlockSpec((None, N_MOD, d), lambda i, j: (mod_row(i), 0, 0)),
                  _resident((1, d)),
                  pl.BlockSpec((d, tf), lambda i, j: (0, j)),
                  pl.BlockSpec((tf, d), lambda i, j: (j, 0)),
                  _resident((1, d))],
        out_specs=pl.BlockSpec((tm, d), lambda i, j: (i, 0)),
        out_shape=jax.ShapeDtypeStruct((ntok, d), F32),
        scratch_shapes=[pltpu.VMEM((tm, d), BF16), pltpu.VMEM((tm, d), F32)],
        compiler_params=_params(("arbitrary", "arbitrary")),
        name="mlp",
    )(x1, mod, g_norm, w1, w2, g_final)


def _pos_tables(n_tokens, d):
    rows = n_tokens // GRID_W
    quarter = d // 4
    omega = 1.0 / (POS_BASE ** (jnp.arange(quarter, dtype=F32) / quarter))
    r = jnp.arange(rows, dtype=F32)[:, None] * omega
    col = jnp.arange(GRID_W, dtype=F32)[:, None] * omega
    e_r = jnp.concatenate([jnp.sin(r), jnp.cos(r)], axis=-1)
    e_c = jnp.concatenate([jnp.sin(col), jnp.cos(col)], axis=-1)
    return e_r, e_c


def _cmul(ar, ai, br, bi):
    return ar * br - ai * bi, ar * bi + ai * br


def _ssm_prepare(a_re, a_im, log_dt, b_re, b_im, c_re, c_im):
    ndir, g, p = a_re.shape
    ci = b_re.shape[-1]
    kt = T1 * ci
    hi = lax.Precision.HIGHEST
    dt = jnp.exp(log_dt)[..., None]
    lr, ang = a_re * dt, a_im * dt

    def power(j, lr_, ang_):
        m = jnp.exp(j * lr_)
        return m * jnp.cos(j * ang_), m * jnp.sin(j * ang_)

    ab_re, ab_im = power(1.0, lr, ang)
    den = a_re * a_re + a_im * a_im
    f_re, f_im = _cmul(ab_re - 1.0, ab_im, a_re / den, -a_im / den)
    bb_re, bb_im = _cmul(f_re[..., None], f_im[..., None], b_re, b_im)

    tt = jnp.arange(T1, dtype=F32)
    j_in = jnp.stack([T1 - 1 - tt, tt])[:, None, None, :]
    pl_re, pl_im = power(j_in, lr[..., None], ang[..., None])
    gm = lambda w: jnp.swapaxes(w, 0, 1)
    ba = jnp.stack(_cmul(jnp.repeat(gm(pl_re), ci, axis=-1), jnp.repeat(gm(pl_im), ci, axis=-1),
                         jnp.tile(gm(bb_re), (1, 1, 1, T1)), jnp.tile(gm(bb_im), (1, 1, 1, T1))),
                   axis=2)
    wend_t = ba.reshape(g, 4 * p, kt)

    cs = jnp.stack([gm(c_re), -gm(c_im)], axis=2)
    kk = jnp.einsum('gdxcp,gdxpl->dgcl', cs, ba, precision=hi)
    kf, kb = kk[0], kk[1]
    mid = (T1 - 1) * ci
    r = jnp.concatenate([kf[..., :mid], kf[..., mid:] + kb[..., :ci], kb[..., ci:]], axis=-1)
    toep_t = jnp.stack([r[..., (T1 - 1 - t) * ci:(2 * T1 - 1 - t) * ci] for t in range(T1)], axis=1)
    toep_t = toep_t.reshape(g, kt, kt)
    w1t = jnp.concatenate([toep_t, wend_t], axis=1).astype(BF16)

    dup = lambda w: jnp.concatenate([w, w], axis=-1)

    def wout(dd, j):
        po_re, po_im = power(j[None, :, None], dup(lr[dd])[:, None, :], dup(ang[dd])[:, None, :])
        ca_re, ca_im = _cmul(dup(c_re[dd])[:, None], dup(c_im[dd])[:, None], po_re[:, :, None], po_im[:, :, None])
        return ca_re.reshape(g, kt, 2 * p), (-ca_im).reshape(g, kt, 2 * p)
    wout2 = jnp.concatenate(wout(0, tt + 1.0) + wout(1, T1 - tt), axis=-1).astype(BF16)

    pt_re, pt_im = power(float(T1), lr, ang)
    pw = jnp.stack([pt_re[0], pt_im[0], pt_re[1], pt_im[1]], axis=0)
    pw = jnp.transpose(pw.reshape(4, g // 2, 2 * p), (1, 0, 2))
    return w1t, wout2, pw


def _trunk(x, pos_tabs, mod, mod_row, h0p, lw, nsb, emit_state, tm_a, tm_b, casts):
    bsz, n, d = x.shape
    x2 = x.reshape(bsz * n, d)
    pa, xb, *cast_out = _mixa_call(x2, pos_tabs, mod, mod_row(tm_a), n // tm_a, lw["g_norm_mix"], lw["w_in"],
                                   lw["db"], lw["g_sgu"], lw["w_s"], lw["b_s"], lw["w_pa"], tm_a,
                                   tuple(casts.values()))
    lw = dict(lw, **dict(zip(casts, cast_out)))
    res = _s5_call(xb, lw["w1t"], lw["wout2"], lw["pw"], lw["d_skip"], h0p, n, nsb, emit_state)
    x1 = _mixb_call(x2, pos_tabs, mod, mod_row(tm_b), n // tm_b, lw["g_norm_mix"], res[0], pa, lw["w_in"],
                    lw["w_glu"], lw["b_glu"], lw["w_pb"], lw["w_out"], tm_b)
    return x1, (res[1] if emit_state else None), lw


def kernel(x_prompt, x_sample, state_ssm_re, state_ssm_im, c, c_ctx, w_ada, b_ada, g_norm_mix, w_in,
           g_sgu, w_spatial, b_spatial, ssm_a_re, ssm_a_im, ssm_log_dt, ssm_b_re, ssm_b_im, ssm_c_re,
           ssm_c_im, ssm_d, w_glu, b_glu, w_proj_a, w_proj_b, w_out, g_norm_mlp, w_mlp_in, w_mlp_out,
           g_final):
    bp, sp, d = x_prompt.shape
    bs, ss, _ = x_sample.shape
    depth = w_in.shape[0]
    assert depth == 1, "positional embedding and final norm are fused assuming a single trunk layer"
    da = w_proj_a.shape[1]
    db = w_proj_b.shape[1]
    g, p = ssm_a_re.shape[2], ssm_a_re.shape[3]
    tm_a, tm_b, tm_mlp = 512, 512, 512
    tf = min(2048, w_mlp_in.shape[2] // 2)

    rows = -(-(bs + 1) // 8) * 8
    cc = jnp.concatenate([c, c_ctx[None], jnp.zeros((rows - bs - 1, d), F32)], axis=0)
    pos_tabs = _pos_tables(ss, d)
    l = 0
    mod = _mod_call(cc, w_ada[l], b_ada[l]).reshape(rows, N_MOD, d)
    w1t, wout2, pw = _ssm_prepare(ssm_a_re[l], ssm_a_im[l], ssm_log_dt[l], ssm_b_re[l],
                                  ssm_b_im[l], ssm_c_re[l], ssm_c_im[l])
    wi = w_in[l]
    lw = dict(
        g_norm_mix=g_norm_mix[l].reshape(1, d),
        w_in=wi.astype(BF16), db=db,
        g_sgu=g_sgu[l].reshape(1, da),
        w_s=w_spatial[l].astype(BF16),
        b_s=b_spatial[l][:, :, None],
        w_pa=w_proj_a[l].astype(BF16),
        b_glu=b_glu[l].reshape(1, db),
        w1t=w1t, wout2=wout2, pw=pw, d_skip=ssm_d[l].reshape(1, db))

    def pairs(st):
        return jnp.transpose(st.reshape(bs, 2, g // 2, 2 * p), (1, 2, 0, 3))
    sre, sim = pairs(state_ssm_re[:, l]), pairs(state_ssm_im[:, l])
    h0p = jnp.stack([sre[0], sim[0], sre[1], sim[1]], axis=1)[None]

    ctx_row = lambda tm: (lambda i: bs)
    seq_row = lambda tm: (lambda i: i // (ss // tm))
    xp1, e, lw = _trunk(x_prompt, None, mod, ctx_row, None, lw, bp, True, tm_a, tm_b,
                        casts=dict(w_glu=w_glu[l], w_pb=w_proj_b[l], w_out=w_out[l]))
    xs1, _, lw = _trunk(x_sample, pos_tabs, mod, seq_row, h0p, lw, bs, False, tm_a, tm_b,
                        casts=dict(w1=w_mlp_in[l], w2=w_mlp_out[l]))
    w1, w2 = lw["w1"], lw["w2"]

    gf = g_final.reshape(1, d)
    gm = g_norm_mlp[l].reshape(1, d)
    xp = _mlp_call(xp1, mod, ctx_row(tm_mlp), gm, w1, w2, gf, tm_mlp, tf).reshape(bp, sp, d)
    xs = _mlp_call(xs1, mod, seq_row(tm_mlp), gm, w1, w2, gf, tm_mlp, tf).reshape(bs, ss, d)

    ev = e.reshape(g // 2, 2, 2, bp, 2, p)
    fin = jnp.transpose(ev, (2, 3, 1, 0, 4, 5)).reshape(2, bp, 2, g, p)
    return xp, xs, fin[0][:, None], fin[1][:, None]
```

```python
import functools
import math

import jax
import jax.numpy as jnp
from jax import lax
from jax.experimental import pallas as pl
from jax.experimental.pallas import tpu as pltpu

F32 = jnp.float32
BF16 = jnp.bfloat16

EPS = 1e-6
POS_BASE = 10000.0
GRID_W = 64
N_MOD = 6
T1 = 16
LANES = 128
SUBLANES = 8
VMEM_LIMIT = 63 * 1024 * 1024


def _gelu(x):
    return 0.5 * x * (1.0 + jnp.tanh(0.7978845608028654 * (x + 0.044715 * (x * x * x))))


def _sigmoid(x):
    return 1.0 / (1.0 + jnp.exp(-x))


def _rms(x, g):
    return x * lax.rsqrt(jnp.mean(x * x, axis=-1, keepdims=True) + EPS) * g


def _modnorm(x, g, shift, scale):
    return _rms(x, g) * (1.0 + scale) + shift


def _resident(shape):
    nd = len(shape)
    return pl.BlockSpec(shape, lambda *_: (0,) * nd, pipeline_mode=pl.Buffered(1))


def _params(sem):
    return pltpu.CompilerParams(dimension_semantics=sem, vmem_limit_bytes=VMEM_LIMIT)


def _add_pos(x, er_ref, ec_ref, tile_in_seq):
    tm, d = x.shape
    half = d // 2
    r0 = tile_in_seq * (tm // GRID_W)
    ec = ec_ref[...]
    parts = []
    for k in range(tm // GRID_W):
        er = jnp.broadcast_to(er_ref[pl.ds(r0 + k, 1), :], (GRID_W, half))
        parts.append(jnp.concatenate([er, ec], axis=1))
    return x + jnp.concatenate(parts, axis=0)


def _mod_kernel(c_ref, w_ref, b_ref, o_ref):
    cc = c_ref[...]
    s = cc * _sigmoid(cc)
    o_ref[...] = jnp.dot(s, w_ref[...], preferred_element_type=F32) + b_ref[...]


def _mod_call(cc, w_ada, b_ada):
    rows, d = cc.shape
    n = w_ada.shape[1]
    tn = n // (2 * N_MOD)
    assert n % tn == 0 and tn % LANES == 0
    return pl.pallas_call(
        _mod_kernel,
        grid=(n // tn,),
        in_specs=[pl.BlockSpec((rows, d), lambda j: (0, 0)),
                  pl.BlockSpec((d, tn), lambda j: (0, j)),
                  pl.BlockSpec((1, tn), lambda j: (0, j))],
        out_specs=pl.BlockSpec((rows, tn), lambda j: (0, j)),
        out_shape=jax.ShapeDtypeStruct((rows, n), F32),
        compiler_params=_params(("arbitrary",)),
        name="mod",
    )(cc, w_ada, b_ada.reshape(1, n))


def _mixa_kernel(*refs, pos_blocks, chunk, da, n_cast):
    if pos_blocks:
        x_ref, er_ref, ec_ref = refs[:3]
        x = _add_pos(x_ref[...], er_ref, ec_ref, pl.program_id(0) % pos_blocks)
        refs = refs[3:]
    else:
        x_ref = refs[0]
        x = x_ref[...]
        refs = refs[1:]
    mod_ref, g_ref, w_ref, gs_ref, ws_ref, bs_ref, wpa_ref = refs[:7]
    cast_in, (o_ref, xb_ref) = refs[7:7 + n_cast], refs[7 + n_cast:9 + n_cast]
    cast_out, (ya_ref, xs_ref) = refs[9 + n_cast:9 + 2 * n_cast], refs[9 + 2 * n_cast:]
    for ci_ref, co_ref in zip(cast_in, cast_out):
        co_ref[...] = ci_ref[...].astype(co_ref.dtype)
    tm, d = x.shape
    h = _modnorm(x, g_ref[...], mod_ref[0:1, :], mod_ref[1:2, :]).astype(BF16)
    z = jnp.dot(h, w_ref[...], preferred_element_type=F32)
    db = xs_ref.shape[0] * LANES
    for cb in range(xs_ref.shape[0]):
        c0 = 2 * da + cb * LANES
        xs_ref[cb] = z[:, c0:c0 + LANES]
        for t in range(T1):
            xb_ref[t, :, cb * LANES:(cb + 1) * LANES] = (
                xs_ref[cb, pl.ds(t, tm // T1, stride=T1), :].astype(xb_ref.dtype))
    u = _gelu(z[:, :da])
    v = _rms(_gelu(z[:, da:2 * da]), gs_ref[...]).astype(BF16)
    ng = ws_ref.shape[0]
    cg = da // ng
    for ck in range(tm // chunk):
        r0 = ck * chunk
        for gi in range(ng):
            c0 = gi * cg
            s = jnp.dot(ws_ref[gi], v[r0:r0 + chunk, c0:c0 + cg], preferred_element_type=F32)
            ya_ref[r0:r0 + chunk, c0:c0 + cg] = (u[r0:r0 + chunk, c0:c0 + cg] * (s + bs_ref[gi])).astype(BF16)
    pa = jnp.dot(ya_ref[...], wpa_ref[...], preferred_element_type=F32)
    o_ref[...] = (_sigmoid(z[:, 2 * da + db:2 * da + db + d]) * pa).astype(o_ref.dtype)


def _mixa_call(x2, pos_tabs, mod, mod_row, pos_blocks, g_norm, w_in, db, g_sgu, w_s, b_s, w_pa, tm, casts=()):
    ntok, d = x2.shape
    steps = ntok // tm
    da = w_pa.shape[0]
    chunk = w_s.shape[1]
    in_specs = [pl.BlockSpec((tm, d), lambda i: (i, 0))]
    args = [x2]
    if pos_tabs is not None:
        in_specs += [_resident(pos_tabs[0].shape), _resident(pos_tabs[1].shape)]
        args += list(pos_tabs)
    in_specs += [pl.BlockSpec((None, N_MOD, d), lambda i: (mod_row(i), 0, 0)),
                 _resident((1, d)), _resident((d, 2 * da + db + d)), _resident((1, da)),
                 _resident(w_s.shape), _resident(b_s.shape), _resident(w_pa.shape)]
    args += [mod, g_norm, w_in, g_sgu, w_s, b_s, w_pa] + list(casts)
    cast_specs = [pl.BlockSpec((w.shape[0] // steps, w.shape[1]), lambda i: (i, 0)) for w in casts]
    assert all(w.shape[0] % (steps * 2 * SUBLANES) == 0 for w in casts)
    return pl.pallas_call(
        functools.partial(_mixa_kernel, pos_blocks=pos_blocks if pos_tabs is not None else 0,
                          chunk=chunk, da=da, n_cast=len(casts)),
        grid=(steps,),
        in_specs=in_specs + cast_specs,
        out_specs=[pl.BlockSpec((tm, d), lambda i: (i, 0)),
                   pl.BlockSpec((T1, tm // T1, db), lambda i: (0, i, 0))] + cast_specs,
        out_shape=[jax.ShapeDtypeStruct((ntok, d), BF16),
                   jax.ShapeDtypeStruct((T1, ntok // T1, db), BF16)]
        + [jax.ShapeDtypeStruct(w.shape, BF16) for w in casts],
        scratch_shapes=[pltpu.VMEM((tm, da), BF16), pltpu.VMEM((db // LANES, tm, LANES), F32)],
        compiler_params=_params(("arbitrary",)),
        name="mixa",
    )(*args)


def _s5_kernel(*refs, nsb, n_chunks, has_h0, emit_state, p, ci):
    it = iter(refs)
    x_ref, w1_ref, wout_ref, pw_ref, d_ref = next(it), next(it), next(it), next(it), next(it)
    h0_ref = next(it) if has_h0 else None
    y_ref = next(it)
    e_ref = next(it) if emit_state else None
    ut_ref, s_ref, hin_ref = next(it), next(it), next(it)

    ngl, kt, cols = ut_ref.shape
    npair = ngl // 2
    pitch = s_ref.shape[2] // nsb

    for t in range(T1):
        st = x_ref[t].astype(F32).T
        for gl in range(ngl):
            ut_ref[gl, t * ci:(t + 1) * ci, :] = st[gl * ci:(gl + 1) * ci, :]

    def local(pr, carry):
        r = []
        for hlf in range(2):
            gl = 2 * pr + hlf
            r1 = jnp.dot(w1_ref[gl], ut_ref[gl].astype(BF16), preferred_element_type=F32)
            ut_ref[gl] = r1[:kt]
            r.append(r1)
        for q in range(4):
            lo = kt + q * p
            sq = jnp.concatenate([r[0][lo:lo + p], r[1][lo:lo + p]], axis=0).T
            for b in range(nsb):
                s_ref[pr, q, b * pitch:b * pitch + n_chunks, :] = sq[b * n_chunks:(b + 1) * n_chunks]
        return carry

    lax.fori_loop(0, npair, local, 0, unroll=True)

    def rows(n):
        return pl.ds(n, nsb, stride=pitch) if nsb > 1 else pl.ds(n, 1)

    def step(n, hs):
        rf, rb = rows(n), rows(n_chunks - 1 - n)
        out = []
        for pr in range(npair):
            fr, fi, br, bi = hs[4 * pr:4 * pr + 4]
            afr, afi = pw_ref[pr, 0:1, :], pw_ref[pr, 1:2, :]
            abr, abi = pw_ref[pr, 2:3, :], pw_ref[pr, 3:4, :]
            sfr, sfi = s_ref[pr, 0, rf, :], s_ref[pr, 1, rf, :]
            sbr, sbi = s_ref[pr, 2, rb, :], s_ref[pr, 3, rb, :]
            hin_ref[pr, 0, rf, :] = fr
            hin_ref[pr, 1, rf, :] = fi
            hin_ref[pr, 2, rb, :] = br
            hin_ref[pr, 3, rb, :] = bi
            out += [afr * fr - afi * fi + sfr, afr * fi + afi * fr + sfi,
                    abr * br - abi * bi + sbr, abr * bi + abi * br + sbi]
        return tuple(out)

    if has_h0:
        init = tuple(h0_ref[pr, q] for pr in range(npair) for q in range(4))
    else:
        init = tuple(jnp.zeros((nsb, LANES), F32) for _ in range(4 * npair))
    fin = lax.fori_loop(0, n_chunks, step, init, unroll=4)
    if emit_state:
        for pr in range(npair):
            for q in range(4):
                e_ref[pr, q] = fin[4 * pr + q]

    first = lax.broadcasted_iota(jnp.int32, (cols, 8 * p), 1) % LANES < p

    def inter(pr, carry):
        hin = jnp.concatenate(
            [jnp.concatenate([hin_ref[pr, q, b * pitch:b * pitch + n_chunks, :] for b in range(nsb)], axis=0)
             for q in range(4)], axis=1)
        for hlf, hg in enumerate((jnp.where(first, hin, 0.0), jnp.where(first, 0.0, hin))):
            gl = 2 * pr + hlf
            ut_ref[gl] += lax.dot_general(wout_ref[gl], hg.astype(BF16), (((1,), (1,)), ((), ())),
                                          preferred_element_type=F32)
        return carry

    lax.fori_loop(0, npair, inter, 0, unroll=True)

    for t in range(T1):
        yt = jnp.concatenate([ut_ref[gl, t * ci:(t + 1) * ci, :] for gl in range(ngl)], axis=0)
        y_ref[t] = _gelu(yt.T + d_ref[...] * x_ref[t].astype(F32)).astype(y_ref.dtype)


def _s5_call(xb, w1t, wout2, pw, d_skip, h0p, n, nsb, emit_state):
    _, ncol, db = xb.shape
    g, kt = w1t.shape[0], w1t.shape[2]
    p = (w1t.shape[1] - kt) // 4
    ci = db // g
    ngl = LANES // ci
    npair = ngl // 2
    assert 2 * p == LANES and ngl % 2 == 0
    n_chunks = n // T1
    cols = nsb * n_chunks
    has_h0 = h0p is not None
    in_specs = [pl.BlockSpec((T1, cols, LANES), lambda i, j: (0, j, i)),
                pl.BlockSpec((ngl,) + w1t.shape[1:], lambda i, j: (i, 0, 0)),
                pl.BlockSpec((ngl,) + wout2.shape[1:], lambda i, j: (i, 0, 0)),
                pl.BlockSpec((npair,) + pw.shape[1:], lambda i, j: (i, 0, 0)),
                pl.BlockSpec((1, LANES), lambda i, j: (0, i))]
    args = [xb, w1t, wout2, pw, d_skip]
    if has_h0:
        in_specs.append(pl.BlockSpec((None, npair, 4, nsb, LANES), lambda i, j: (j, i, 0, 0, 0)))
        args.append(h0p)
    out_specs = [pl.BlockSpec((T1, cols, LANES), lambda i, j: (0, j, i))]
    out_shape = [jax.ShapeDtypeStruct((T1, ncol, db), xb.dtype)]
    if emit_state:
        out_specs.append(pl.BlockSpec((None, npair, 4, nsb, LANES), lambda i, j: (j, i, 0, 0, 0)))
        out_shape.append(jax.ShapeDtypeStruct((ncol // cols, g // 2, 4, nsb, LANES), F32))
    return pl.pallas_call(
        functools.partial(_s5_kernel, nsb=nsb, n_chunks=n_chunks, has_h0=has_h0, emit_state=emit_state,
                          p=p, ci=ci),
        grid=(g // ngl, ncol // cols),
        in_specs=in_specs,
        out_specs=out_specs,
        out_shape=out_shape,
        scratch_shapes=[pltpu.VMEM((ngl, kt, cols), F32)]
        + [pltpu.VMEM((npair, 4, nsb * (n_chunks + SUBLANES), LANES), F32)] * 2,
        compiler_params=_params(("arbitrary", "arbitrary")),
        name="s5",
    )(*args)


def _mixb_kernel(*refs, pos_blocks, n_gb):
    if pos_blocks:
        x_ref, er_ref, ec_ref, mod_ref, g_ref, y_ref, pa_ref = refs[:7]
        x = _add_pos(x_ref[...], er_ref, ec_ref, pl.program_id(0) % pos_blocks)
        refs = refs[7:]
    else:
        x_ref, mod_ref, g_ref, y_ref, pa_ref = refs[:5]
        x = x_ref[...]
        refs = refs[5:]
    wgb_refs = refs[:n_gb]
    wglu_ref, bglu_ref, wpb_ref, wout_ref, o_ref, ys_ref = refs[n_gb:]
    tm = x.shape[0]
    h = _modnorm(x, g_ref[...], mod_ref[0:1, :], mod_ref[1:2, :]).astype(BF16)
    for cb in range(ys_ref.shape[0]):
        for t in range(T1):
            ys_ref[cb, pl.ds(t, tm // T1, stride=T1), :] = y_ref[t, :, cb * LANES:(cb + 1) * LANES].astype(F32)
    y = jnp.concatenate([ys_ref[cb] for cb in range(ys_ref.shape[0])], axis=1)
    gate = _sigmoid(jnp.dot(y.astype(BF16), wglu_ref[...], preferred_element_type=F32) + bglu_ref[...])
    yb = (y * gate).astype(BF16)
    pb = jnp.dot(yb, wpb_ref[...], preferred_element_type=F32)
    glb = jnp.concatenate([jnp.dot(h, w[...], preferred_element_type=F32) for w in wgb_refs], axis=1)
    pb = _sigmoid(glb) * pb
    m = (pa_ref[...].astype(F32) + pb).astype(BF16)
    o_ref[...] = x + mod_ref[2:3, :] * jnp.dot(m, wout_ref[...], preferred_element_type=F32)


def _mixb_call(x2, pos_tabs, mod, mod_row, pos_blocks, g_norm, y, pa, w_in, w_glu, b_glu, w_pb, w_out, tm):
    ntok, d = x2.shape
    db = y.shape[2]
    gb0 = w_in.shape[1] - d
    bw = math.gcd(gb0, d)
    n_gb = d // bw
    in_specs = [pl.BlockSpec((tm, d), lambda i: (i, 0))]
    args = [x2]
    if pos_tabs is not None:
        in_specs += [_resident(pos_tabs[0].shape), _resident(pos_tabs[1].shape)]
        args += list(pos_tabs)
    in_specs += [pl.BlockSpec((None, N_MOD, d), lambda i: (mod_row(i), 0, 0)),
                 _resident((1, d)),
                 pl.BlockSpec((T1, tm // T1, db), lambda i: (0, i, 0)),
                 pl.BlockSpec((tm, d), lambda i: (i, 0)),
                 *[pl.BlockSpec((d, bw), functools.partial(lambda k, i: (0, gb0 // bw + k), k),
                                pipeline_mode=pl.Buffered(1)) for k in range(n_gb)],
                 _resident(w_glu.shape), _resident((1, db)),
                 _resident(w_pb.shape), _resident(w_out.shape)]
    args += [mod, g_norm, y, pa] + [w_in] * n_gb + [w_glu, b_glu, w_pb, w_out]
    return pl.pallas_call(
        functools.partial(_mixb_kernel, pos_blocks=pos_blocks if pos_tabs is not None else 0, n_gb=n_gb),
        grid=(ntok // tm,),
        in_specs=in_specs,
        out_specs=pl.BlockSpec((tm, d), lambda i: (i, 0)),
        out_shape=jax.ShapeDtypeStruct((ntok, d), F32),
        scratch_shapes=[pltpu.VMEM((db // LANES, tm, LANES), F32)],
        compiler_params=_params(("arbitrary",)),
        name="mixb",
    )(*args)


def _mlp_kernel(x_ref, mod_ref, g_ref, w1_ref, w2_ref, gf_ref, o_ref, h_ref, acc_ref):
    j = pl.program_id(1)
    nj = pl.num_programs(1)

    def ffn(h):
        hid = jnp.dot(h, w1_ref[...], preferred_element_type=F32)
        hid = jnp.square(jnp.maximum(hid, 0.0)).astype(BF16)
        return jnp.dot(hid, w2_ref[...], preferred_element_type=F32)

    @pl.when(j == 0)
    def _():
        h = _modnorm(x_ref[...], g_ref[...], mod_ref[3:4, :], mod_ref[4:5, :]).astype(BF16)
        h_ref[...] = h
        acc_ref[...] = ffn(h)

    @pl.when(jnp.logical_and(j > 0, j < nj - 1))
    def _():
        acc_ref[...] += ffn(h_ref[...])

    @pl.when(j == nj - 1)
    def _():
        x2 = x_ref[...] + mod_ref[5:6, :] * (acc_ref[...] + ffn(h_ref[...]))
        o_ref[...] = _rms(x2, gf_ref[...])


def _mlp_call(x1, mod, mod_row, g_norm, w1, w2, g_final, tm, tf):
    ntok, d = x1.shape
    dff = w1.shape[1]
    assert dff // tf >= 2
    return pl.pallas_call(
        _mlp_kernel,
        grid=(ntok // tm, dff // tf),
        in_specs=[pl.BlockSpec((tm, d), lambda i, j: (i, 0)),
                  pl.BlockSpec((None, N_MOD, d), lambda i, j: (mod_row(i), 0, 0)),
                  _resident((1, d)),
                  pl.BlockSpec((d, tf), lambda i, j: (0, j)),
                  pl.BlockSpec((tf, d), lambda i, j: (j, 0)),
                  _resident((1, d))],
        out_specs=pl.BlockSpec((tm, d), lambda i, j: (i, 0)),
        out_shape=jax.ShapeDtypeStruct((ntok, d), F32),
        scratch_shapes=[pltpu.VMEM((tm, d), BF16), pltpu.VMEM((tm, d), F32)],
        compiler_params=_params(("arbitrary", "arbitrary")),
        name="mlp",
    )(x1, mod, g_norm, w1, w2, g_final)


def _pos_tables(n_tokens, d):
    rows = n_tokens // GRID_W
    quarter = d // 4
    omega = 1.0 / (POS_BASE ** (jnp.arange(quarter, dtype=F32) / quarter))
    r = jnp.arange(rows, dtype=F32)[:, None] * omega
    col = jnp.arange(GRID_W, dtype=F32)[:, None] * omega
    e_r = jnp.concatenate([jnp.sin(r), jnp.cos(r)], axis=-1)
    e_c = jnp.concatenate([jnp.sin(col), jnp.cos(col)], axis=-1)
    return e_r, e_c


def _cmul(ar, ai, br, bi):
    return ar * br - ai * bi, ar * bi + ai * br


def _ssm_prepare(a_re, a_im, log_dt, b_re, b_im, c_re, c_im):
    ndir, g, p = a_re.shape
    ci = b_re.shape[-1]
    kt = T1 * ci
    hi = lax.Precision.HIGHEST
    dt = jnp.exp(log_dt)[..., None]
    lr, ang = a_re * dt, a_im * dt

    def power(j, lr_, ang_):
        m = jnp.exp(j * lr_)
        return m * jnp.cos(j * ang_), m * jnp.sin(j * ang_)

    ab_re, ab_im = power(1.0, lr, ang)
    den = a_re * a_re + a_im * a_im
    f_re, f_im = _cmul(ab_re - 1.0, ab_im, a_re / den, -a_im / den)
    bb_re, bb_im = _cmul(f_re[..., None], f_im[..., None], b_re, b_im)

    tt = jnp.arange(T1, dtype=F32)
    j_in = jnp.stack([T1 - 1 - tt, tt])[:, None, None, :]
    pl_re, pl_im = power(j_in, lr[..., None], ang[..., None])
    def ba_dir(dd):
        return jnp.stack(_cmul(jnp.repeat(pl_re[dd], ci, axis=-1), jnp.repeat(pl_im[dd], ci, axis=-1),
                               jnp.tile(bb_re[dd], (1, 1, T1)), jnp.tile(bb_im[dd], (1, 1, T1))), axis=1)
    ba = jnp.stack([ba_dir(0), ba_dir(1)], axis=1)
    wend_t = ba.reshape(g, 4 * p, kt)

    cs = jnp.stack([jnp.stack([c_re[dd], -c_im[dd]], axis=1) for dd in range(ndir)], axis=1)
    kk = jnp.einsum('gdxcp,gdxpl->dgcl', cs, ba, precision=hi)
    kf, kb = kk[0], kk[1]
    mid = (T1 - 1) * ci
    r = jnp.concatenate([kf[..., :mid], kf[..., mid:] + kb[..., :ci], kb[..., ci:]], axis=-1)
    toep_t = jnp.stack([r[..., (T1 - 1 - t) * ci:(2 * T1 - 1 - t) * ci] for t in range(T1)], axis=1)
    toep_t = toep_t.reshape(g, kt, kt)
    w1t = jnp.concatenate([toep_t, wend_t], axis=1).astype(BF16)

    dup = lambda w: jnp.concatenate([w, w], axis=-1)

    def wout(dd, j):
        po_re, po_im = power(j[None, :, None], dup(lr[dd])[:, None, :], dup(ang[dd])[:, None, :])
        ca_re, ca_im = _cmul(dup(c_re[dd])[:, None], dup(c_im[dd])[:, None], po_re[:, :, None], po_im[:, :, None])
        return ca_re.reshape(g, kt, 2 * p), (-ca_im).reshape(g, kt, 2 * p)
    wout2 = jnp.concatenate(wout(0, tt + 1.0) + wout(1, T1 - tt), axis=-1).astype(BF16)

    pt_re, pt_im = power(float(T1), lr, ang)
    pw = jnp.stack([pt_re[0], pt_im[0], pt_re[1], pt_im[1]], axis=0)
    pw = jnp.transpose(pw.reshape(4, g // 2, 2 * p), (1, 0, 2))
    return w1t, wout2, pw


def _trunk(x, pos_tabs, mod, mod_row, h0p, lw, nsb, emit_state, tm_a, tm_b, casts):
    bsz, n, d = x.shape
    x2 = x.reshape(bsz * n, d)
    pa, xb, *cast_out = _mixa_call(x2, pos_tabs, mod, mod_row(tm_a), n // tm_a, lw["g_norm_mix"], lw["w_in"],
                                   lw["db"], lw["g_sgu"], lw["w_s"], lw["b_s"], lw["w_pa"], tm_a,
                                   tuple(casts.values()))
    lw = dict(lw, **dict(zip(casts, cast_out)))
    res = _s5_call(xb, lw["w1t"], lw["wout2"], lw["pw"], lw["d_skip"], h0p, n, nsb, emit_state)
    x1 = _mixb_call(x2, pos_tabs, mod, mod_row(tm_b), n // tm_b, lw["g_norm_mix"], res[0], pa, lw["w_in"],
                    lw["w_glu"], lw["b_glu"], lw["w_pb"], lw["w_out"], tm_b)
    return x1, (res[1] if emit_state else None), lw


def kernel(x_prompt, x_sample, state_ssm_re, state_ssm_im, c, c_ctx, w_ada, b_ada, g_norm_mix, w_in,
           g_sgu, w_spatial, b_spatial, ssm_a_re, ssm_a_im, ssm_log_dt, ssm_b_re, ssm_b_im, ssm_c_re,
           ssm_c_im, ssm_d, w_glu, b_glu, w_proj_a, w_proj_b, w_out, g_norm_mlp, w_mlp_in, w_mlp_out,
           g_final):
    bp, sp, d = x_prompt.shape
    bs, ss, _ = x_sample.shape
    depth = w_in.shape[0]
    assert depth == 1, "positional embedding and final norm are fused assuming a single trunk layer"
    da = w_proj_a.shape[1]
    db = w_proj_b.shape[1]
    g, p = ssm_a_re.shape[2], ssm_a_re.shape[3]
    tm_a, tm_b, tm_mlp = 512, 512, 512
    tf = min(2048, w_mlp_in.shape[2] // 2)

    rows = -(-(bs + 1) // 8) * 8
    cc = jnp.concatenate([c, c_ctx[None], jnp.zeros((rows - bs - 1, d), F32)], axis=0)
    pos_tabs = _pos_tables(ss, d)
    l = 0
    mod = _mod_call(cc, w_ada[l], b_ada[l]).reshape(rows, N_MOD, d)
    w1t, wout2, pw = _ssm_prepare(ssm_a_re[l], ssm_a_im[l], ssm_log_dt[l], ssm_b_re[l],
                                  ssm_b_im[l], ssm_c_re[l], ssm_c_im[l])
    wi = w_in[l]
    lw = dict(
        g_norm_mix=g_norm_mix[l].reshape(1, d),
        w_in=wi.astype(BF16), db=db,
        g_sgu=g_sgu[l].reshape(1, da),
        w_s=w_spatial[l].astype(BF16),
        b_s=b_spatial[l][:, :, None],
        w_pa=w_proj_a[l].astype(BF16),
        b_glu=b_glu[l].reshape(1, db),
        w1t=w1t, wout2=wout2, pw=pw, d_skip=ssm_d[l].reshape(1, db))

    def pairs(st):
        return jnp.transpose(st.reshape(bs, 2, g // 2, 2 * p), (1, 2, 0, 3))
    sre, sim = pairs(state_ssm_re[:, l]), pairs(state_ssm_im[:, l])
    h0p = jnp.stack([sre[0], sim[0], sre[1], sim[1]], axis=1)[None]

    ctx_row = lambda tm: (lambda i: bs)
    seq_row = lambda tm: (lambda i: i // (ss // tm))
    xp1, e, lw = _trunk(x_prompt, None, mod, ctx_row, None, lw, bp, True, tm_a, tm_b,
                        casts=dict(w_glu=w_glu[l], w_pb=w_proj_b[l], w_out=w_out[l]))
    xs1, _, lw = _trunk(x_sample, pos_tabs, mod, seq_row, h0p, lw, bs, False, tm_a, tm_b,
                        casts=dict(w1=w_mlp_in[l], w2=w_mlp_out[l]))
    w1, w2 = lw["w1"], lw["w2"]

    gf = g_final.reshape(1, d)
    gm = g_norm_mlp[l].reshape(1, d)
    xp = _mlp_call(xp1, mod, ctx_row(tm_mlp), gm, w1, w2, gf, tm_mlp, tf).reshape(bp, sp, d)
    xs = _mlp_call(xs1, mod, seq_row(tm_mlp), gm, w1, w2, gf, tm_mlp, tf).reshape(bs, ss, d)

    ev = e.reshape(g // 2, 2, 2, bp, 2, p)
    fin = jnp.transpose(ev, (2, 3, 1, 0, 4, 5)).reshape(2, bp, 2, g, p)
    return xp, xs, fin[0][:, None], fin[1][:, None]
```

```python
import functools
import math

import jax
import jax.numpy as jnp
from jax import lax
from jax.experimental import pallas as pl
from jax.experimental.pallas import tpu as pltpu

F32 = jnp.float32
BF16 = jnp.bfloat16

EPS = 1e-6
POS_BASE = 10000.0
GRID_W = 64
N_MOD = 6
T1 = 16
LANES = 128
SUBLANES = 8
VMEM_LIMIT = 63 * 1024 * 1024


def _gelu(x):
    return 0.5 * x * (1.0 + jnp.tanh(0.7978845608028654 * (x + 0.044715 * (x * x * x))))


def _sigmoid(x):
    return 1.0 / (1.0 + jnp.exp(-x))


def _rms(x, g):
    return x * lax.rsqrt(jnp.mean(x * x, axis=-1, keepdims=True) + EPS) * g


def _modnorm(x, g, shift, scale):
    return _rms(x, g) * (1.0 + scale) + shift


def _resident(shape):
    nd = len(shape)
    return pl.BlockSpec(shape, lambda *_: (0,) * nd, pipeline_mode=pl.Buffered(1))


def _params(sem):
    return pltpu.CompilerParams(dimension_semantics=sem, vmem_limit_bytes=VMEM_LIMIT)


def _add_pos(x, er_ref, ec_ref, tile_in_seq):
    tm, d = x.shape
    half = d // 2
    r0 = tile_in_seq * (tm // GRID_W)
    ec = ec_ref[...]
    parts = []
    for k in range(tm // GRID_W):
        er = jnp.broadcast_to(er_ref[pl.ds(r0 + k, 1), :], (GRID_W, half))
        parts.append(jnp.concatenate([er, ec], axis=1))
    return x + jnp.concatenate(parts, axis=0)


def _mod_kernel(c_ref, w_ref, b_ref, o_ref):
    cc = c_ref[...]
    s = cc * _sigmoid(cc)
    o_ref[...] = jnp.dot(s, w_ref[...], preferred_element_type=F32) + b_ref[...]


def _mod_call(cc, w_ada, b_ada):
    rows, d = cc.shape
    n = w_ada.shape[1]
    tn = n // (2 * N_MOD)
    assert n % tn == 0 and tn % LANES == 0
    return pl.pallas_call(
        _mod_kernel,
        grid=(n // tn,),
        in_specs=[pl.BlockSpec((rows, d), lambda j: (0, 0)),
                  pl.BlockSpec((d, tn), lambda j: (0, j)),
                  pl.BlockSpec((1, tn), lambda j: (0, j))],
        out_specs=pl.BlockSpec((rows, tn), lambda j: (0, j)),
        out_shape=jax.ShapeDtypeStruct((rows, n), F32),
        compiler_params=_params(("arbitrary",)),
        name="mod",
    )(cc, w_ada, b_ada.reshape(1, n))


def _mixa_kernel(*refs, pos_blocks, chunk, da, n_cast):
    if pos_blocks:
        x_ref, er_ref, ec_ref = refs[:3]
        x = _add_pos(x_ref[...], er_ref, ec_ref, pl.program_id(0) % pos_blocks)
        refs = refs[3:]
    else:
        x_ref = refs[0]
        x = x_ref[...]
        refs = refs[1:]
    mod_ref, g_ref, w_ref, gs_ref, ws_ref, bs_ref, wpa_ref = refs[:7]
    cast_in, (o_ref, xb_ref) = refs[7:7 + n_cast], refs[7 + n_cast:9 + n_cast]
    cast_out, (ya_ref, xs_ref) = refs[9 + n_cast:9 + 2 * n_cast], refs[9 + 2 * n_cast:]
    for ci_ref, co_ref in zip(cast_in, cast_out):
        co_ref[...] = ci_ref[...].astype(co_ref.dtype)
    tm, d = x.shape
    h = _modnorm(x, g_ref[...], mod_ref[0:1, :], mod_ref[1:2, :]).astype(BF16)
    z = jnp.dot(h, w_ref[...], preferred_element_type=F32)
    db = xs_ref.shape[0] * LANES
    for cb in range(xs_ref.shape[0]):
        c0 = 2 * da + cb * LANES
        xs_ref[cb] = z[:, c0:c0 + LANES]
        for t in range(T1):
            xb_ref[t, :, cb * LANES:(cb + 1) * LANES] = (
                xs_ref[cb, pl.ds(t, tm // T1, stride=T1), :].astype(xb_ref.dtype))
    u = _gelu(z[:, :da])
    v = _rms(_gelu(z[:, da:2 * da]), gs_ref[...]).astype(BF16)
    ng = ws_ref.shape[0]
    cg = da // ng
    for ck in range(tm // chunk):
        r0 = ck * chunk
        for gi in range(ng):
            c0 = gi * cg
            s = jnp.dot(ws_ref[gi], v[r0:r0 + chunk, c0:c0 + cg], preferred_element_type=F32)
            ya_ref[r0:r0 + chunk, c0:c0 + cg] = (u[r0:r0 + chunk, c0:c0 + cg] * (s + bs_ref[gi])).astype(BF16)
    pa = jnp.dot(ya_ref[...], wpa_ref[...], preferred_element_type=F32)
    o_ref[...] = (_sigmoid(z[:, 2 * da + db:2 * da + db + d]) * pa).astype(o_ref.dtype)


def _mixa_call(x2, pos_tabs, mod, mod_row, pos_blocks, g_norm, w_in, db, g_sgu, w_s, b_s, w_pa, tm, casts=()):
    ntok, d = x2.shape
    steps = ntok // tm
    da = w_pa.shape[0]
    chunk = w_s.shape[1]
    in_specs = [pl.BlockSpec((tm, d), lambda i: (i, 0))]
    args = [x2]
    if pos_tabs is not None:
        in_specs += [_resident(pos_tabs[0].shape), _resident(pos_tabs[1].shape)]
        args += list(pos_tabs)
    in_specs += [pl.BlockSpec((None, N_MOD, d), lambda i: (mod_row(i), 0, 0)),
                 _resident((1, d)), _resident((d, 2 * da + db + d)), _resident((1, da)),
                 _resident(w_s.shape), _resident(b_s.shape), _resident(w_pa.shape)]
    args += [mod, g_norm, w_in, g_sgu, w_s, b_s, w_pa] + list(casts)
    cast_specs = [pl.BlockSpec((w.shape[0] // steps, w.shape[1]), lambda i: (i, 0)) for w in casts]
    assert all(w.shape[0] % (steps * 2 * SUBLANES) == 0 for w in casts)
    return pl.pallas_call(
        functools.partial(_mixa_kernel, pos_blocks=pos_blocks if pos_tabs is not None else 0,
                          chunk=chunk, da=da, n_cast=len(casts)),
        grid=(steps,),
        in_specs=in_specs + cast_specs,
        out_specs=[pl.BlockSpec((tm, d), lambda i: (i, 0)),
                   pl.BlockSpec((T1, tm // T1, db), lambda i: (0, i, 0))] + cast_specs,
        out_shape=[jax.ShapeDtypeStruct((ntok, d), BF16),
                   jax.ShapeDtypeStruct((T1, ntok // T1, db), BF16)]
        + [jax.ShapeDtypeStruct(w.shape, BF16) for w in casts],
        scratch_shapes=[pltpu.VMEM((tm, da), BF16), pltpu.VMEM((db // LANES, tm, LANES), F32)],
        compiler_params=_params(("arbitrary",)),
        name="mixa",
    )(*args)


def _s5_kernel(*refs, nsb, n_chunks, has_h0, emit_state, p, ci):
    it = iter(refs)
    x_ref, w1_ref, wout_ref, pw_ref, d_ref = next(it), next(it), next(it), next(it), next(it)
    h0_refs = (next(it), next(it)) if has_h0 else None
    y_ref = next(it)
    e_refs = (next(it), next(it)) if emit_state else None
    ut_ref, s_ref, hin_ref = next(it), next(it), next(it)

    ngl, kt, cols = ut_ref.shape
    npair = ngl // 2
    pitch = s_ref.shape[2] // nsb

    for t in range(T1):
        st = x_ref[t].astype(F32).T
        for gl in range(ngl):
            ut_ref[gl, t * ci:(t + 1) * ci, :] = st[gl * ci:(gl + 1) * ci, :]

    def local(pr, carry):
        r = []
        for hlf in range(2):
            gl = 2 * pr + hlf
            r1 = jnp.dot(w1_ref[gl], ut_ref[gl].astype(BF16), preferred_element_type=F32)
            ut_ref[gl] = r1[:kt]
            r.append(r1)
        for q in range(4):
            lo = kt + q * p
            sq = jnp.concatenate([r[0][lo:lo + p], r[1][lo:lo + p]], axis=0).T
            for b in range(nsb):
                s_ref[pr, q, b * pitch:b * pitch + n_chunks, :] = sq[b * n_chunks:(b + 1) * n_chunks]
        return carry

    lax.fori_loop(0, npair, local, 0, unroll=True)

    def rows(n):
        return pl.ds(n, nsb, stride=pitch) if nsb > 1 else pl.ds(n, 1)

    def step(n, hs):
        rf, rb = rows(n), rows(n_chunks - 1 - n)
        out = []
        for pr in range(npair):
            fr, fi, br, bi = hs[4 * pr:4 * pr + 4]
            afr, afi = pw_ref[pr, 0:1, :], pw_ref[pr, 1:2, :]
            abr, abi = pw_ref[pr, 2:3, :], pw_ref[pr, 3:4, :]
            sfr, sfi = s_ref[pr, 0, rf, :], s_ref[pr, 1, rf, :]
            sbr, sbi = s_ref[pr, 2, rb, :], s_ref[pr, 3, rb, :]
            hin_ref[pr, 0, rf, :] = fr
            hin_ref[pr, 1, rf, :] = fi
            hin_ref[pr, 2, rb, :] = br
            hin_ref[pr, 3, rb, :] = bi
            out += [afr * fr - afi * fi + sfr, afr * fi + afi * fr + sfi,
                    abr * br - abi * bi + sbr, abr * bi + abi * br + sbi]
        return tuple(out)

    if has_h0:
        init = tuple(h0_refs[q % 2][:, q // 2, pr, :] for pr in range(npair) for q in range(4))
    else:
        init = tuple(jnp.zeros((nsb, LANES), F32) for _ in range(4 * npair))
    fin = lax.fori_loop(0, n_chunks, step, init, unroll=4)
    if emit_state:
        for pr in range(npair):
            for q in range(4):
                e_refs[q % 2][:, q // 2, pr, :] = fin[4 * pr + q]

    first = lax.broadcasted_iota(jnp.int32, (cols, 8 * p), 1) % LANES < p

    def inter(pr, carry):
        hin = jnp.concatenate(
            [jnp.concatenate([hin_ref[pr, q, b * pitch:b * pitch + n_chunks, :] for b in range(nsb)], axis=0)
             for q in range(4)], axis=1)
        for hlf, hg in enumerate((jnp.where(first, hin, 0.0), jnp.where(first, 0.0, hin))):
            gl = 2 * pr + hlf
            ut_ref[gl] += lax.dot_general(wout_ref[gl], hg.astype(BF16), (((1,), (1,)), ((), ())),
                                          preferred_element_type=F32)
        return carry

    lax.fori_loop(0, npair, inter, 0, unroll=True)

    for t in range(T1):
        yt = jnp.concatenate([ut_ref[gl, t * ci:(t + 1) * ci, :] for gl in range(ngl)], axis=0)
        y_ref[t] = _gelu(yt.T + d_ref[...] * x_ref[t].astype(F32)).astype(y_ref.dtype)


def _s5_call(xb, w1t, wout2, pw, d_skip, h0p, n, nsb, emit_state):
    _, ncol, db = xb.shape
    g, kt = w1t.shape[0], w1t.shape[2]
    p = (w1t.shape[1] - kt) // 4
    ci = db // g
    ngl = LANES // ci
    npair = ngl // 2
    assert 2 * p == LANES and ngl % 2 == 0
    n_chunks = n // T1
    cols = nsb * n_chunks
    has_h0 = h0p is not None
    in_specs = [pl.BlockSpec((T1, cols, LANES), lambda i, j: (0, j, i)),
                pl.BlockSpec((ngl,) + w1t.shape[1:], lambda i, j: (i, 0, 0)),
                pl.BlockSpec((ngl,) + wout2.shape[1:], lambda i, j: (i, 0, 0)),
                pl.BlockSpec((npair,) + pw.shape[1:], lambda i, j: (i, 0, 0)),
                pl.BlockSpec((1, LANES), lambda i, j: (0, i))]
    args = [xb, w1t, wout2, pw, d_skip]
    if has_h0:
        in_specs += [pl.BlockSpec((nsb, 2, None, npair, LANES), lambda i, j: (j, 0, i, 0, 0))] * 2
        args += list(h0p)
    out_specs = [pl.BlockSpec((T1, cols, LANES), lambda i, j: (0, j, i))]
    out_shape = [jax.ShapeDtypeStruct((T1, ncol, db), xb.dtype)]
    if emit_state:
        out_specs += [pl.BlockSpec((nsb, 2, None, npair, LANES), lambda i, j: (j, 0, i, 0, 0))] * 2
        out_shape += [jax.ShapeDtypeStruct((ncol // n_chunks, 2, g // ngl, npair, LANES), F32)] * 2
    return pl.pallas_call(
        functools.partial(_s5_kernel, nsb=nsb, n_chunks=n_chunks, has_h0=has_h0, emit_state=emit_state,
                          p=p, ci=ci),
        grid=(g // ngl, ncol // cols),
        in_specs=in_specs,
        out_specs=out_specs,
        out_shape=out_shape,
        scratch_shapes=[pltpu.VMEM((ngl, kt, cols), F32)]
        + [pltpu.VMEM((npair, 4, nsb * (n_chunks + SUBLANES), LANES), F32)] * 2,
        compiler_params=_params(("arbitrary", "arbitrary")),
        name="s5",
    )(*args)


def _mixb_kernel(*refs, pos_blocks, n_gb):
    if pos_blocks:
        x_ref, er_ref, ec_ref, mod_ref, g_ref, y_ref, pa_ref = refs[:7]
        x = _add_pos(x_ref[...], er_ref, ec_ref, pl.program_id(0) % pos_blocks)
        refs = refs[7:]
    else:
        x_ref, mod_ref, g_ref, y_ref, pa_ref = refs[:5]
        x = x_ref[...]
        refs = refs[5:]
    wgb_refs = refs[:n_gb]
    wglu_ref, bglu_ref, wpb_ref, wout_ref, o_ref, ys_ref = refs[n_gb:]
    tm = x.shape[0]
    h = _modnorm(x, g_ref[...], mod_ref[0:1, :], mod_ref[1:2, :]).astype(BF16)
    for cb in range(ys_ref.shape[0]):
        for t in range(T1):
            ys_ref[cb, pl.ds(t, tm // T1, stride=T1), :] = y_ref[t, :, cb * LANES:(cb + 1) * LANES].astype(F32)
    y = jnp.concatenate([ys_ref[cb] for cb in range(ys_ref.shape[0])], axis=1)
    gate = _sigmoid(jnp.dot(y.astype(BF16), wglu_ref[...], preferred_element_type=F32) + bglu_ref[...])
    yb = (y * gate).astype(BF16)
    pb = jnp.dot(yb, wpb_ref[...], preferred_element_type=F32)
    glb = jnp.concatenate([jnp.dot(h, w[...], preferred_element_type=F32) for w in wgb_refs], axis=1)
    pb = _sigmoid(glb) * pb
    m = (pa_ref[...].astype(F32) + pb).astype(BF16)
    o_ref[...] = x + mod_ref[2:3, :] * jnp.dot(m, wout_ref[...], preferred_element_type=F32)


def _mixb_call(x2, pos_tabs, mod, mod_row, pos_blocks, g_norm, y, pa, w_in, w_glu, b_glu, w_pb, w_out, tm):
    ntok, d = x2.shape
    db = y.shape[2]
    gb0 = w_in.shape[1] - d
    bw = math.gcd(gb0, d)
    n_gb = d // bw
    in_specs = [pl.BlockSpec((tm, d), lambda i: (i, 0))]
    args = [x2]
    if pos_tabs is not None:
        in_specs += [_resident(pos_tabs[0].shape), _resident(pos_tabs[1].shape)]
        args += list(pos_tabs)
    in_specs += [pl.BlockSpec((None, N_MOD, d), lambda i: (mod_row(i), 0, 0)),
                 _resident((1, d)),
                 pl.BlockSpec((T1, tm // T1, db), lambda i: (0, i, 0)),
                 pl.BlockSpec((tm, d), lambda i: (i, 0)),
                 *[pl.BlockSpec((d, bw), functools.partial(lambda k, i: (0, gb0 // bw + k), k),
                                pipeline_mode=pl.Buffered(1)) for k in range(n_gb)],
                 _resident(w_glu.shape), _resident((1, db)),
                 _resident(w_pb.shape), _resident(w_out.shape)]
    args += [mod, g_norm, y, pa] + [w_in] * n_gb + [w_glu, b_glu, w_pb, w_out]
    return pl.pallas_call(
        functools.partial(_mixb_kernel, pos_blocks=pos_blocks if pos_tabs is not None else 0, n_gb=n_gb),
        grid=(ntok // tm,),
        in_specs=in_specs,
        out_specs=pl.BlockSpec((tm, d), lambda i: (i, 0)),
        out_shape=jax.ShapeDtypeStruct((ntok, d), F32),
        scratch_shapes=[pltpu.VMEM((db // LANES, tm, LANES), F32)],
        compiler_params=_params(("arbitrary",)),
        name="mixb",
    )(*args)


def _mlp_kernel(x_ref, mod_ref, g_ref, w1_ref, w2_ref, gf_ref, o_ref, h_ref, acc_ref):
    j = pl.program_id(1)
    nj = pl.num_programs(1)

    def ffn(h):
        hid = jnp.dot(h, w1_ref[...], preferred_element_type=F32)
        hid = jnp.square(jnp.maximum(hid, 0.0)).astype(BF16)
        return jnp.dot(hid, w2_ref[...], preferred_element_type=F32)

    @pl.when(j == 0)
    def _():
        h = _modnorm(x_ref[...], g_ref[...], mod_ref[3:4, :], mod_ref[4:5, :]).astype(BF16)
        h_ref[...] = h
        acc_ref[...] = ffn(h)

    @pl.when(jnp.logical_and(j > 0, j < nj - 1))
    def _():
        acc_ref[...] += ffn(h_ref[...])

    @pl.when(j == nj - 1)
    def _():
        x2 = x_ref[...] + mod_ref[5:6, :] * (acc_ref[...] + ffn(h_ref[...]))
        o_ref[...] = _rms(x2, gf_ref[...])


def _mlp_call(x1, mod, mod_row, g_norm, w1, w2, g_final, tm, tf):
    ntok, d = x1.shape
    dff = w1.shape[1]
    assert dff // tf >= 2
    return pl.pallas_call(
        _mlp_kernel,
        grid=(ntok // tm, dff // tf),
        in_specs=[pl.BlockSpec((tm, d), lambda i, j: (i, 0)),
                  pl.BlockSpec((None, N_MOD, d), lambda i, j: (mod_row(i), 0, 0)),
                  _resident((1, d)),
                  pl.BlockSpec((d, tf), lambda i, j: (0, j)),
                  pl.BlockSpec((tf, d), lambda i, j: (j, 0)),
                  _resident((1, d))],
        out_specs=pl.BlockSpec((tm, d), lambda i, j: (i, 0)),
        out_shape=jax.ShapeDtypeStruct((ntok, d), F32),
        scratch_shapes=[pltpu.VMEM((tm, d), BF16), pltpu.VMEM((tm, d), F32)],
        compiler_params=_params(("arbitrary", "arbitrary")),
        name="mlp",
    )(x1, mod, g_norm, w1, w2, g_final)


def _pos_tables(n_tokens, d):
    rows = n_tokens // GRID_W
    quarter = d // 4
    omega = 1.0 / (POS_BASE ** (jnp.arange(quarter, dtype=F32) / quarter))
    r = jnp.arange(rows, dtype=F32)[:, None] * omega
    col = jnp.arange(GRID_W, dtype=F32)[:, None] * omega
    e_r = jnp.concatenate([jnp.sin(r), jnp.cos(r)], axis=-1)
    e_c = jnp.concatenate([jnp.sin(col), jnp.cos(col)], axis=-1)
    return e_r, e_c


def _cmul(ar, ai, br, bi):
    return ar * br - ai * bi, ar * bi + ai * br


def _ssm_prepare(a_re, a_im, log_dt, b_re, b_im, c_re, c_im):
    ndir, g, p = a_re.shape
    ci = b_re.shape[-1]
    kt = T1 * ci
    hi = lax.Precision.HIGHEST
    dt = jnp.exp(log_dt)[..., None]
    lr, ang = a_re * dt, a_im * dt

    def power(j, lr_, ang_):
        m = jnp.exp(j * lr_)
        return m * jnp.cos(j * ang_), m * jnp.sin(j * ang_)

    ab_re, ab_im = power(1.0, lr, ang)
    den = a_re * a_re + a_im * a_im
    f_re, f_im = _cmul(ab_re - 1.0, ab_im, a_re / den, -a_im / den)
    bb_re, bb_im = _cmul(f_re[..., None], f_im[..., None], b_re, b_im)

    tt = jnp.arange(T1, dtype=F32)
    j_in = jnp.stack([T1 - 1 - tt, tt])[:, None, None, :]
    pl_re, pl_im = power(j_in, lr[..., None], ang[..., None])
    gm = lambda w: jnp.swapaxes(w, 0, 1)
    ba = jnp.stack(_cmul(jnp.repeat(gm(pl_re), ci, axis=-1), jnp.repeat(gm(pl_im), ci, axis=-1),
                         jnp.tile(gm(bb_re), (1, 1, 1, T1)), jnp.tile(gm(bb_im), (1, 1, 1, T1))),
                   axis=2)
    wend_t = ba.reshape(g, 4 * p, kt)

    cs = jnp.stack([gm(c_re), -gm(c_im)], axis=2)
    kk = jnp.einsum('gdxcp,gdxpl->dgcl', cs, ba, precision=hi)
    kf, kb = kk[0], kk[1]
    mid = (T1 - 1) * ci
    r = jnp.concatenate([kf[..., :mid], kf[..., mid:] + kb[..., :ci], kb[..., ci:]], axis=-1)
    toep_t = jnp.stack([r[..., (T1 - 1 - t) * ci:(2 * T1 - 1 - t) * ci] for t in range(T1)], axis=1)
    toep_t = toep_t.reshape(g, kt, kt)
    w1t = jnp.concatenate([toep_t, wend_t], axis=1).astype(BF16)

    dup = lambda w: jnp.concatenate([w, w], axis=-1)

    def wout(dd, j):
        po_re, po_im = power(j[None, :, None], dup(lr[dd])[:, None, :], dup(ang[dd])[:, None, :])
        ca_re, ca_im = _cmul(dup(c_re[dd])[:, None], dup(c_im[dd])[:, None], po_re[:, :, None], po_im[:, :, None])
        return ca_re.reshape(g, kt, 2 * p), (-ca_im).reshape(g, kt, 2 * p)
    wout2 = jnp.concatenate(wout(0, tt + 1.0) + wout(1, T1 - tt), axis=-1).astype(BF16)

    pt_re, pt_im = power(float(T1), lr, ang)
    pw = jnp.stack([pt_re[0], pt_im[0], pt_re[1], pt_im[1]], axis=0)
    pw = jnp.transpose(pw.reshape(4, g // 2, 2 * p), (1, 0, 2))
    return w1t, wout2, pw


def _trunk(x, pos_tabs, mod, mod_row, h0p, lw, nsb, emit_state, tm_a, tm_b, casts):
    bsz, n, d = x.shape
    x2 = x.reshape(bsz * n, d)
    pa, xb, *cast_out = _mixa_call(x2, pos_tabs, mod, mod_row(tm_a), n // tm_a, lw["g_norm_mix"], lw["w_in"],
                                   lw["db"], lw["g_sgu"], lw["w_s"], lw["b_s"], lw["w_pa"], tm_a,
                                   tuple(casts.values()))
    lw = dict(lw, **dict(zip(casts, cast_out)))
    res = _s5_call(xb, lw["w1t"], lw["wout2"], lw["pw"], lw["d_skip"], h0p, n, nsb, emit_state)
    x1 = _mixb_call(x2, pos_tabs, mod, mod_row(tm_b), n // tm_b, lw["g_norm_mix"], res[0], pa, lw["w_in"],
                    lw["w_glu"], lw["b_glu"], lw["w_pb"], lw["w_out"], tm_b)
    return x1, (res[1:3] if emit_state else None), lw


def kernel(x_prompt, x_sample, state_ssm_re, state_ssm_im, c, c_ctx, w_ada, b_ada, g_norm_mix, w_in,
           g_sgu, w_spatial, b_spatial, ssm_a_re, ssm_a_im, ssm_log_dt, ssm_b_re, ssm_b_im, ssm_c_re,
           ssm_c_im, ssm_d, w_glu, b_glu, w_proj_a, w_proj_b, w_out, g_norm_mlp, w_mlp_in, w_mlp_out,
           g_final):
    bp, sp, d = x_prompt.shape
    bs, ss, _ = x_sample.shape
    depth = w_in.shape[0]
    assert depth == 1, "positional embedding and final norm are fused assuming a single trunk layer"
    da = w_proj_a.shape[1]
    db = w_proj_b.shape[1]
    g, p = ssm_a_re.shape[2], ssm_a_re.shape[3]
    tm_a, tm_b, tm_mlp = 512, 512, 512
    tf = min(2048, w_mlp_in.shape[2] // 2)

    rows = -(-(bs + 1) // 8) * 8
    cc = jnp.concatenate([c, c_ctx[None], jnp.zeros((rows - bs - 1, d), F32)], axis=0)
    pos_tabs = _pos_tables(ss, d)
    l = 0
    mod = _mod_call(cc, w_ada[l], b_ada[l]).reshape(rows, N_MOD, d)
    w1t, wout2, pw = _ssm_prepare(ssm_a_re[l], ssm_a_im[l], ssm_log_dt[l], ssm_b_re[l],
                                  ssm_b_im[l], ssm_c_re[l], ssm_c_im[l])
    wi = w_in[l]
    lw = dict(
        g_norm_mix=g_norm_mix[l].reshape(1, d),
        w_in=wi.astype(BF16), db=db,
        g_sgu=g_sgu[l].reshape(1, da),
        w_s=w_spatial[l].astype(BF16),
        b_s=b_spatial[l][:, :, None],
        w_pa=w_proj_a[l].astype(BF16),
        b_glu=b_glu[l].reshape(1, db),
        w1t=w1t, wout2=wout2, pw=pw, d_skip=ssm_d[l].reshape(1, db))

    h0p = tuple(st[:, l].reshape(bs, 2, g // 8, 4, 2 * p) for st in (state_ssm_re, state_ssm_im))

    ctx_row = lambda tm: (lambda i: bs)
    seq_row = lambda tm: (lambda i: i // (ss // tm))
    xp1, e, lw = _trunk(x_prompt, None, mod, ctx_row, None, lw, bp, True, tm_a, tm_b,
                        casts=dict(w_glu=w_glu[l], w_pb=w_proj_b[l], w_out=w_out[l]))
    xs1, _, lw = _trunk(x_sample, pos_tabs, mod, seq_row, h0p, lw, bs, False, tm_a, tm_b,
                        casts=dict(w1=w_mlp_in[l], w2=w_mlp_out[l]))
    w1, w2 = lw["w1"], lw["w2"]

    gf = g_final.reshape(1, d)
    gm = g_norm_mlp[l].reshape(1, d)
    xp = _mlp_call(xp1, mod, ctx_row(tm_mlp), gm, w1, w2, gf, tm_mlp, tf).reshape(bp, sp, d)
    xs = _mlp_call(xs1, mod, seq_row(tm_mlp), gm, w1, w2, gf, tm_mlp, tf).reshape(bs, ss, d)

    return xp, xs, e[0].reshape(bp, 1, 2, g, p), e[1].reshape(bp, 1, 2, g, p)
```

```python
import functools
import math

import jax
import jax.numpy as jnp
from jax import lax
from jax.experimental import pallas as pl
from jax.experimental.pallas import tpu as pltpu

F32 = jnp.float32
BF16 = jnp.bfloat16

EPS = 1e-6
POS_BASE = 10000.0
GRID_W = 64
N_MOD = 6
T1 = 16
LANES = 128
SUBLANES = 8
VMEM_LIMIT = 63 * 1024 * 1024


def _gelu(x):
    return 0.5 * x * (1.0 + jnp.tanh(0.7978845608028654 * (x + 0.044715 * (x * x * x))))


def _sigmoid(x):
    return 1.0 / (1.0 + jnp.exp(-x))


def _rms(x, g):
    return x * lax.rsqrt(jnp.mean(x * x, axis=-1, keepdims=True) + EPS) * g


def _modnorm(x, g, shift, scale):
    return _rms(x, g) * (1.0 + scale) + shift


def _resident(shape):
    nd = len(shape)
    return pl.BlockSpec(shape, lambda *_: (0,) * nd, pipeline_mode=pl.Buffered(1))


def _params(sem, fuse_inputs=None):
    return pltpu.CompilerParams(dimension_semantics=sem, vmem_limit_bytes=VMEM_LIMIT, allow_input_fusion=fuse_inputs)


def _add_pos(x, er_ref, ec_ref, tile_in_seq):
    tm, d = x.shape
    half = d // 2
    r0 = tile_in_seq * (tm // GRID_W)
    ec = ec_ref[...]
    parts = []
    for k in range(tm // GRID_W):
        er = jnp.broadcast_to(er_ref[pl.ds(r0 + k, 1), :], (GRID_W, half))
        parts.append(jnp.concatenate([er, ec], axis=1))
    return x + jnp.concatenate(parts, axis=0)


def _mod_kernel(c_ref, w_ref, b_ref, o_ref):
    cc = c_ref[...]
    s = cc * _sigmoid(cc)
    o_ref[...] = jnp.dot(s, w_ref[...], preferred_element_type=F32) + b_ref[...]


def _mod_call(cc, w_ada, b_ada):
    rows, d = cc.shape
    n = w_ada.shape[1]
    tn = n // (2 * N_MOD)
    assert n % tn == 0 and tn % LANES == 0
    return pl.pallas_call(
        _mod_kernel,
        grid=(n // tn,),
        in_specs=[pl.BlockSpec((rows, d), lambda j: (0, 0)),
                  pl.BlockSpec((d, tn), lambda j: (0, j)),
                  pl.BlockSpec((1, tn), lambda j: (0, j))],
        out_specs=pl.BlockSpec((rows, tn), lambda j: (0, j)),
        out_shape=jax.ShapeDtypeStruct((rows, n), F32),
        compiler_params=_params(("arbitrary",)),
        name="mod",
    )(cc, w_ada, b_ada.reshape(1, n))


def _mixa_kernel(*refs, pos_blocks, chunk, da, n_cast):
    if pos_blocks:
        x_ref, er_ref, ec_ref = refs[:3]
        x = _add_pos(x_ref[...], er_ref, ec_ref, pl.program_id(0) % pos_blocks)
        refs = refs[3:]
    else:
        x_ref = refs[0]
        x = x_ref[...]
        refs = refs[1:]
    mod_ref, g_ref, w_ref, gs_ref, ws_ref, bs_ref, wpa_ref = refs[:7]
    cast_in, (o_ref, xb_ref) = refs[7:7 + n_cast], refs[7 + n_cast:9 + n_cast]
    cast_out, (ya_ref, xs_ref) = refs[9 + n_cast:9 + 2 * n_cast], refs[9 + 2 * n_cast:]
    for ci_ref, co_ref in zip(cast_in, cast_out):
        co_ref[...] = ci_ref[...].astype(co_ref.dtype)
    tm, d = x.shape
    h = _modnorm(x, g_ref[...], mod_ref[0:1, :], mod_ref[1:2, :]).astype(BF16)
    z = jnp.dot(h, w_ref[...], preferred_element_type=F32)
    db = xs_ref.shape[0] * LANES
    for cb in range(xs_ref.shape[0]):
        c0 = 2 * da + cb * LANES
        xs_ref[cb] = z[:, c0:c0 + LANES]
        for t in range(T1):
            xb_ref[t, :, cb * LANES:(cb + 1) * LANES] = (
                xs_ref[cb, pl.ds(t, tm // T1, stride=T1), :].astype(xb_ref.dtype))
    u = _gelu(z[:, :da])
    v = _rms(_gelu(z[:, da:2 * da]), gs_ref[...]).astype(BF16)
    ng = ws_ref.shape[0]
    cg = da // ng
    for ck in range(tm // chunk):
        r0 = ck * chunk
        for gi in range(ng):
            c0 = gi * cg
            s = jnp.dot(ws_ref[gi], v[r0:r0 + chunk, c0:c0 + cg], preferred_element_type=F32)
            ya_ref[r0:r0 + chunk, c0:c0 + cg] = (u[r0:r0 + chunk, c0:c0 + cg] * (s + bs_ref[gi])).astype(BF16)
    pa = jnp.dot(ya_ref[...], wpa_ref[...], preferred_element_type=F32)
    o_ref[...] = (_sigmoid(z[:, 2 * da + db:2 * da + db + d]) * pa).astype(o_ref.dtype)


def _mixa_call(x2, pos_tabs, mod, mod_row, pos_blocks, g_norm, w_in, db, g_sgu, w_s, b_s, w_pa, tm, casts=()):
    ntok, d = x2.shape
    steps = ntok // tm
    da = w_pa.shape[0]
    chunk = w_s.shape[1]
    in_specs = [pl.BlockSpec((tm, d), lambda i: (i, 0))]
    args = [x2]
    if pos_tabs is not None:
        in_specs += [_resident(pos_tabs[0].shape), _resident(pos_tabs[1].shape)]
        args += list(pos_tabs)
    in_specs += [pl.BlockSpec((None, N_MOD, d), lambda i: (mod_row(i), 0, 0)),
                 _resident((1, d)), _resident((d, 2 * da + db + d)), _resident((1, da)),
                 _resident(w_s.shape), _resident(b_s.shape), _resident(w_pa.shape)]
    args += [mod, g_norm, w_in, g_sgu, w_s, b_s, w_pa] + list(casts)
    fuse = [a is w_s or a is w_pa for a in args]
    cast_specs = [pl.BlockSpec((w.shape[0] // steps, w.shape[1]), lambda i: (i, 0)) for w in casts]
    assert all(w.shape[0] % (steps * 2 * SUBLANES) == 0 for w in casts)
    return pl.pallas_call(
        functools.partial(_mixa_kernel, pos_blocks=pos_blocks if pos_tabs is not None else 0,
                          chunk=chunk, da=da, n_cast=len(casts)),
        grid=(steps,),
        in_specs=in_specs + cast_specs,
        out_specs=[pl.BlockSpec((tm, d), lambda i: (i, 0)),
                   pl.BlockSpec((T1, tm // T1, db), lambda i: (0, i, 0))] + cast_specs,
        out_shape=[jax.ShapeDtypeStruct((ntok, d), BF16),
                   jax.ShapeDtypeStruct((T1, ntok // T1, db), BF16)]
        + [jax.ShapeDtypeStruct(w.shape, BF16) for w in casts],
        scratch_shapes=[pltpu.VMEM((tm, da), BF16), pltpu.VMEM((db // LANES, tm, LANES), F32)],
        compiler_params=_params(("arbitrary",), fuse),
        name="mixa",
    )(*args)


def _s5_kernel(*refs, nsb, n_chunks, has_h0, emit_state, p, ci):
    it = iter(refs)
    x_ref, w1_ref, wout_ref, pw_ref, d_ref = next(it), next(it), next(it), next(it), next(it)
    h0_refs = (next(it), next(it)) if has_h0 else None
    y_ref = next(it)
    e_refs = (next(it), next(it)) if emit_state else None
    ut_ref, s_ref, hin_ref = next(it), next(it), next(it)

    ngl, kt, cols = ut_ref.shape
    npair = ngl // 2
    pitch = s_ref.shape[2] // nsb

    for t in range(T1):
        st = x_ref[t].astype(F32).T
        for gl in range(ngl):
            ut_ref[gl, t * ci:(t + 1) * ci, :] = st[gl * ci:(gl + 1) * ci, :]

    def local(pr, carry):
        r = []
        for hlf in range(2):
            gl = 2 * pr + hlf
            r1 = jnp.dot(w1_ref[gl], ut_ref[gl].astype(BF16), preferred_element_type=F32)
            ut_ref[gl] = r1[:kt]
            r.append(r1)
        for q in range(4):
            lo = kt + q * p
            sq = jnp.concatenate([r[0][lo:lo + p], r[1][lo:lo + p]], axis=0).T
            for b in range(nsb):
                s_ref[pr, q, b * pitch:b * pitch + n_chunks, :] = sq[b * n_chunks:(b + 1) * n_chunks]
        return carry

    lax.fori_loop(0, npair, local, 0, unroll=True)

    def rows(n):
        return pl.ds(n, nsb, stride=pitch) if nsb > 1 else pl.ds(n, 1)

    def step(n, hs):
        rf, rb = rows(n), rows(n_chunks - 1 - n)
        out = []
        for pr in range(npair):
            fr, fi, br, bi = hs[4 * pr:4 * pr + 4]
            afr, afi = pw_ref[pr, 0:1, :], pw_ref[pr, 1:2, :]
            abr, abi = pw_ref[pr, 2:3, :], pw_ref[pr, 3:4, :]
            sfr, sfi = s_ref[pr, 0, rf, :], s_ref[pr, 1, rf, :]
            sbr, sbi = s_ref[pr, 2, rb, :], s_ref[pr, 3, rb, :]
            hin_ref[pr, 0, rf, :] = fr
            hin_ref[pr, 1, rf, :] = fi
            hin_ref[pr, 2, rb, :] = br
            hin_ref[pr, 3, rb, :] = bi
            out += [afr * fr - afi * fi + sfr, afr * fi + afi * fr + sfi,
                    abr * br - abi * bi + sbr, abr * bi + abi * br + sbi]
        return tuple(out)

    if has_h0:
        init = tuple(h0_refs[q % 2][:, q // 2, pr, :] for pr in range(npair) for q in range(4))
    else:
        init = tuple(jnp.zeros((nsb, LANES), F32) for _ in range(4 * npair))
    fin = lax.fori_loop(0, n_chunks, step, init, unroll=4)
    if emit_state:
        for pr in range(npair):
            for q in range(4):
                e_refs[q % 2][:, q // 2, pr, :] = fin[4 * pr + q]

    first = lax.broadcasted_iota(jnp.int32, (cols, 8 * p), 1) % LANES < p

    def inter(pr, carry):
        hin = jnp.concatenate(
            [jnp.concatenate([hin_ref[pr, q, b * pitch:b * pitch + n_chunks, :] for b in range(nsb)], axis=0)
             for q in range(4)], axis=1)
        for hlf, hg in enumerate((jnp.where(first, hin, 0.0), jnp.where(first, 0.0, hin))):
            gl = 2 * pr + hlf
            ut_ref[gl] += lax.dot_general(wout_ref[gl], hg.astype(BF16), (((1,), (1,)), ((), ())),
                                          preferred_element_type=F32)
        return carry

    lax.fori_loop(0, npair, inter, 0, unroll=True)

    for t in range(T1):
        yt = jnp.concatenate([ut_ref[gl, t * ci:(t + 1) * ci, :] for gl in range(ngl)], axis=0)
        y_ref[t] = _gelu(yt.T + d_ref[...] * x_ref[t].astype(F32)).astype(y_ref.dtype)


def _s5_call(xb, w1t, wout2, pw, d_skip, h0p, n, nsb, emit_state):
    _, ncol, db = xb.shape
    g, kt = w1t.shape[0], w1t.shape[2]
    p = (w1t.shape[1] - kt) // 4
    ci = db // g
    ngl = LANES // ci
    npair = ngl // 2
    assert 2 * p == LANES and ngl % 2 == 0
    n_chunks = n // T1
    cols = nsb * n_chunks
    has_h0 = h0p is not None
    in_specs = [pl.BlockSpec((T1, cols, LANES), lambda i, j: (0, j, i)),
                pl.BlockSpec((ngl,) + w1t.shape[1:], lambda i, j: (i, 0, 0)),
                pl.BlockSpec((ngl,) + wout2.shape[1:], lambda i, j: (i, 0, 0)),
                pl.BlockSpec((npair,) + pw.shape[1:], lambda i, j: (i, 0, 0)),
                pl.BlockSpec((1, LANES), lambda i, j: (0, i))]
    args = [xb, w1t, wout2, pw, d_skip]
    if has_h0:
        in_specs += [pl.BlockSpec((nsb, 2, None, npair, LANES), lambda i, j: (j, 0, i, 0, 0))] * 2
        args += list(h0p)
    out_specs = [pl.BlockSpec((T1, cols, LANES), lambda i, j: (0, j, i))]
    out_shape = [jax.ShapeDtypeStruct((T1, ncol, db), xb.dtype)]
    if emit_state:
        out_specs += [pl.BlockSpec((nsb, 2, None, npair, LANES), lambda i, j: (j, 0, i, 0, 0))] * 2
        out_shape += [jax.ShapeDtypeStruct((ncol // n_chunks, 2, g // ngl, npair, LANES), F32)] * 2
    return pl.pallas_call(
        functools.partial(_s5_kernel, nsb=nsb, n_chunks=n_chunks, has_h0=has_h0, emit_state=emit_state,
                          p=p, ci=ci),
        grid=(g // ngl, ncol // cols),
        in_specs=in_specs,
        out_specs=out_specs,
        out_shape=out_shape,
        scratch_shapes=[pltpu.VMEM((ngl, kt, cols), F32)]
        + [pltpu.VMEM((npair, 4, nsb * (n_chunks + SUBLANES), LANES), F32)] * 2,
        compiler_params=_params(("arbitrary", "arbitrary")),
        name="s5",
    )(*args)


def _mixb_kernel(*refs, pos_blocks, n_gb):
    if pos_blocks:
        x_ref, er_ref, ec_ref, mod_ref, g_ref, y_ref, pa_ref = refs[:7]
        x = _add_pos(x_ref[...], er_ref, ec_ref, pl.program_id(0) % pos_blocks)
        refs = refs[7:]
    else:
        x_ref, mod_ref, g_ref, y_ref, pa_ref = refs[:5]
        x = x_ref[...]
        refs = refs[5:]
    wgb_refs = refs[:n_gb]
    wglu_ref, bglu_ref, wpb_ref, wout_ref, o_ref, ys_ref = refs[n_gb:]
    tm = x.shape[0]
    h = _modnorm(x, g_ref[...], mod_ref[0:1, :], mod_ref[1:2, :]).astype(BF16)
    for cb in range(ys_ref.shape[0]):
        for t in range(T1):
            ys_ref[cb, pl.ds(t, tm // T1, stride=T1), :] = y_ref[t, :, cb * LANES:(cb + 1) * LANES].astype(F32)
    y = jnp.concatenate([ys_ref[cb] for cb in range(ys_ref.shape[0])], axis=1)
    gate = _sigmoid(jnp.dot(y.astype(BF16), wglu_ref[...], preferred_element_type=F32) + bglu_ref[...])
    yb = (y * gate).astype(BF16)
    pb = jnp.dot(yb, wpb_ref[...], preferred_element_type=F32)
    glb = jnp.concatenate([jnp.dot(h, w[...], preferred_element_type=F32) for w in wgb_refs], axis=1)
    pb = _sigmoid(glb) * pb
    m = (pa_ref[...].astype(F32) + pb).astype(BF16)
    o_ref[...] = x + mod_ref[2:3, :] * jnp.dot(m, wout_ref[...], preferred_element_type=F32)


def _mixb_call(x2, pos_tabs, mod, mod_row, pos_blocks, g_norm, y, pa, w_in, w_glu, b_glu, w_pb, w_out, tm):
    ntok, d = x2.shape
    db = y.shape[2]
    gb0 = w_in.shape[1] - d
    bw = math.gcd(gb0, d)
    n_gb = d // bw
    in_specs = [pl.BlockSpec((tm, d), lambda i: (i, 0))]
    args = [x2]
    if pos_tabs is not None:
        in_specs += [_resident(pos_tabs[0].shape), _resident(pos_tabs[1].shape)]
        args += list(pos_tabs)
    in_specs += [pl.BlockSpec((None, N_MOD, d), lambda i: (mod_row(i), 0, 0)),
                 _resident((1, d)),
                 pl.BlockSpec((T1, tm // T1, db), lambda i: (0, i, 0)),
                 pl.BlockSpec((tm, d), lambda i: (i, 0)),
                 *[pl.BlockSpec((d, bw), functools.partial(lambda k, i: (0, gb0 // bw + k), k),
                                pipeline_mode=pl.Buffered(1)) for k in range(n_gb)],
                 _resident(w_glu.shape), _resident((1, db)),
                 _resident(w_pb.shape), _resident(w_out.shape)]
    args += [mod, g_norm, y, pa] + [w_in] * n_gb + [w_glu, b_glu, w_pb, w_out]
    return pl.pallas_call(
        functools.partial(_mixb_kernel, pos_blocks=pos_blocks if pos_tabs is not None else 0, n_gb=n_gb),
        grid=(ntok // tm,),
        in_specs=in_specs,
        out_specs=pl.BlockSpec((tm, d), lambda i: (i, 0)),
        out_shape=jax.ShapeDtypeStruct((ntok, d), F32),
        scratch_shapes=[pltpu.VMEM((db // LANES, tm, LANES), F32)],
        compiler_params=_params(("arbitrary",)),
        name="mixb",
    )(*args)


def _mlp_kernel(x_ref, mod_ref, g_ref, w1_ref, w2_ref, gf_ref, o_ref, h_ref, acc_ref):
    j = pl.program_id(1)
    nj = pl.num_programs(1)

    def ffn(h):
        hid = jnp.dot(h, w1_ref[...], preferred_element_type=F32)
        hid = jnp.square(jnp.maximum(hid, 0.0)).astype(BF16)
        return jnp.dot(hid, w2_ref[...], preferred_element_type=F32)

    @pl.when(j == 0)
    def _():
        h = _modnorm(x_ref[...], g_ref[...], mod_ref[3:4, :], mod_ref[4:5, :]).astype(BF16)
        h_ref[...] = h
        acc_ref[...] = ffn(h)

    @pl.when(jnp.logical_and(j > 0, j < nj - 1))
    def _():
        acc_ref[...] += ffn(h_ref[...])

    @pl.when(j == nj - 1)
    def _():
        x2 = x_ref[...] + mod_ref[5:6, :] * (acc_ref[...] + ffn(h_ref[...]))
        o_ref[...] = _rms(x2, gf_ref[...])


def _mlp_call(x1, mod, mod_row, g_norm, w1, w2, g_final, tm, tf):
    ntok, d = x1.shape
    dff = w1.shape[1]
    assert dff // tf >= 2
    return pl.pallas_call(
        _mlp_kernel,
        grid=(ntok // tm, dff // tf),
        in_specs=[pl.BlockSpec((tm, d), lambda i, j: (i, 0)),
                  pl.BlockSpec((None, N_MOD, d), lambda i, j: (mod_row(i), 0, 0)),
                  _resident((1, d)),
                  pl.BlockSpec((d, tf), lambda i, j: (0, j)),
                  pl.BlockSpec((tf, d), lambda i, j: (j, 0)),
                  _resident((1, d))],
        out_specs=pl.BlockSpec((tm, d), lambda i, j: (i, 0)),
        out_shape=jax.ShapeDtypeStruct((ntok, d), F32),
        scratch_shapes=[pltpu.VMEM((tm, d), BF16), pltpu.VMEM((tm, d), F32)],
        compiler_params=_params(("arbitrary", "arbitrary")),
        name="mlp",
    )(x1, mod, g_norm, w1, w2, g_final)


def _pos_tables(n_tokens, d):
    rows = n_tokens // GRID_W
    quarter = d // 4
    omega = 1.0 / (POS_BASE ** (jnp.arange(quarter, dtype=F32) / quarter))
    r = jnp.arange(rows, dtype=F32)[:, None] * omega
    col = jnp.arange(GRID_W, dtype=F32)[:, None] * omega
    e_r = jnp.concatenate([jnp.sin(r), jnp.cos(r)], axis=-1)
    e_c = jnp.concatenate([jnp.sin(col), jnp.cos(col)], axis=-1)
    return e_r, e_c


def _cmul(ar, ai, br, bi):
    return ar * br - ai * bi, ar * bi + ai * br


def _ssm_prepare(a_re, a_im, log_dt, b_re, b_im, c_re, c_im):
    ndir, g, p = a_re.shape
    ci = b_re.shape[-1]
    kt = T1 * ci
    hi = lax.Precision.HIGHEST
    dt = jnp.exp(log_dt)[..., None]
    lr, ang = a_re * dt, a_im * dt

    def power(j, lr_, ang_):
        m = jnp.exp(j * lr_)
        return m * jnp.cos(j * ang_), m * jnp.sin(j * ang_)

    ab_re, ab_im = power(1.0, lr, ang)
    den = a_re * a_re + a_im * a_im
    f_re, f_im = _cmul(ab_re - 1.0, ab_im, a_re / den, -a_im / den)
    bb_re, bb_im = _cmul(f_re[..., None], f_im[..., None], b_re, b_im)

    tt = jnp.arange(T1, dtype=F32)
    j_in = jnp.stack([T1 - 1 - tt, tt])[:, None, None, :]
    pl_re, pl_im = power(j_in, lr[..., None], ang[..., None])
    gm = lambda w: jnp.swapaxes(w, 0, 1)
    ba = jnp.stack(_cmul(jnp.repeat(gm(pl_re), ci, axis=-1), jnp.repeat(gm(pl_im), ci, axis=-1),
                         jnp.tile(gm(bb_re), (1, 1, 1, T1)), jnp.tile(gm(bb_im), (1, 1, 1, T1))),
                   axis=2)
    wend_t = ba.reshape(g, 4 * p, kt)

    cs = jnp.stack([gm(c_re), -gm(c_im)], axis=2)
    kk = jnp.einsum('gdxcp,gdxpl->dgcl', cs, ba, precision=hi)
    kf, kb = kk[0], kk[1]
    mid = (T1 - 1) * ci
    r = jnp.concatenate([kf[..., :mid], kf[..., mid:] + kb[..., :ci], kb[..., ci:]], axis=-1)
    toep_t = jnp.stack([r[..., (T1 - 1 - t) * ci:(2 * T1 - 1 - t) * ci] for t in range(T1)], axis=1)
    toep_t = toep_t.reshape(g, kt, kt)
    w1t = jnp.concatenate([toep_t, wend_t], axis=1).astype(BF16)

    dup = lambda w: jnp.concatenate([w, w], axis=-1)

    def wout(dd, j):
        po_re, po_im = power(j[None, :, None], dup(lr[dd])[:, None, :], dup(ang[dd])[:, None, :])
        ca_re, ca_im = _cmul(dup(c_re[dd])[:, None], dup(c_im[dd])[:, None], po_re[:, :, None], po_im[:, :, None])
        return ca_re.reshape(g, kt, 2 * p), (-ca_im).reshape(g, kt, 2 * p)
    wout2 = jnp.concatenate(wout(0, tt + 1.0) + wout(1, T1 - tt), axis=-1).astype(BF16)

    pt_re, pt_im = power(float(T1), lr, ang)
    pw = jnp.stack([pt_re[0], pt_im[0], pt_re[1], pt_im[1]], axis=0)
    pw = jnp.transpose(pw.reshape(4, g // 2, 2 * p), (1, 0, 2))
    return w1t, wout2, pw


def _trunk(x, pos_tabs, mod, mod_row, h0p, lw, nsb, emit_state, tm_a, tm_b, casts):
    bsz, n, d = x.shape
    x2 = x.reshape(bsz * n, d)
    pa, xb, *cast_out = _mixa_call(x2, pos_tabs, mod, mod_row(tm_a), n // tm_a, lw["g_norm_mix"], lw["w_in"],
                                   lw["db"], lw["g_sgu"], lw["w_s"], lw["b_s"], lw["w_pa"], tm_a,
                                   tuple(casts.values()))
    lw = dict(lw, **dict(zip(casts, cast_out)))
    res = _s5_call(xb, lw["w1t"], lw["wout2"], lw["pw"], lw["d_skip"], h0p, n, nsb, emit_state)
    x1 = _mixb_call(x2, pos_tabs, mod, mod_row(tm_b), n // tm_b, lw["g_norm_mix"], res[0], pa, lw["w_in"],
                    lw["w_glu"], lw["b_glu"], lw["w_pb"], lw["w_out"], tm_b)
    return x1, (res[1:3] if emit_state else None), lw


def kernel(x_prompt, x_sample, state_ssm_re, state_ssm_im, c, c_ctx, w_ada, b_ada, g_norm_mix, w_in,
           g_sgu, w_spatial, b_spatial, ssm_a_re, ssm_a_im, ssm_log_dt, ssm_b_re, ssm_b_im, ssm_c_re,
           ssm_c_im, ssm_d, w_glu, b_glu, w_proj_a, w_proj_b, w_out, g_norm_mlp, w_mlp_in, w_mlp_out,
           g_final):
    bp, sp, d = x_prompt.shape
    bs, ss, _ = x_sample.shape
    depth = w_in.shape[0]
    assert depth == 1, "positional embedding and final norm are fused assuming a single trunk layer"
    da = w_proj_a.shape[1]
    db = w_proj_b.shape[1]
    g, p = ssm_a_re.shape[2], ssm_a_re.shape[3]
    tm_a, tm_b, tm_mlp = 512, 512, 512
    tf = min(2048, w_mlp_in.shape[2] // 2)

    rows = -(-(bs + 1) // 8) * 8
    cc = jnp.concatenate([c, c_ctx[None], jnp.zeros((rows - bs - 1, d), F32)], axis=0)
    pos_tabs = _pos_tables(ss, d)
    l = 0
    mod = _mod_call(cc, w_ada[l], b_ada[l]).reshape(rows, N_MOD, d)
    w1t, wout2, pw = _ssm_prepare(ssm_a_re[l], ssm_a_im[l], ssm_log_dt[l], ssm_b_re[l],
                                  ssm_b_im[l], ssm_c_re[l], ssm_c_im[l])
    wi = w_in[l]
    lw = dict(
        g_norm_mix=g_norm_mix[l].reshape(1, d),
        w_in=wi.astype(BF16), db=db,
        g_sgu=g_sgu[l].reshape(1, da),
        w_s=w_spatial[l].astype(BF16),
        b_s=b_spatial[l][:, :, None],
        w_pa=w_proj_a[l].astype(BF16),
        b_glu=b_glu[l].reshape(1, db),
        w1t=w1t, wout2=wout2, pw=pw, d_skip=ssm_d[l].reshape(1, db))

    h0p = tuple(st[:, l].reshape(bs, 2, g // 8, 4, 2 * p) for st in (state_ssm_re, state_ssm_im))

    ctx_row = lambda tm: (lambda i: bs)
    seq_row = lambda tm: (lambda i: i // (ss // tm))
    xp1, e, lw = _trunk(x_prompt, None, mod, ctx_row, None, lw, bp, True, tm_a, tm_b,
                        casts=dict(w_glu=w_glu[l], w_pb=w_proj_b[l], w_out=w_out[l]))
    xs1, _, lw = _trunk(x_sample, pos_tabs, mod, seq_row, h0p, lw, bs, False, tm_a, tm_b,
                        casts=dict(w1=w_mlp_in[l], w2=w_mlp_out[l]))
    w1, w2 = lw["w1"], lw["w2"]

    gf = g_final.reshape(1, d)
    gm = g_norm_mlp[l].reshape(1, d)
    xp = _mlp_call(xp1, mod, ctx_row(tm_mlp), gm, w1, w2, gf, tm_mlp, tf).reshape(bp, sp, d)
    xs = _mlp_call(xs1, mod, seq_row(tm_mlp), gm, w1, w2, gf, tm_mlp, tf).reshape(bs, ss, d)

    return xp, xs, e[0].reshape(bp, 1, 2, g, p), e[1].reshape(bp, 1, 2, g, p)
```
